```python
import math
import jax
import jax.numpy as jnp
from jax import lax
import numpy as np

D_MODEL = 2048
BATCH = 4
SEQ = 4096
DEPTH = 4

MEM_LEN = 256
N_EVEN = (DEPTH + 1) // 2
N_ODD = DEPTH // 2
CONV_K = 4
LRU_WIDTH = D_MODEL // 2
LRU_BLOCKS = 8
LRU_BLOCK = LRU_WIDTH // LRU_BLOCKS
LRU_C = 8.0
HGRN_WIDTH = D_MODEL // 2
HGRN_HEADS = 8
HGRN_DK = HGRN_WIDTH // HGRN_HEADS
HGRN_DV = HGRN_DK
HGRN_CHUNK = 64
AB_IN = 2 * LRU_WIDTH + 4 * HGRN_WIDTH
AB_SPLITS = [LRU_WIDTH, 2 * LRU_WIDTH, 2 * LRU_WIDTH + HGRN_WIDTH,
             2 * LRU_WIDTH + 2 * HGRN_WIDTH, 2 * LRU_WIDTH + 3 * HGRN_WIDTH]
AB_OUT = LRU_WIDTH + HGRN_WIDTH
SSD_INNER = 2 * D_MODEL
SSD_HEADDIM = 64
SSD_HEADS = SSD_INNER // SSD_HEADDIM
SSD_GROUPS = 8
SSD_HPG = SSD_HEADS // SSD_GROUPS
SSD_STATE = 128
SSD_CHUNK = 128
SSD_CONV_DIM = SSD_INNER + 2 * SSD_GROUPS * SSD_STATE
SSD_IN = SSD_INNER + SSD_CONV_DIM + SSD_HEADS
XA_HEADS = 4
XA_HEADDIM = D_MODEL // XA_HEADS
FFN_HIDDEN = ((8 * D_MODEL // 3 + 255) // 256) * 256

kernel_name = 'hybrid_rglru_hgrn2_ssd_trunk'


def rmsnorm(x, g, eps=1e-6):
    xf = x.astype(jnp.float32)
    y = xf * lax.rsqrt(jnp.mean(xf * xf, axis=-1, keepdims=True) + eps)
    return (y * g.astype(jnp.float32)).astype(x.dtype)


def causal_dwconv(u, w, b):
    k_width = w.shape[0]
    s = u.shape[1]
    up = jnp.pad(u, ((0, 0), (k_width - 1, 0), (0, 0)))
    out = b
    for k in range(k_width):
        out = out + up[:, k:k + s] * w[k]
    return out


def linear_recurrence(a, b):
    def combine(l, r):
        return (l[0] * r[0], r[0] * l[1] + r[1])
    _, h = lax.associative_scan(combine, (a, b), axis=1)
    return h


def hgrn2_chunked(q, k, log_f, v):
    bsz, s, h, dk = q.shape
    dv = v.shape[-1]
    n_chunks = s // HGRN_CHUNK

    def to_chunks(t):
        return t.reshape(bsz, n_chunks, HGRN_CHUNK, h, t.shape[-1]).transpose(1, 0, 3, 2, 4)

    causal = jnp.tril(jnp.ones((HGRN_CHUNK, HGRN_CHUNK), dtype=bool))[:, :, None]

    def step(state, inp):
        qc, kc, gc, vc = inp
        cum = jnp.cumsum(gc, axis=2)
        diff = cum[:, :, :, None, :] - cum[:, :, None, :, :]
        decay = jnp.exp(jnp.where(causal, diff, -jnp.inf))
        scores = jnp.einsum('bhtk,bhsk,bhtsk->bhts', qc, kc, decay)
        o = jnp.einsum('bhts,bhsv->bhtv', scores, vc)
        o = o + jnp.einsum('bhtk,bhkv->bhtv', qc * jnp.exp(cum), state)
        last = cum[:, :, -1:, :]
        state = (jnp.exp(last[:, :, 0, :, None]) * state
                 + jnp.einsum('bhsk,bhsv->bhkv', kc * jnp.exp(last - cum), vc))
        return state, o

    state0 = jnp.zeros((bsz, h, dk, dv), jnp.float32)
    xs = (to_chunks(q), to_chunks(k), to_chunks(log_f), to_chunks(v))
    _, o = lax.scan(step, state0, xs)
    return o.transpose(1, 0, 3, 2, 4).reshape(bsz, s, h, dv)


def ssd_chunked(xh, dt, a_neg, bm, cm):
    bsz, s, g, r, p = xh.shape
    n = bm.shape[-1]
    n_chunks = s // SSD_CHUNK

    def to_chunks(t):
        return jnp.moveaxis(t.reshape(bsz, n_chunks, SSD_CHUNK, *t.shape[2:]), 1, 0)

    causal = jnp.tril(jnp.ones((SSD_CHUNK, SSD_CHUNK), dtype=bool))

    def step(state, inp):
        xc, dtc, bc, cc = inp
        cum = jnp.cumsum(dtc * a_neg, axis=1)
        cum_h = jnp.moveaxis(cum, 1, -1)
        seg = cum_h[..., :, None] - cum_h[..., None, :]
        decay = jnp.exp(jnp.where(causal, seg, -jnp.inf))
        cb = jnp.einsum('btgn,bsgn->bgts', cc, bc)
        xdt = xc * dtc[..., None]
        y = jnp.einsum('bgrts,bsgrp->btgrp', cb[:, :, None] * decay, xdt)
        y = y + jnp.einsum('btgn,bgrpn->btgrp', cc, state) * jnp.exp(cum)[..., None]
        last = cum[:, -1]
        w_s = jnp.exp(last[:, None] - cum)[..., None]
        state = (jnp.exp(last)[..., None, None] * state
                 + jnp.einsum('bsgn,bsgrp->bgrpn', bc, xdt * w_s))
        return state, y

    state0 = jnp.zeros((bsz, g, r, p, n), jnp.float32)
    xs = (to_chunks(xh), to_chunks(dt), to_chunks(bm), to_chunks(cm))
    _, y = lax.scan(step, state0, xs)
    return jnp.moveaxis(y, 0, 1).reshape(bsz, s, g, r, p)


def rglru_hgrn2_mixer(h, w_in, w_out, conv_w, conv_b, w_r, b_r, w_i, b_i, lam,
                      lower_bound, head_norm):
    bsz, s, _ = h.shape
    f32 = jnp.float32
    proj = h @ w_in
    xa, ga, q, f, iv, gb = jnp.split(proj, AB_SPLITS, axis=-1)
    xa = causal_dwconv(xa, conv_w, conv_b)
    xb = xa.reshape(bsz, s, LRU_BLOCKS, LRU_BLOCK)
    r_gate = jax.nn.sigmoid(jnp.einsum('bshi,hij->bshj', xb, w_r).reshape(bsz, s, LRU_WIDTH) + b_r).astype(f32)
    i_gate = jax.nn.sigmoid(jnp.einsum('bshi,hij->bshj', xb, w_i).reshape(bsz, s, LRU_WIDTH) + b_i).astype(f32)
    log_a = -LRU_C * r_gate * jax.nn.softplus(-lam.astype(f32))
    a = jnp.exp(log_a)
    mult = jnp.sqrt(-jnp.expm1(2.0 * log_a))
    h_lru = linear_recurrence(a, mult * i_gate * xa.astype(f32))
    y_a = jax.nn.gelu(ga) * h_lru.astype(h.dtype)
    qh = jax.nn.silu(q).astype(f32).reshape(bsz, s, HGRN_HEADS, HGRN_DK)
    fg = lower_bound + (1.0 - lower_bound) * jax.nn.sigmoid(f.astype(f32))
    kh = (1.0 - fg).reshape(bsz, s, HGRN_HEADS, HGRN_DK)
    log_fg = jnp.log(fg).reshape(bsz, s, HGRN_HEADS, HGRN_DK)
    vh = iv.astype(f32).reshape(bsz, s, HGRN_HEADS, HGRN_DV)
    o = hgrn2_chunked(qh, kh, log_fg, vh)
    o = rmsnorm(o, head_norm.reshape(HGRN_HEADS, HGRN_DV)).reshape(bsz, s, HGRN_WIDTH)
    y_b = o.astype(h.dtype) * jax.nn.silu(gb)
    return jnp.concatenate([y_a, y_b], axis=-1) @ w_out


def ssd_mixer(h, w_in, w_out, conv_w, conv_b, dt_bias, a_log, d_skip, norm_w):
    bsz, s, _ = h.shape
    f32 = jnp.float32
    proj = h @ w_in
    z, xbc, dt_raw = jnp.split(proj, [SSD_INNER, SSD_INNER + SSD_CONV_DIM], axis=-1)
    xbc = jax.nn.silu(causal_dwconv(xbc, conv_w, conv_b))
    xs, bm, cm = jnp.split(xbc, [SSD_INNER, SSD_INNER + SSD_GROUPS * SSD_STATE], axis=-1)
    dt = jax.nn.softplus(dt_raw.astype(f32) + dt_bias.astype(f32))
    a_neg = -jnp.exp(a_log.astype(f32))
    xh = xs.astype(f32).reshape(bsz, s, SSD_GROUPS, SSD_HPG, SSD_HEADDIM)
    y = ssd_chunked(xh,
                    dt.reshape(bsz, s, SSD_GROUPS, SSD_HPG),
                    a_neg.reshape(SSD_GROUPS, SSD_HPG),
                    bm.astype(f32).reshape(bsz, s, SSD_GROUPS, SSD_STATE),
                    cm.astype(f32).reshape(bsz, s, SSD_GROUPS, SSD_STATE))
    y = y + d_skip.astype(f32).reshape(SSD_GROUPS, SSD_HPG)[:, :, None] * xh
    y = y.reshape(bsz, s, SSD_INNER).astype(h.dtype) * jax.nn.silu(z)
    y = rmsnorm(y.reshape(bsz, s, SSD_GROUPS, SSD_INNER // SSD_GROUPS),
                norm_w.reshape(SSD_GROUPS, SSD_INNER // SSD_GROUPS)).reshape(bsz, s, SSD_INNER)
    return y @ w_out


def memory_cross_attention(h, mem_n, w_q, w_kv, w_o):
    bsz, s, _ = h.shape
    q = (h @ w_q).reshape(bsz, s, XA_HEADS, XA_HEADDIM)
    k, v = jnp.split(mem_n @ w_kv, 2, axis=-1)
    k = k.reshape(bsz, -1, XA_HEADS, XA_HEADDIM)
    v = v.reshape(bsz, -1, XA_HEADS, XA_HEADDIM)
    scores = jnp.einsum('bshd,bmhd->bhsm', q, k).astype(jnp.float32) * (XA_HEADDIM ** -0.5)
    p = jax.nn.softmax(scores, axis=-1).astype(v.dtype)
    o = jnp.einsum('bhsm,bmhd->bshd', p, v).reshape(bsz, s, D_MODEL)
    return o @ w_o


def swiglu(h, w_gate, w_up, w_down):
    return (jax.nn.silu(h @ w_gate) * (h @ w_up)) @ w_down


def setup_inputs(seed: int = 0) -> dict:
    key = jax.random.key(seed)
    k = jax.random.split(key, 32)
    f32 = jnp.float32

    def nrm(i, shape, scale):
        return jax.random.normal(k[i], shape, f32) * scale

    def gain(i, shape):
        return 1.0 + nrm(i, shape, 0.02)

    lam_u = jax.random.uniform(k[16], (N_EVEN, LRU_WIDTH), f32, 0.9, 0.999)
    lam_s = lam_u ** (1.0 / LRU_C)
    lam = jnp.log(lam_s) - jnp.log1p(-lam_s)
    dt0 = jnp.exp(jax.random.uniform(k[22], (N_ODD, SSD_HEADS), f32, math.log(1e-3), math.log(1e-1)))
    dt_bias = dt0 + jnp.log(-jnp.expm1(-dt0))
    a_log = jnp.log(jax.random.uniform(k[23], (N_ODD, SSD_HEADS), f32, 1.0, 16.0))
    return {
        'x': nrm(0, (BATCH, SEQ, D_MODEL), 1.0),
        'mem': nrm(1, (BATCH, MEM_LEN, D_MODEL), 1.0),
        'norm_mix': gain(2, (DEPTH, D_MODEL)),
        'norm_xattn': gain(3, (DEPTH, D_MODEL)),
        'norm_ffn': gain(4, (DEPTH, D_MODEL)),
        'norm_mem': gain(5, (D_MODEL,)),
        'norm_final': gain(6, (D_MODEL,)),
        'ab_w_in': nrm(7, (N_EVEN, D_MODEL, AB_IN), D_MODEL ** -0.5),
        'ab_w_out': nrm(8, (N_EVEN, AB_OUT, D_MODEL), AB_OUT ** -0.5),
        'lru_conv_w': nrm(9, (N_EVEN, CONV_K, LRU_WIDTH), CONV_K ** -0.5),
        'lru_conv_b': nrm(10, (N_EVEN, LRU_WIDTH), 0.01),
        'lru_w_r': nrm(11, (N_EVEN, LRU_BLOCKS, LRU_BLOCK, LRU_BLOCK), LRU_BLOCK ** -0.5),
        'lru_b_r': nrm(12, (N_EVEN, LRU_WIDTH), 0.01),
        'lru_w_i': nrm(13, (N_EVEN, LRU_BLOCKS, LRU_BLOCK, LRU_BLOCK), LRU_BLOCK ** -0.5),
        'lru_b_i': nrm(14, (N_EVEN, LRU_WIDTH), 0.01),
        'lru_lambda': lam,
        'hgrn_lower_bounds': nrm(15, (N_EVEN, HGRN_WIDTH), 0.1),
        'hgrn_norm': gain(17, (N_EVEN, HGRN_WIDTH)),
        'ssd_w_in': nrm(18, (N_ODD, D_MODEL, SSD_IN), D_MODEL ** -0.5),
        'ssd_w_out': nrm(19, (N_ODD, SSD_INNER, D_MODEL), SSD_INNER ** -0.5),
        'ssd_conv_w': nrm(20, (N_ODD, CONV_K, SSD_CONV_DIM), CONV_K ** -0.5),
        'ssd_conv_b': nrm(21, (N_ODD, SSD_CONV_DIM), 0.01),
        'ssd_dt_bias': dt_bias,
        'ssd_a_log': a_log,
        'ssd_d': gain(24, (N_ODD, SSD_HEADS)),
        'ssd_norm': gain(25, (N_ODD, SSD_INNER)),
        'xa_w_q': nrm(26, (DEPTH, D_MODEL, D_MODEL), D_MODEL ** -0.5),
        'xa_w_kv': nrm(27, (DEPTH, D_MODEL, 2 * D_MODEL), D_MODEL ** -0.5),
        'xa_w_o': nrm(28, (DEPTH, D_MODEL, D_MODEL), D_MODEL ** -0.5),
        'ffn_w_gate': nrm(29, (DEPTH, D_MODEL, FFN_HIDDEN), D_MODEL ** -0.5),
        'ffn_w_up': nrm(30, (DEPTH, D_MODEL, FFN_HIDDEN), D_MODEL ** -0.5),
        'ffn_w_down': nrm(31, (DEPTH, FFN_HIDDEN, D_MODEL), FFN_HIDDEN ** -0.5),
    }


def reference(x, mem, norm_mix, norm_xattn, norm_ffn, norm_mem, norm_final,
              ab_w_in, ab_w_out, lru_conv_w, lru_conv_b, lru_w_r, lru_b_r, lru_w_i, lru_b_i,
              lru_lambda, hgrn_lower_bounds, hgrn_norm,
              ssd_w_in, ssd_w_out, ssd_conv_w, ssd_conv_b, ssd_dt_bias, ssd_a_log, ssd_d, ssd_norm,
              xa_w_q, xa_w_kv, xa_w_o, ffn_w_gate, ffn_w_up, ffn_w_down):
    sm = jax.nn.softmax(hgrn_lower_bounds.astype(jnp.float32), axis=0)
    lower_bounds = jnp.cumsum(sm, axis=0) - sm[0]
    mem_n = rmsnorm(mem, norm_mem)
    for layer in range(DEPTH):
        h = rmsnorm(x, norm_mix[layer])
        if layer % 2 == 0:
            e = layer // 2
            y = rglru_hgrn2_mixer(h, ab_w_in[e], ab_w_out[e], lru_conv_w[e], lru_conv_b[e],
                                  lru_w_r[e], lru_b_r[e], lru_w_i[e], lru_b_i[e], lru_lambda[e],
                                  lower_bounds[e], hgrn_norm[e])
        else:
            o = layer // 2
            y = ssd_mixer(h, ssd_w_in[o], ssd_w_out[o], ssd_conv_w[o], ssd_conv_b[o],
                          ssd_dt_bias[o], ssd_a_log[o], ssd_d[o], ssd_norm[o])
        x = x + y
        x = x + memory_cross_attention(rmsnorm(x, norm_xattn[layer]), mem_n,
                                       xa_w_q[layer], xa_w_kv[layer], xa_w_o[layer])
        x = x + swiglu(rmsnorm(x, norm_ffn[layer]), ffn_w_gate[layer], ffn_w_up[layer], ffn_w_down[layer])
    return rmsnorm(x, norm_final)
```

```python
import functools
import math

import jax
import jax.numpy as jnp
from jax import lax
from jax.experimental import pallas as pl
from jax.experimental.pallas import tpu as pltpu

F32 = jnp.float32
BF16 = jnp.bfloat16
EPS = 1e-6
NEG_INF = float("-inf")

VMEM_LIMIT_BYTES = 56 * 1024 * 1024
SUBLANES = 8
LANES = 128

CONV_K = 4
LRU_BLOCK = 128
LRU_C = 8.0
HGRN_DK = 128
HGRN_CHUNK = 64
HGRN_SUB = SUBLANES
SSD_HEADDIM = 64
SSD_HPG = 8
SSD_STATE = 128
SSD_CHUNK = 128
XA_HEADS = 4
MEM_LEN = 256


def _cparams(*sem):
    return pltpu.CompilerParams(dimension_semantics=sem, vmem_limit_bytes=VMEM_LIMIT_BYTES)


def _rms_scale(x):
    return lax.rsqrt(jnp.mean(x * x, axis=-1, keepdims=True) + EPS)


def _silu(x):
    return x * jax.nn.sigmoid(x)


def _gelu_tanh(x):
    c = math.sqrt(2.0 / math.pi)
    return x * (0.5 * (1.0 + jnp.tanh(c * (x + 0.044715 * (x * x * x)))))


def _softplus(x):
    return jnp.maximum(x, 0.0) + jnp.log1p(jnp.exp(-jnp.abs(x)))


def _prefix_rows(a, b):
    n = a.shape[0]
    row = lax.broadcasted_iota(jnp.int32, a.shape, 0)
    s = 1
    while s < n:
        keep = row >= s
        a_sh = pltpu.roll(a, s, 0)
        b_sh = pltpu.roll(b, s, 0)
        b = jnp.where(keep, a * b_sh + b, b)
        a = jnp.where(keep, a * a_sh, a)
        s *= 2
    return a, b


def _cumsum(x, axis):
    n = x.shape[axis]
    idx = lax.broadcasted_iota(jnp.int32, x.shape, axis)
    s = 1
    while s < n:
        x = jnp.where(idx >= s, x + pltpu.roll(x, s, axis), x)
        s *= 2
    return x


def _causal_conv(u, buf_ref, w_ref, b_ref):
    t = u.shape[0]
    buf_ref[SUBLANES:SUBLANES + t, :] = u
    out = b_ref[...] + w_ref[3:4, :] * u
    for j in range(1, CONV_K):
        out = out + w_ref[CONV_K - 1 - j:CONV_K - j, :] * buf_ref[SUBLANES - j:SUBLANES - j + t, :]
    buf_ref[0:SUBLANES, :] = u[t - SUBLANES:t, :]
    return out


def _norm_matmul_kernel(x_ref, g_ref, w_ref, o_ref, xn_ref):
    @pl.when(pl.program_id(1) == 0)
    def _():
        x = x_ref[...]
        xn_ref[...] = (x * _rms_scale(x) * g_ref[...]).astype(BF16)

    o_ref[...] = jnp.dot(xn_ref[...], w_ref[...], preferred_element_type=F32).astype(o_ref.dtype)


def _norm_matmul(x, g, w, out_dtype, tm, tn):
    m, k = x.shape
    n = w.shape[1]
    tm = min(tm, m)
    tn = min(tn, n)
    return pl.pallas_call(
        _norm_matmul_kernel,
        grid=(m // tm, n // tn),
        in_specs=[
            pl.BlockSpec((tm, k), lambda i, j: (i, 0)),
            pl.BlockSpec((1, k), lambda i, j: (0, 0)),
            pl.BlockSpec((k, tn), lambda i, j: (0, j)),
        ],
        out_specs=pl.BlockSpec((tm, tn), lambda i, j: (i, j)),
        out_shape=jax.ShapeDtypeStruct((m, n), out_dtype),
        scratch_shapes=[pltpu.VMEM((tm, k), BF16)],
        compiler_params=_cparams("parallel", "arbitrary"),
    )(x, g.reshape(1, k), w)


def _matmul_res_kernel(a_ref, w_ref, r_ref, o_ref):
    o_ref[...] = r_ref[...] + jnp.dot(a_ref[...], w_ref[...], preferred_element_type=F32)


def _matmul_res(a, w, res, tm, tn):
    m, k = a.shape
    n = w.shape[1]
    tm = min(tm, m)
    tn = min(tn, n)
    return pl.pallas_call(
        _matmul_res_kernel,
        grid=(m // tm, n // tn),
        in_specs=[
            pl.BlockSpec((tm, k), lambda i, j: (i, 0)),
            pl.BlockSpec((k, tn), lambda i, j: (0, j)),
            pl.BlockSpec((tm, tn), lambda i, j: (i, j)),
        ],
        out_specs=pl.BlockSpec((tm, tn), lambda i, j: (i, j)),
        out_shape=jax.ShapeDtypeStruct((m, n), F32),
        compiler_params=_cparams("parallel", "arbitrary"),
    )(a, w, res)


def _matmul2_res_kernel(a1_ref, a2_ref, w1_ref, w2_ref, r_ref, o_ref):
    acc = jnp.dot(a1_ref[...], w1_ref[...], preferred_element_type=F32)
    acc = acc + jnp.dot(a2_ref[...], w2_ref[...], preferred_element_type=F32)
    o_ref[...] = r_ref[...] + acc


def _matmul2_res(a1, a2, w, res, tm, tn):
    m, k1 = a1.shape
    k2 = a2.shape[1]
    assert k1 == k2
    n = w.shape[1]
    tm = min(tm, m)
    tn = min(tn, n)
    return pl.pallas_call(
        _matmul2_res_kernel,
        grid=(m // tm, n // tn),
        in_specs=[
            pl.BlockSpec((tm, k1), lambda i, j: (i, 0)),
            pl.BlockSpec((tm, k2), lambda i, j: (i, 0)),
            pl.BlockSpec((k1, tn), lambda i, j: (0, j)),
            pl.BlockSpec((k2, tn), lambda i, j: (1, j)),
            pl.BlockSpec((tm, tn), lambda i, j: (i, j)),
        ],
        out_specs=pl.BlockSpec((tm, tn), lambda i, j: (i, j)),
        out_shape=jax.ShapeDtypeStruct((m, n), F32),
        compiler_params=_cparams("parallel", "arbitrary"),
    )(a1, a2, w, w, res)


def _ffn_kernel(x_ref, g_ref, wg_ref, wu_ref, wd_ref, o_ref, xn_ref):
    @pl.when(pl.program_id(1) == 0)
    def _():
        x = x_ref[...]
        xn_ref[...] = (x * _rms_scale(x) * g_ref[...]).astype(BF16)
        o_ref[...] = x

    xn = xn_ref[...]
    gate = jnp.dot(xn, wg_ref[...], preferred_element_type=F32)
    up = jnp.dot(xn, wu_ref[...], preferred_element_type=F32)
    hid = (_silu(gate) * up).astype(BF16)
    o_ref[...] += jnp.dot(hid, wd_ref[...], preferred_element_type=F32)


def _ffn(x, g, wg, wu, wd, tm, th):
    m, d = x.shape
    hdim = wg.shape[1]
    tm = min(tm, m)
    return pl.pallas_call(
        _ffn_kernel,
        grid=(m // tm, hdim // th),
        in_specs=[
            pl.BlockSpec((tm, d), lambda i, j: (i, 0)),
            pl.BlockSpec((1, d), lambda i, j: (0, 0)),
            pl.BlockSpec((d, th), lambda i, j: (0, j)),
            pl.BlockSpec((d, th), lambda i, j: (0, j)),
            pl.BlockSpec((th, d), lambda i, j: (j, 0)),
        ],
        out_specs=pl.BlockSpec((tm, d), lambda i, j: (i, 0)),
        out_shape=jax.ShapeDtypeStruct((m, d), F32),
        scratch_shapes=[pltpu.VMEM((tm, d), BF16)],
        compiler_params=_cparams("parallel", "arbitrary"),
    )(x, g.reshape(1, d), wg, wu, wd)


def _xattn_kernel(q_ref, k_ref, v_ref, wo_ref, x_ref, o_ref, ob_ref, *, heads, scale):
    hd = q_ref.shape[2] // heads
    for h in range(heads):
        sl = slice(h * hd, (h + 1) * hd)
        s = lax.dot_general(q_ref[0, :, sl], k_ref[0, :, sl], (((1,), (1,)), ((), ())),
                            preferred_element_type=F32) * scale
        p = jnp.exp(s - jnp.max(s, axis=-1, keepdims=True))
        p = p / jnp.sum(p, axis=-1, keepdims=True)
        ob_ref[:, sl] = jnp.dot(p.astype(BF16), v_ref[0, :, sl], preferred_element_type=F32).astype(BF16)
    o_ref[0] = x_ref[0] + jnp.dot(ob_ref[...], wo_ref[...], preferred_element_type=F32)


def _xattn(q, kv, layer, wo, x, tm):
    b, s, d = x.shape
    mem = kv.shape[1]
    tm = min(tm, s)
    kern = functools.partial(_xattn_kernel, heads=XA_HEADS, scale=(d // XA_HEADS) ** -0.5)
    return pl.pallas_call(
        kern,
        grid=(b, s // tm),
        in_specs=[
            pl.BlockSpec((1, tm, d), lambda i, j: (i, j, 0)),
            pl.BlockSpec((1, mem, d), lambda i, j: (i, 0, 2 * layer)),
            pl.BlockSpec((1, mem, d), lambda i, j: (i, 0, 2 * layer + 1)),
            pl.BlockSpec((d, d), lambda i, j: (0, 0)),
            pl.BlockSpec((1, tm, d), lambda i, j: (i, j, 0)),
        ],
        out_specs=pl.BlockSpec((1, tm, d), lambda i, j: (i, j, 0)),
        out_shape=jax.ShapeDtypeStruct((b, s, d), F32),
        scratch_shapes=[pltpu.VMEM((tm, d), BF16)],
        compiler_params=_cparams("parallel", "arbitrary"),
    )(q, kv, kv, wo, x)


def _lru_kernel(xa_ref, ga_ref, cw_ref, cb_ref, wri_ref, br_ref, bi_ref, sp_ref, o_ref, buf_ref, h_ref):
    @pl.when(pl.program_id(1) == 0)
    def _():
        buf_ref[0:SUBLANES, :] = jnp.zeros((SUBLANES, buf_ref.shape[1]), F32)
        h_ref[...] = jnp.zeros(h_ref.shape, F32)

    xc = _causal_conv(xa_ref[0], buf_ref, cw_ref, cb_ref)
    t = xc.shape[0]
    for blk in range(xc.shape[1] // LRU_BLOCK):
        sl = slice(blk * LRU_BLOCK, (blk + 1) * LRU_BLOCK)
        xb = xc[:, sl]
        pre = jnp.dot(xb.astype(BF16), wri_ref[blk], preferred_element_type=F32)
        r_gate = jax.nn.sigmoid(pre[:, :LRU_BLOCK] + br_ref[:, sl])
        i_gate = jax.nn.sigmoid(pre[:, LRU_BLOCK:] + bi_ref[:, sl])
        log_a = (-LRU_C) * r_gate * sp_ref[:, sl]
        a = jnp.exp(log_a)
        mult = jnp.sqrt(-jnp.tanh(log_a) * (a * a + 1.0))
        a_cum, h_loc = _prefix_rows(a, mult * i_gate * xb)
        h = a_cum * h_ref[0:1, sl] + h_loc
        h_ref[0:1, sl] = h[t - 1:t, :]
        o_ref[0, :, sl] = (_gelu_tanh(ga_ref[0, :, sl]) * h).astype(BF16)


def _lru(proj, conv_w, conv_b, w_ri, b_r, b_i, sp, tb):
    b, s, _ = proj.shape
    w = conv_w.shape[1]
    tb = min(tb, s)
    vec = lambda: pl.BlockSpec((1, w), lambda i, j: (0, 0))
    return pl.pallas_call(
        _lru_kernel,
        grid=(b, s // tb),
        in_specs=[
            pl.BlockSpec((1, tb, w), lambda i, j: (i, j, 0)),
            pl.BlockSpec((1, tb, w), lambda i, j: (i, j, 1)),
            pl.BlockSpec((CONV_K, w), lambda i, j: (0, 0)),
            vec(),
            pl.BlockSpec(w_ri.shape, lambda i, j: (0, 0, 0)),
            vec(), vec(), vec(),
        ],
        out_specs=pl.BlockSpec((1, tb, w), lambda i, j: (i, j, 0)),
        out_shape=jax.ShapeDtypeStruct((b, s, w), BF16),
        scratch_shapes=[pltpu.VMEM((tb + SUBLANES, w), F32), pltpu.VMEM((SUBLANES, w), F32)],
        compiler_params=_cparams("parallel", "arbitrary"),
    )(proj, proj, conv_w, conv_b.reshape(1, w), w_ri, b_r.reshape(1, w), b_i.reshape(1, w), sp.reshape(1, w))


def _hgrn_chunk(q, f, v, lb, st):
    c, dk = q.shape
    nsub = c // HGRN_SUB
    qh = _silu(q)
    fg = lb + (1.0 - lb) * jax.nn.sigmoid(f)
    kh = 1.0 - fg
    g = jnp.log(fg)
    cum = _cumsum(g, 0)
    ex = cum - g
    cum3 = cum.reshape(nsub, HGRN_SUB, dk)
    ex3 = ex.reshape(nsub, HGRN_SUB, dk)
    base3 = jnp.broadcast_to(ex3[:, 0:1, :], cum3.shape)
    q3 = qh.reshape(nsub, HGRN_SUB, dk)
    k3 = kh.reshape(nsub, HGRN_SUB, dk)
    v3 = v.reshape(nsub, HGRN_SUB, dk)
    vb = v.astype(BF16)

    row3 = lax.broadcasted_iota(jnp.int32, cum3.shape, 1)
    terms = []
    for s in range(HGRN_SUB):
        diff = cum3 - cum3[:, s:s + 1, :]
        dec = jnp.exp(jnp.where(row3 >= s, diff, NEG_INF))
        terms.append((dec * q3 * k3[:, s:s + 1, :]).reshape(c, dk))
    stacked = jnp.concatenate(terms, axis=0).astype(BF16)
    ones = jnp.ones((dk, dk), BF16)
    summed = jnp.dot(stacked, ones, preferred_element_type=F32)
    o3 = jnp.zeros(cum3.shape, F32)
    for s in range(HGRN_SUB):
        o3 = o3 + summed[s * c:(s + 1) * c, :].reshape(nsub, HGRN_SUB, dk) * v3[:, s:s + 1, :]
    o = o3.reshape(c, dk)

    q_loc = (qh * jnp.exp(cum - base3.reshape(c, dk))).astype(BF16)
    pad = jnp.zeros((LANES, dk), F32)
    k_parts = []
    for i in range(1, nsub):
        n = i * HGRN_SUB
        base_i = ex[n:n + 1, :]
        k_parts.append(kh[0:n, :] * jnp.exp(base_i - cum[0:n, :]))
        k_parts.append(pad[0:LANES - n, :])
    k_hat = jnp.concatenate(k_parts, axis=0).astype(BF16)
    a_all = lax.dot_general(q_loc, k_hat, (((1,), (1,)), ((), ())), preferred_element_type=F32)
    a_rows = [jnp.zeros((HGRN_SUB, c), F32)]
    for i in range(1, nsub):
        a_rows.append(a_all[i * HGRN_SUB:(i + 1) * HGRN_SUB, (i - 1) * LANES:(i - 1) * LANES + c])
    a_off = jnp.concatenate(a_rows, axis=0).astype(BF16)
    o = o + jnp.dot(a_off, vb, preferred_element_type=F32)

    q_in = (qh * jnp.exp(cum)).astype(BF16)
    o = o + lax.dot_general(q_in, st.astype(BF16), (((1,), (1,)), ((), ())), preferred_element_type=F32)
    last = cum[c - 1:c, :]
    k_out = (kh * jnp.exp(last - cum)).astype(BF16)
    st_new = st * jnp.exp(last) + lax.dot_general(vb, k_out, (((0,), (0,)), ((), ())),
                                                  preferred_element_type=F32)
    return o, st_new


def _hgrn_kernel(q_ref, f_ref, v_ref, gb_ref, lb_ref, gn_ref, o_ref, st_ref):
    @pl.when(pl.program_id(2) == 0)
    def _():
        st_ref[...] = jnp.zeros(st_ref.shape, F32)

    lb = lb_ref[...]
    st = st_ref[...]
    for ci in range(q_ref.shape[1] // HGRN_CHUNK):
        rows = slice(ci * HGRN_CHUNK, (ci + 1) * HGRN_CHUNK)
        o, st = _hgrn_chunk(q_ref[0, rows, :], f_ref[0, rows, :], v_ref[0, rows, :], lb, st)
        o = o * _rms_scale(o) * gn_ref[...]
        o_ref[0, rows, :] = (o * _silu(gb_ref[0, rows, :])).astype(BF16)
    st_ref[...] = st


def _hgrn(proj, lower_bound, head_norm, col0, heads, tb):
    b, s, _ = proj.shape
    tb = min(tb, s)
    c0 = col0 // HGRN_DK
    part = lambda p: pl.BlockSpec((1, tb, HGRN_DK), lambda i, h, j: (i, j, c0 + p * heads + h))
    vec = lambda: pl.BlockSpec((1, HGRN_DK), lambda i, h, j: (0, h))
    w = heads * HGRN_DK
    return pl.pallas_call(
        _hgrn_kernel,
        grid=(b, heads, s // tb),
        in_specs=[part(0), part(1), part(2), part(3), vec(), vec()],
        out_specs=pl.BlockSpec((1, tb, HGRN_DK), lambda i, h, j: (i, j, h)),
        out_shape=jax.ShapeDtypeStruct((b, s, w), BF16),
        scratch_shapes=[pltpu.VMEM((HGRN_DK, HGRN_DK), F32)],
        compiler_params=_cparams("parallel", "parallel", "arbitrary"),
    )(proj, proj, proj, proj, lower_bound.reshape(1, w), head_norm.reshape(1, w))


def _ssd_kernel(z_ref, x_ref, b_ref, c_ref, dtc_ref, dtr_ref,
                cwx_ref, cbx_ref, cwb_ref, cbb_ref, cwc_ref, cbc_ref,
                dbc_ref, dbr_ref, anc_ref, anr_ref, dsk_ref, nw_ref,
                o_ref, st_ref, y_ref, bufx_ref, bufb_ref, bufc_ref):
    @pl.when(pl.program_id(2) == 0)
    def _():
        st_ref[...] = jnp.zeros(st_ref.shape, F32)
        bufx_ref[0:SUBLANES, :] = jnp.zeros((SUBLANES, bufx_ref.shape[1]), F32)
        bufb_ref[0:SUBLANES, :] = jnp.zeros((SUBLANES, bufb_ref.shape[1]), F32)
        bufc_ref[0:SUBLANES, :] = jnp.zeros((SUBLANES, bufc_ref.shape[1]), F32)

    t = x_ref.shape[1]
    p = SSD_HEADDIM
    xs = _silu(_causal_conv(x_ref[0], bufx_ref, cwx_ref, cbx_ref))
    bm = _silu(_causal_conv(b_ref[0], bufb_ref, cwb_ref, cbb_ref))
    cm = _silu(_causal_conv(c_ref[0], bufc_ref, cwc_ref, cbc_ref))
    bmb = bm.astype(BF16)
    cmb = cm.astype(BF16)

    dt_c = _softplus(dtc_ref[0, 0] + dbc_ref[0])
    cum_c = _cumsum(dt_c * anc_ref[0], 0)
    dt_r = _softplus(dtr_ref[0, 0] + dbr_ref[0])
    cum_r = _cumsum(dt_r * anr_ref[0], 1)
    last_c = cum_c[t - 1:t, :]

    cb = lax.dot_general(cmb, bmb, (((1,), (1,)), ((), ())), preferred_element_type=F32)
    y_in = jnp.dot(cmb, st_ref[...].astype(BF16), preferred_element_type=F32)
    causal = (lax.broadcasted_iota(jnp.int32, (t, t), 0) >= lax.broadcasted_iota(jnp.int32, (t, t), 1))

    xw_parts = []
    for h in range(SSD_HPG):
        sl = slice(h * p, (h + 1) * p)
        cum_h = cum_c[:, h:h + 1]
        xdt = xs[:, sl] * dt_c[:, h:h + 1]
        seg = cum_h - cum_r[h:h + 1, :]
        dec = jnp.exp(jnp.where(causal, seg, NEG_INF))
        y = jnp.dot((cb * dec).astype(BF16), xdt.astype(BF16), preferred_element_type=F32)
        y = y + y_in[:, sl] * jnp.exp(cum_h)
        y = y + dsk_ref[0][:, h:h + 1] * xs[:, sl]
        y_ref[:, sl] = y
        xw_parts.append((xdt * jnp.exp(last_c[:, h:h + 1] - cum_h)).astype(BF16))
        st_ref[:, sl] = st_ref[:, sl] * jnp.exp(last_c[:, h:h + 1])
    xw = jnp.concatenate(xw_parts, axis=1)
    st_ref[...] += lax.dot_general(bmb, xw, (((0,), (0,)), ((), ())), preferred_element_type=F32)

    y = y_ref[...] * _silu(z_ref[0])
    o_ref[0] = (y * _rms_scale(y) * nw_ref[...]).astype(BF16)


def _ssd(proj, dt_raw, conv_w, conv_b, dt_bias, a_neg, d_skip, norm_w, inner, groups):
    b, s, _ = proj.shape
    t = min(SSD_CHUNK, s)
    gw = inner // groups
    n = SSD_STATE
    hpg = SSD_HPG
    dt_g = dt_raw.reshape(b, s, groups, hpg)
    dt_col = dt_g.transpose(0, 2, 1, 3)
    dt_row = dt_g.transpose(0, 2, 3, 1)
    per_head = lambda a: (a.reshape(groups, 1, hpg), a.reshape(groups, hpg, 1))
    db_c, db_r = per_head(dt_bias)
    an_c, an_r = per_head(a_neg)
    ds_c, _ = per_head(d_skip)
    cw = conv_w
    cbias = conv_b.reshape(1, -1)
    xoff, boff, coff = inner // gw, (2 * inner) // n, (2 * inner + groups * n) // n
    cboff, ccoff = inner // n, (inner + groups * n) // n
    col = lambda shape: pl.BlockSpec((1,) + shape, lambda i, g, j: (g, 0, 0))
    return pl.pallas_call(
        _ssd_kernel,
        grid=(b, groups, s // t),
        in_specs=[
            pl.BlockSpec((1, t, gw), lambda i, g, j: (i, j, g)),
            pl.BlockSpec((1, t, gw), lambda i, g, j: (i, j, xoff + g)),
            pl.BlockSpec((1, t, n), lambda i, g, j: (i, j, boff + g)),
            pl.BlockSpec((1, t, n), lambda i, g, j: (i, j, coff + g)),
            pl.BlockSpec((1, 1, t, hpg), lambda i, g, j: (i, g, j, 0)),
            pl.BlockSpec((1, 1, hpg, t), lambda i, g, j: (i, g, 0, j)),
            pl.BlockSpec((CONV_K, gw), lambda i, g, j: (0, g)),
            pl.BlockSpec((1, gw), lambda i, g, j: (0, g)),
            pl.BlockSpec((CONV_K, n), lambda i, g, j: (0, cboff + g)),
            pl.BlockSpec((1, n), lambda i, g, j: (0, cboff + g)),
            pl.BlockSpec((CONV_K, n), lambda i, g, j: (0, ccoff + g)),
            pl.BlockSpec((1, n), lambda i, g, j: (0, ccoff + g)),
            col((1, hpg)), col((hpg, 1)), col((1, hpg)), col((hpg, 1)), col((1, hpg)),
            pl.BlockSpec((1, gw), lambda i, g, j: (0, g)),
        ],
        out_specs=pl.BlockSpec((1, t, gw), lambda i, g, j: (i, j, g)),
        out_shape=jax.ShapeDtypeStruct((b, s, inner), BF16),
        scratch_shapes=[
            pltpu.VMEM((n, gw), F32),
            pltpu.VMEM((t, gw), F32),
            pltpu.VMEM((t + SUBLANES, gw), F32),
            pltpu.VMEM((t + SUBLANES, n), F32),
            pltpu.VMEM((t + SUBLANES, n), F32),
        ],
        compiler_params=_cparams("parallel", "parallel", "arbitrary"),
    )(proj, proj, proj, proj, dt_col, dt_row, cw, cbias, cw, cbias, cw, cbias,
      db_c, db_r, an_c, an_r, ds_c, norm_w.reshape(1, inner))


def _rmsnorm_kernel(x_ref, g_ref, o_ref):
    x = x_ref[...]
    o_ref[...] = x * _rms_scale(x) * g_ref[...]


def _rmsnorm(x, g, tm):
    m, d = x.shape
    tm = min(tm, m)
    return pl.pallas_call(
        _rmsnorm_kernel,
        grid=(m // tm,),
        in_specs=[pl.BlockSpec((tm, d), lambda i: (i, 0)), pl.BlockSpec((1, d), lambda i: (0, 0))],
        out_specs=pl.BlockSpec((tm, d), lambda i: (i, 0)),
        out_shape=jax.ShapeDtypeStruct((m, d), F32),
        compiler_params=_cparams("parallel"),
    )(x, g.reshape(1, d))


TM = 1024
TN = 1024
FFN_TH = 512
XA_TM = 512
LRU_TB = 256
HGRN_TB = 256


def kernel(x, mem, norm_mix, norm_xattn, norm_ffn, norm_mem, norm_final, ab_w_in, ab_w_out, lru_conv_w, lru_conv_b, lru_w_r, lru_b_r, lru_w_i, lru_b_i, lru_lambda, hgrn_lower_bounds, hgrn_norm, ssd_w_in, ssd_w_out, ssd_conv_w, ssd_conv_b, ssd_dt_bias, ssd_a_log, ssd_d, ssd_norm, xa_w_q, xa_w_kv, xa_w_o, ffn_w_gate, ffn_w_up, ffn_w_down):
    bsz, seq, d = x.shape
    depth = norm_mix.shape[0]
    m = bsz * seq
    lru_w = lru_conv_w.shape[2]
    hgrn_w = hgrn_norm.shape[1]
    ssd_heads = ssd_a_log.shape[1]
    ssd_inner = ssd_norm.shape[1]
    ssd_groups = ssd_heads // SSD_HPG
    ssd_main = ssd_w_in.shape[2] - ssd_heads

    sm = jax.nn.softmax(hgrn_lower_bounds.astype(F32), axis=0)
    lower_bounds = jnp.cumsum(sm, axis=0) - sm[0]
    lru_sp = jax.nn.softplus(-lru_lambda.astype(F32))
    ssd_a_neg = -jnp.exp(ssd_a_log.astype(F32))
    lru_w_ri = jnp.concatenate([lru_w_r, lru_w_i], axis=-1).astype(BF16)
    w_kv_all = jnp.transpose(xa_w_kv, (1, 0, 2)).reshape(d, depth * 2 * d).astype(BF16)

    x2 = x.reshape(m, d)
    kv = _norm_matmul(mem.reshape(bsz * mem.shape[1], d), norm_mem, w_kv_all, BF16, TM, TN)
    kv = kv.reshape(bsz, mem.shape[1], depth * 2 * d)

    for layer in range(depth):
        if layer % 2 == 0:
            e = layer // 2
            proj = _norm_matmul(x2, norm_mix[layer], ab_w_in[e].astype(BF16), F32, TM, TN)
            proj = proj.reshape(bsz, seq, -1)
            ya = _lru(proj, lru_conv_w[e], lru_conv_b[e], lru_w_ri[e], lru_b_r[e], lru_b_i[e], lru_sp[e], LRU_TB)
            yb = _hgrn(proj, lower_bounds[e], hgrn_norm[e], 2 * lru_w, hgrn_w // HGRN_DK, HGRN_TB)
            x2 = _matmul2_res(ya.reshape(m, lru_w), yb.reshape(m, hgrn_w), ab_w_out[e].astype(BF16), x2, TM, TN)
        else:
            o = layer // 2
            w_in = ssd_w_in[o]
            proj = _norm_matmul(x2, norm_mix[layer], w_in[:, :ssd_main].astype(BF16), F32, TM, TN)
            dt_raw = _norm_matmul(x2, norm_mix[layer], w_in[:, ssd_main:].astype(BF16), F32, TM, TN)
            y = _ssd(proj.reshape(bsz, seq, ssd_main), dt_raw.reshape(bsz, seq, ssd_heads),
                     ssd_conv_w[o], ssd_conv_b[o], ssd_dt_bias[o], ssd_a_neg[o], ssd_d[o], ssd_norm[o],
                     ssd_inner, ssd_groups)
            x2 = _matmul_res(y.reshape(m, ssd_inner), ssd_w_out[o].astype(BF16), x2, TM, TN // 2)
        q = _norm_matmul(x2, norm_xattn[layer], xa_w_q[layer].astype(BF16), BF16, TM, TN)
        x2 = _xattn(q.reshape(bsz, seq, d), kv, layer, xa_w_o[layer].astype(BF16), x2.reshape(bsz, seq, d), XA_TM)
        x2 = x2.reshape(m, d)
        x2 = _ffn(x2, norm_ffn[layer], ffn_w_gate[layer].astype(BF16), ffn_w_up[layer].astype(BF16),
                  ffn_w_down[layer].astype(BF16), TM, FFN_TH)
    return _rmsnorm(x2, norm_final, TM).reshape(bsz, seq, d)
```

```python
import functools
import math

import jax
import jax.numpy as jnp
from jax import lax
from jax.experimental import pallas as pl
from jax.experimental.pallas import tpu as pltpu

F32 = jnp.float32
BF16 = jnp.bfloat16
EPS = 1e-6
NEG_INF = float("-inf")

VMEM_LIMIT_BYTES = 56 * 1024 * 1024
CAST_BLOCK_BYTES = 4 * 1024 * 1024
SUBLANES = 8
LANES = 128

CONV_K = 4
LRU_BLOCK = 128
LRU_C = 8.0
HGRN_DK = 128
HGRN_CHUNK = 64
HGRN_SUB = SUBLANES
SSD_HEADDIM = 64
SSD_HPG = 8
SSD_STATE = 128
SSD_CHUNK = 128
XA_HEADS = 4
MEM_LEN = 256


def _cparams(*sem):
    return pltpu.CompilerParams(dimension_semantics=sem, vmem_limit_bytes=VMEM_LIMIT_BYTES)


def _rms_scale(x):
    return lax.rsqrt(jnp.mean(x * x, axis=-1, keepdims=True) + EPS)


def _silu(x):
    return x * jax.nn.sigmoid(x)


def _gelu_tanh(x):
    c = math.sqrt(2.0 / math.pi)
    return x * (0.5 * (1.0 + jnp.tanh(c * (x + 0.044715 * (x * x * x)))))


def _softplus(x):
    return jnp.maximum(x, 0.0) + jnp.log1p(jnp.exp(-jnp.abs(x)))


def _prefix_rows(a, b):
    n = a.shape[0]
    row = lax.broadcasted_iota(jnp.int32, a.shape, 0)
    s = 1
    while s < n:
        keep = row >= s
        a_sh = pltpu.roll(a, s, 0)
        b_sh = pltpu.roll(b, s, 0)
        b = jnp.where(keep, a * b_sh + b, b)
        a = jnp.where(keep, a * a_sh, a)
        s *= 2
    return a, b


def _cumsum(x, axis):
    n = x.shape[axis]
    idx = lax.broadcasted_iota(jnp.int32, x.shape, axis)
    s = 1
    while s < n:
        x = jnp.where(idx >= s, x + pltpu.roll(x, s, axis), x)
        s *= 2
    return x


def _causal_conv(u, buf_ref, w_ref, b_ref):
    t = u.shape[0]
    buf_ref[SUBLANES:SUBLANES + t, :] = u
    out = b_ref[...] + w_ref[3:4, :] * u
    for j in range(1, CONV_K):
        out = out + w_ref[CONV_K - 1 - j:CONV_K - j, :] * buf_ref[SUBLANES - j:SUBLANES - j + t, :]
    buf_ref[0:SUBLANES, :] = u[t - SUBLANES:t, :]
    return out


def _norm_matmul_kernel(x_ref, g_ref, w_ref, o_ref, xn_ref):
    @pl.when(pl.program_id(1) == 0)
    def _():
        x = x_ref[...]
        xn_ref[...] = (x * _rms_scale(x) * g_ref[...]).astype(BF16)

    o_ref[...] = jnp.dot(xn_ref[...], w_ref[...], preferred_element_type=F32).astype(o_ref.dtype)


def _norm_matmul(x, g, w, layer, out_dtype, tm, tn, name):
    m, k = x.shape
    n = w.shape[2]
    tm = min(tm, m)
    tn = min(tn, n)
    return pl.pallas_call(
        _norm_matmul_kernel,
        grid=(m // tm, n // tn),
        in_specs=[
            pl.BlockSpec((tm, k), lambda i, j: (i, 0)),
            pl.BlockSpec((1, k), lambda i, j: (0, 0)),
            pl.BlockSpec((None, k, tn), lambda i, j: (layer, 0, j)),
        ],
        out_specs=pl.BlockSpec((tm, tn), lambda i, j: (i, j)),
        out_shape=jax.ShapeDtypeStruct((m, n), out_dtype),
        scratch_shapes=[pltpu.VMEM((tm, k), BF16)],
        compiler_params=_cparams("parallel", "arbitrary"),
        name=name,
    )(x, g.reshape(1, k), w)


def _kv_proj_kernel(x_ref, g_ref, w_ref, o_ref, xn_ref):
    @pl.when((pl.program_id(0) == 0) & (pl.program_id(1) == 0))
    def _():
        x = x_ref[...]
        xn_ref[...] = (x * _rms_scale(x) * g_ref[...]).astype(BF16)

    o_ref[...] = jnp.dot(xn_ref[...], w_ref[...], preferred_element_type=F32).astype(o_ref.dtype)


def _kv_proj(x, g, w, tn):
    m, k = x.shape
    nl, _, n = w.shape
    return pl.pallas_call(
        _kv_proj_kernel,
        grid=(nl, n // tn),
        in_specs=[
            pl.BlockSpec((m, k), lambda l, j: (0, 0)),
            pl.BlockSpec((1, k), lambda l, j: (0, 0)),
            pl.BlockSpec((None, k, tn), lambda l, j: (l, 0, j)),
        ],
        out_specs=pl.BlockSpec((None, m, tn), lambda l, j: (l, 0, j)),
        out_shape=jax.ShapeDtypeStruct((nl, m, n), BF16),
        scratch_shapes=[pltpu.VMEM((m, k), BF16)],
        compiler_params=_cparams("arbitrary", "arbitrary"),
        name="kv_proj",
    )(x, g.reshape(1, k), w)


def _cast_kernel(x_ref, *o_refs):
    off = 0
    for o_ref in o_refs:
        width = o_ref.shape[-1]
        o_ref[...] = x_ref[:, off:off + width].astype(BF16)
        off += width


def _cast_bf16(w, splits=None):
    nl, k, n = w.shape
    splits = splits or (n,)
    assert sum(splits) == n
    tk = SUBLANES
    while tk * 2 <= k and k % (tk * 2) == 0 and tk * 2 * n * 4 <= CAST_BLOCK_BYTES:
        tk *= 2
    outs = pl.pallas_call(
        _cast_kernel,
        grid=(nl, k // tk),
        in_specs=[pl.BlockSpec((None, tk, n), lambda l, i: (l, i, 0))],
        out_specs=[pl.BlockSpec((None, tk, s), lambda l, i: (l, i, 0)) for s in splits],
        out_shape=[jax.ShapeDtypeStruct((nl, k, s), BF16) for s in splits],
        compiler_params=_cparams("parallel", "parallel"),
        name="cast_bf16",
    )(w)
    return outs if len(splits) > 1 else outs[0]


def _matmul_res_kernel(a_ref, w_ref, r_ref, o_ref):
    o_ref[...] = r_ref[...] + jnp.dot(a_ref[...], w_ref[...], preferred_element_type=F32)


def _matmul_res(a, w, layer, res, tm, tn, name):
    m, k = a.shape
    n = w.shape[2]
    tm = min(tm, m)
    tn = min(tn, n)
    return pl.pallas_call(
        _matmul_res_kernel,
        grid=(m // tm, n // tn),
        in_specs=[
            pl.BlockSpec((tm, k), lambda i, j: (i, 0)),
            pl.BlockSpec((None, k, tn), lambda i, j: (layer, 0, j)),
            pl.BlockSpec((tm, tn), lambda i, j: (i, j)),
        ],
        out_specs=pl.BlockSpec((tm, tn), lambda i, j: (i, j)),
        out_shape=jax.ShapeDtypeStruct((m, n), F32),
        compiler_params=_cparams("parallel", "arbitrary"),
        name=name,
    )(a, w, res)


def _matmul2_res_kernel(a1_ref, a2_ref, w1_ref, w2_ref, r_ref, o_ref):
    acc = jnp.dot(a1_ref[...], w1_ref[...], preferred_element_type=F32)
    acc = acc + jnp.dot(a2_ref[...], w2_ref[...], preferred_element_type=F32)
    o_ref[...] = r_ref[...] + acc


def _matmul2_res(a1, a2, w, layer, res, tm, tn, name):
    m, k1 = a1.shape
    k2 = a2.shape[1]
    assert k1 == k2
    n = w.shape[2]
    tm = min(tm, m)
    tn = min(tn, n)
    return pl.pallas_call(
        _matmul2_res_kernel,
        grid=(m // tm, n // tn),
        in_specs=[
            pl.BlockSpec((tm, k1), lambda i, j: (i, 0)),
            pl.BlockSpec((tm, k2), lambda i, j: (i, 0)),
            pl.BlockSpec((None, k1, tn), lambda i, j: (layer, 0, j)),
            pl.BlockSpec((None, k2, tn), lambda i, j: (layer, 1, j)),
            pl.BlockSpec((tm, tn), lambda i, j: (i, j)),
        ],
        out_specs=pl.BlockSpec((tm, tn), lambda i, j: (i, j)),
        out_shape=jax.ShapeDtypeStruct((m, n), F32),
        compiler_params=_cparams("parallel", "arbitrary"),
        name=name,
    )(a1, a2, w, w, res)


def _ffn_kernel(x_ref, g_ref, wg_ref, wu_ref, wd_ref, o_ref, xn_ref):
    @pl.when(pl.program_id(1) == 0)
    def _():
        x = x_ref[...]
        xn_ref[...] = (x * _rms_scale(x) * g_ref[...]).astype(BF16)
        o_ref[...] = x

    xn = xn_ref[...]
    gate = jnp.dot(xn, wg_ref[...], preferred_element_type=F32)
    up = jnp.dot(xn, wu_ref[...], preferred_element_type=F32)
    hid = (_silu(gate) * up).astype(BF16)
    o_ref[...] += jnp.dot(hid, wd_ref[...], preferred_element_type=F32)


def _ffn(x, g, wg, wu, wd, layer, tm, th):
    m, d = x.shape
    hdim = wg.shape[2]
    tm = min(tm, m)
    return pl.pallas_call(
        _ffn_kernel,
        grid=(m // tm, hdim // th),
        in_specs=[
            pl.BlockSpec((tm, d), lambda i, j: (i, 0)),
            pl.BlockSpec((1, d), lambda i, j: (0, 0)),
            pl.BlockSpec((None, d, th), lambda i, j: (layer, 0, j)),
            pl.BlockSpec((None, d, th), lambda i, j: (layer, 0, j)),
            pl.BlockSpec((None, th, d), lambda i, j: (layer, j, 0)),
        ],
        out_specs=pl.BlockSpec((tm, d), lambda i, j: (i, 0)),
        out_shape=jax.ShapeDtypeStruct((m, d), F32),
        scratch_shapes=[pltpu.VMEM((tm, d), BF16)],
        compiler_params=_cparams("parallel", "arbitrary"),
        name="ffn",
    )(x, g.reshape(1, d), wg, wu, wd)


def _xattn_kernel(q_ref, k_ref, v_ref, wo_ref, x_ref, o_ref, ob_ref, *, heads, scale):
    hd = q_ref.shape[2] // heads
    for h in range(heads):
        sl = slice(h * hd, (h + 1) * hd)
        s = lax.dot_general(q_ref[0, :, sl], k_ref[0, :, sl], (((1,), (1,)), ((), ())),
                            preferred_element_type=F32) * scale
        p = jnp.exp(s - jnp.max(s, axis=-1, keepdims=True))
        p = p / jnp.sum(p, axis=-1, keepdims=True)
        ob_ref[:, sl] = jnp.dot(p.astype(BF16), v_ref[0, :, sl], preferred_element_type=F32).astype(BF16)
    o_ref[0] = x_ref[0] + jnp.dot(ob_ref[...], wo_ref[...], preferred_element_type=F32)


def _xattn(q, kv, layer, wo, x, tm):
    b, s, d = x.shape
    mem = kv.shape[2]
    tm = min(tm, s)
    kern = functools.partial(_xattn_kernel, heads=XA_HEADS, scale=(d // XA_HEADS) ** -0.5)
    return pl.pallas_call(
        kern,
        grid=(b, s // tm),
        in_specs=[
            pl.BlockSpec((1, tm, d), lambda i, j: (i, j, 0)),
            pl.BlockSpec((None, 1, mem, d), lambda i, j: (layer, i, 0, 0)),
            pl.BlockSpec((None, 1, mem, d), lambda i, j: (layer, i, 0, 1)),
            pl.BlockSpec((None, d, d), lambda i, j: (layer, 0, 0)),
            pl.BlockSpec((1, tm, d), lambda i, j: (i, j, 0)),
        ],
        out_specs=pl.BlockSpec((1, tm, d), lambda i, j: (i, j, 0)),
        out_shape=jax.ShapeDtypeStruct((b, s, d), F32),
        scratch_shapes=[pltpu.VMEM((tm, d), BF16)],
        compiler_params=_cparams("parallel", "arbitrary"),
        name="xattn",
    )(q, kv, kv, wo, x)


def _lru_kernel(xa_ref, ga_ref, cw_ref, cb_ref, wri_ref, br_ref, bi_ref, sp_ref, o_ref, buf_ref, h_ref):
    @pl.when(pl.program_id(1) == 0)
    def _():
        buf_ref[0:SUBLANES, :] = jnp.zeros((SUBLANES, buf_ref.shape[1]), F32)
        h_ref[...] = jnp.zeros(h_ref.shape, F32)

    xc = _causal_conv(xa_ref[0], buf_ref, cw_ref, cb_ref)
    t = xc.shape[0]
    for blk in range(xc.shape[1] // LRU_BLOCK):
        sl = slice(blk * LRU_BLOCK, (blk + 1) * LRU_BLOCK)
        xb = xc[:, sl]
        pre = jnp.dot(xb.astype(BF16), wri_ref[blk], preferred_element_type=F32)
        r_gate = jax.nn.sigmoid(pre[:, :LRU_BLOCK] + br_ref[:, sl])
        i_gate = jax.nn.sigmoid(pre[:, LRU_BLOCK:] + bi_ref[:, sl])
        log_a = (-LRU_C) * r_gate * sp_ref[:, sl]
        a = jnp.exp(log_a)
        mult = jnp.sqrt(-jnp.tanh(log_a) * (a * a + 1.0))
        a_cum, h_loc = _prefix_rows(a, mult * i_gate * xb)
        h = a_cum * h_ref[0:1, sl] + h_loc
        h_ref[0:1, sl] = h[t - 1:t, :]
        o_ref[0, :, sl] = (_gelu_tanh(ga_ref[0, :, sl]) * h).astype(BF16)


def _lru(proj, conv_w, conv_b, w_ri, b_r, b_i, sp, tb):
    b, s, _ = proj.shape
    w = conv_w.shape[1]
    tb = min(tb, s)
    vec = lambda: pl.BlockSpec((1, w), lambda i, j: (0, 0))
    return pl.pallas_call(
        _lru_kernel,
        grid=(b, s // tb),
        in_specs=[
            pl.BlockSpec((1, tb, w), lambda i, j: (i, j, 0)),
            pl.BlockSpec((1, tb, w), lambda i, j: (i, j, 1)),
            pl.BlockSpec((CONV_K, w), lambda i, j: (0, 0)),
            vec(),
            pl.BlockSpec(w_ri.shape, lambda i, j: (0, 0, 0)),
            vec(), vec(), vec(),
        ],
        out_specs=pl.BlockSpec((1, tb, w), lambda i, j: (i, j, 0)),
        out_shape=jax.ShapeDtypeStruct((b, s, w), BF16),
        scratch_shapes=[pltpu.VMEM((tb + SUBLANES, w), F32), pltpu.VMEM((SUBLANES, w), F32)],
        compiler_params=_cparams("parallel", "arbitrary"),
        name="lru",
    )(proj, proj, conv_w, conv_b.reshape(1, w), w_ri, b_r.reshape(1, w), b_i.reshape(1, w), sp.reshape(1, w))


def _hgrn_chunk(q, f, v, lb, st):
    c, dk = q.shape
    nsub = c // HGRN_SUB
    qh = _silu(q)
    fg = lb + (1.0 - lb) * jax.nn.sigmoid(f)
    kh = 1.0 - fg
    g = jnp.log(fg)
    cum = _cumsum(g, 0)
    ex = cum - g
    cum3 = cum.reshape(nsub, HGRN_SUB, dk)
    ex3 = ex.reshape(nsub, HGRN_SUB, dk)
    base3 = jnp.broadcast_to(ex3[:, 0:1, :], cum3.shape)
    q3 = qh.reshape(nsub, HGRN_SUB, dk)
    k3 = kh.reshape(nsub, HGRN_SUB, dk)
    v3 = v.reshape(nsub, HGRN_SUB, dk)
    vb = v.astype(BF16)

    row3 = lax.broadcasted_iota(jnp.int32, cum3.shape, 1)
    terms = []
    for s in range(HGRN_SUB):
        diff = cum3 - cum3[:, s:s + 1, :]
        dec = jnp.exp(jnp.where(row3 >= s, diff, NEG_INF))
        terms.append((dec * q3 * k3[:, s:s + 1, :]).reshape(c, dk))
    stacked = jnp.concatenate(terms, axis=0).astype(BF16)
    ones = jnp.ones((dk, dk), BF16)
    summed = jnp.dot(stacked, ones, preferred_element_type=F32)
    o3 = jnp.zeros(cum3.shape, F32)
    for s in range(HGRN_SUB):
        o3 = o3 + summed[s * c:(s + 1) * c, :].reshape(nsub, HGRN_SUB, dk) * v3[:, s:s + 1, :]
    o = o3.reshape(c, dk)

    q_loc = (qh * jnp.exp(cum - base3.reshape(c, dk))).astype(BF16)
    pad = jnp.zeros((LANES, dk), F32)
    k_parts = []
    for i in range(1, nsub):
        n = i * HGRN_SUB
        base_i = ex[n:n + 1, :]
        k_parts.append(kh[0:n, :] * jnp.exp(base_i - cum[0:n, :]))
        k_parts.append(pad[0:LANES - n, :])
    k_hat = jnp.concatenate(k_parts, axis=0).astype(BF16)
    a_all = lax.dot_general(q_loc, k_hat, (((1,), (1,)), ((), ())), preferred_element_type=F32)
    a_rows = [jnp.zeros((HGRN_SUB, c), F32)]
    for i in range(1, nsub):
        a_rows.append(a_all[i * HGRN_SUB:(i + 1) * HGRN_SUB, (i - 1) * LANES:(i - 1) * LANES + c])
    a_off = jnp.concatenate(a_rows, axis=0).astype(BF16)
    o = o + jnp.dot(a_off, vb, preferred_element_type=F32)

    q_in = (qh * jnp.exp(cum)).astype(BF16)
    o = o + lax.dot_general(q_in, st.astype(BF16), (((1,), (1,)), ((), ())), preferred_element_type=F32)
    last = cum[c - 1:c, :]
    k_out = (kh * jnp.exp(last - cum)).astype(BF16)
    st_new = st * jnp.exp(last) + lax.dot_general(vb, k_out, (((0,), (0,)), ((), ())),
                                                  preferred_element_type=F32)
    return o, st_new


def _hgrn_kernel(q_ref, f_ref, v_ref, gb_ref, lb_ref, gn_ref, o_ref, st_ref):
    @pl.when(pl.program_id(2) == 0)
    def _():
        st_ref[...] = jnp.zeros(st_ref.shape, F32)

    lb = lb_ref[...]
    st = st_ref[...]
    for ci in range(q_ref.shape[1] // HGRN_CHUNK):
        rows = slice(ci * HGRN_CHUNK, (ci + 1) * HGRN_CHUNK)
        o, st = _hgrn_chunk(q_ref[0, rows, :], f_ref[0, rows, :], v_ref[0, rows, :], lb, st)
        o = o * _rms_scale(o) * gn_ref[...]
        o_ref[0, rows, :] = (o * _silu(gb_ref[0, rows, :])).astype(BF16)
    st_ref[...] = st


def _hgrn(proj, lower_bound, head_norm, col0, heads, tb):
    b, s, _ = proj.shape
    tb = min(tb, s)
    c0 = col0 // HGRN_DK
    part = lambda p: pl.BlockSpec((1, tb, HGRN_DK), lambda i, h, j: (i, j, c0 + p * heads + h))
    vec = lambda: pl.BlockSpec((1, HGRN_DK), lambda i, h, j: (0, h))
    w = heads * HGRN_DK
    return pl.pallas_call(
        _hgrn_kernel,
        grid=(b, heads, s // tb),
        in_specs=[part(0), part(1), part(2), part(3), vec(), vec()],
        out_specs=pl.BlockSpec((1, tb, HGRN_DK), lambda i, h, j: (i, j, h)),
        out_shape=jax.ShapeDtypeStruct((b, s, w), BF16),
        scratch_shapes=[pltpu.VMEM((HGRN_DK, HGRN_DK), F32)],
        compiler_params=_cparams("parallel", "parallel", "arbitrary"),
        name="hgrn",
    )(proj, proj, proj, proj, lower_bound.reshape(1, w), head_norm.reshape(1, w))


def _ssd_kernel(z_ref, x_ref, b_ref, c_ref, dtc_ref, dtr_ref,
                cwx_ref, cbx_ref, cwb_ref, cbb_ref, cwc_ref, cbc_ref,
                dbc_ref, dbr_ref, anc_ref, anr_ref, dsk_ref, nw_ref,
                o_ref, st_ref, y_ref, bufx_ref, bufb_ref, bufc_ref):
    @pl.when(pl.program_id(2) == 0)
    def _():
        st_ref[...] = jnp.zeros(st_ref.shape, F32)
        bufx_ref[0:SUBLANES, :] = jnp.zeros((SUBLANES, bufx_ref.shape[1]), F32)
        bufb_ref[0:SUBLANES, :] = jnp.zeros((SUBLANES, bufb_ref.shape[1]), F32)
        bufc_ref[0:SUBLANES, :] = jnp.zeros((SUBLANES, bufc_ref.shape[1]), F32)

    t = x_ref.shape[1]
    p = SSD_HEADDIM
    xs = _silu(_causal_conv(x_ref[0], bufx_ref, cwx_ref, cbx_ref))
    bm = _silu(_causal_conv(b_ref[0], bufb_ref, cwb_ref, cbb_ref))
    cm = _silu(_causal_conv(c_ref[0], bufc_ref, cwc_ref, cbc_ref))
    bmb = bm.astype(BF16)
    cmb = cm.astype(BF16)

    dt_c = _softplus(dtc_ref[0, 0] + dbc_ref[0])
    cum_c = _cumsum(dt_c * anc_ref[0], 0)
    dt_r = _softplus(dtr_ref[0, 0] + dbr_ref[0])
    cum_r = _cumsum(dt_r * anr_ref[0], 1)
    last_c = cum_c[t - 1:t, :]

    cb = lax.dot_general(cmb, bmb, (((1,), (1,)), ((), ())), preferred_element_type=F32)
    y_in = jnp.dot(cmb, st_ref[...].astype(BF16), preferred_element_type=F32)
    causal = (lax.broadcasted_iota(jnp.int32, (t, t), 0) >= lax.broadcasted_iota(jnp.int32, (t, t), 1))

    xw_parts = []
    for h in range(SSD_HPG):
        sl = slice(h * p, (h + 1) * p)
        cum_h = cum_c[:, h:h + 1]
        xdt = xs[:, sl] * dt_c[:, h:h + 1]
        seg = cum_h - cum_r[h:h + 1, :]
        dec = jnp.exp(jnp.where(causal, seg, NEG_INF))
        y = jnp.dot((cb * dec).astype(BF16), xdt.astype(BF16), preferred_element_type=F32)
        y = y + y_in[:, sl] * jnp.exp(cum_h)
        y = y + dsk_ref[0][:, h:h + 1] * xs[:, sl]
        y_ref[:, sl] = y
        xw_parts.append((xdt * jnp.exp(last_c[:, h:h + 1] - cum_h)).astype(BF16))
        st_ref[:, sl] = st_ref[:, sl] * jnp.exp(last_c[:, h:h + 1])
    xw = jnp.concatenate(xw_parts, axis=1)
    st_ref[...] += lax.dot_general(bmb, xw, (((0,), (0,)), ((), ())), preferred_element_type=F32)

    y = y_ref[...] * _silu(z_ref[0])
    o_ref[0] = (y * _rms_scale(y) * nw_ref[...]).astype(BF16)


def _ssd(proj, dt_raw, conv_w, conv_b, dt_bias, a_neg, d_skip, norm_w, inner, groups):
    b, s, _ = proj.shape
    t = min(SSD_CHUNK, s)
    gw = inner // groups
    n = SSD_STATE
    hpg = SSD_HPG
    dt_g = dt_raw.reshape(b, s, groups, hpg)
    dt_col = dt_g.transpose(0, 2, 1, 3)
    dt_row = dt_g.transpose(0, 2, 3, 1)
    per_head = lambda a: (a.reshape(groups, 1, hpg), a.reshape(groups, hpg, 1))
    db_c, db_r = per_head(dt_bias)
    an_c, an_r = per_head(a_neg)
    ds_c, _ = per_head(d_skip)
    cw = conv_w
    cbias = conv_b.reshape(1, -1)
    xoff, boff, coff = inner // gw, (2 * inner) // n, (2 * inner + groups * n) // n
    cboff, ccoff = inner // n, (inner + groups * n) // n
    col = lambda shape: pl.BlockSpec((1,) + shape, lambda i, g, j: (g, 0, 0))
    return pl.pallas_call(
        _ssd_kernel,
        grid=(b, groups, s // t),
        in_specs=[
            pl.BlockSpec((1, t, gw), lambda i, g, j: (i, j, g)),
            pl.BlockSpec((1, t, gw), lambda i, g, j: (i, j, xoff + g)),
            pl.BlockSpec((1, t, n), lambda i, g, j: (i, j, boff + g)),
            pl.BlockSpec((1, t, n), lambda i, g, j: (i, j, coff + g)),
            pl.BlockSpec((1, 1, t, hpg), lambda i, g, j: (i, g, j, 0)),
            pl.BlockSpec((1, 1, hpg, t), lambda i, g, j: (i, g, 0, j)),
            pl.BlockSpec((CONV_K, gw), lambda i, g, j: (0, g)),
            pl.BlockSpec((1, gw), lambda i, g, j: (0, g)),
            pl.BlockSpec((CONV_K, n), lambda i, g, j: (0, cboff + g)),
            pl.BlockSpec((1, n), lambda i, g, j: (0, cboff + g)),
            pl.BlockSpec((CONV_K, n), lambda i, g, j: (0, ccoff + g)),
            pl.BlockSpec((1, n), lambda i, g, j: (0, ccoff + g)),
            col((1, hpg)), col((hpg, 1)), col((1, hpg)), col((hpg, 1)), col((1, hpg)),
            pl.BlockSpec((1, gw), lambda i, g, j: (0, g)),
        ],
        out_specs=pl.BlockSpec((1, t, gw), lambda i, g, j: (i, j, g)),
        out_shape=jax.ShapeDtypeStruct((b, s, inner), BF16),
        scratch_shapes=[
            pltpu.VMEM((n, gw), F32),
            pltpu.VMEM((t, gw), F32),
            pltpu.VMEM((t + SUBLANES, gw), F32),
            pltpu.VMEM((t + SUBLANES, n), F32),
            pltpu.VMEM((t + SUBLANES, n), F32),
        ],
        compiler_params=_cparams("parallel", "parallel", "arbitrary"),
    )(proj, proj, proj, proj, dt_col, dt_row, cw, cbias, cw, cbias, cw, cbias,
      db_c, db_r, an_c, an_r, ds_c, norm_w.reshape(1, inner))


def _split3(x):
    hi = x.astype(BF16).astype(F32)
    rem = x - hi
    mid = rem.astype(BF16).astype(F32)
    return hi, mid, rem - mid


def _ssd2_kernel(z_ref, x_ref, b_ref, c_ref, dtr_ref,
                 cwx_ref, cbx_ref, cwb_ref, cbb_ref, cwc_ref, cbc_ref,
                 dbr_ref, anr_ref, dsk_ref, nw_ref, tri_ref, sel_ref,
                 o_ref, st_ref, y_ref, bufx_ref, bufb_ref, bufc_ref):
    @pl.when(pl.program_id(2) == 0)
    def _():
        st_ref[...] = jnp.zeros(st_ref.shape, F32)
        bufx_ref[0:SUBLANES, :] = jnp.zeros((SUBLANES, bufx_ref.shape[1]), F32)
        bufb_ref[0:SUBLANES, :] = jnp.zeros((SUBLANES, bufb_ref.shape[1]), F32)
        bufc_ref[0:SUBLANES, :] = jnp.zeros((SUBLANES, bufc_ref.shape[1]), F32)

    t = x_ref.shape[1]
    n = st_ref.shape[0]
    xs = _silu(_causal_conv(x_ref[0], bufx_ref, cwx_ref, cbx_ref))
    bm = _silu(_causal_conv(b_ref[0], bufb_ref, cwb_ref, cbb_ref))
    cm = _silu(_causal_conv(c_ref[0], bufc_ref, cwc_ref, cbc_ref))
    xsb = xs.astype(BF16)

    dtr = _softplus(dtr_ref[0, 0] + dbr_ref[...])
    dta = dtr * anr_ref[...]
    parts = jnp.concatenate(list(_split3(dta)) + [jnp.zeros_like(dta)], axis=0).astype(BF16)
    c3 = jnp.dot(parts, tri_ref[...], preferred_element_type=F32)
    hpg = dta.shape[0]
    cum = c3[0:hpg] + c3[hpg:2 * hpg] + c3[2 * hpg:3 * hpg]
    pad = jnp.zeros((sel_ref.shape[0] - 3 * hpg, t), F32)
    cols = jnp.concatenate(list(_split3(cum)) + [pad], axis=0).astype(BF16)
    bc_all = lax.dot_general(cols, sel_ref[...], (((0,), (0,)), ((), ())), preferred_element_type=F32)
    row_term = cum - jnp.log(dtr)
    last = jnp.broadcast_to(cum[:, t - 1:t], cum.shape)
    row_scale = jnp.exp(last - cum) * dtr
    e_last = jnp.exp(bc_all[t - 1:t, :])

    cb = lax.dot_general(cm.astype(BF16), bm.astype(BF16), (((1,), (1,)), ((), ())),
                         preferred_element_type=F32)
    bt = bm.T
    causal = (lax.broadcasted_iota(jnp.int32, (t, t), 0) >= lax.broadcasted_iota(jnp.int32, (t, t), 1))
    low_t = lax.broadcasted_iota(jnp.int32, (t, LANES), 1) < SSD_HEADDIM
    low_n = lax.broadcasted_iota(jnp.int32, (n, LANES), 1) < SSD_HEADDIM

    for pair in range(hpg // 2):
        pc = slice(pair * LANES, (pair + 1) * LANES)
        rhs = jnp.concatenate([xsb[:, pc], st_ref[:, pc].astype(BF16)], axis=0)
        lhs, zl = [], []
        for h in (2 * pair, 2 * pair + 1):
            bc = bc_all[:, h * LANES:(h + 1) * LANES]
            dec = jnp.exp(jnp.where(causal, bc - row_term[h:h + 1, :], NEG_INF))
            lhs.append(jnp.concatenate([cb * dec, cm * jnp.exp(bc)], axis=1).astype(BF16))
            zl.append((bt * row_scale[h:h + 1, :]).astype(BF16))
        y2 = jnp.dot(jnp.concatenate(lhs, axis=0), rhs, preferred_element_type=F32)
        y_ref[:, pc] = jnp.where(low_t, y2[:t], y2[t:]) + dsk_ref[:, pc] * xs[:, pc]
        z2 = jnp.dot(jnp.concatenate(zl, axis=0), xsb[:, pc], preferred_element_type=F32)
        st = st_ref[:, pc]
        ea = e_last[:, 2 * pair * LANES:(2 * pair + 1) * LANES]
        eb = e_last[:, (2 * pair + 1) * LANES:(2 * pair + 2) * LANES]
        st_ref[:, pc] = jnp.where(low_n, st * ea + z2[:n], st * eb + z2[n:])

    y = y_ref[...] * _silu(z_ref[0])
    o_ref[0] = (y * _rms_scale(y) * nw_ref[...]).astype(BF16)


def _ssd2(proj, dt_raw, conv_w, conv_b, dt_bias, a_neg, d_skip, norm_w, inner, groups):
    b, s, _ = proj.shape
    t = min(SSD_CHUNK, s)
    gw = inner // groups
    n = SSD_STATE
    hpg = SSD_HPG
    assert gw == hpg * SSD_HEADDIM and 2 * SSD_HEADDIM == LANES
    dt_row = dt_raw.reshape(b, s, groups, hpg).transpose(0, 2, 3, 1)
    db_r = dt_bias.reshape(groups, hpg, 1)
    an_r = a_neg.reshape(groups, hpg, 1)
    dsk = jnp.repeat(d_skip, SSD_HEADDIM).reshape(1, inner)
    tri = (jnp.arange(t)[:, None] <= jnp.arange(t)[None, :]).astype(BF16)
    krow = jnp.arange(LANES)[:, None]
    sel = ((krow < 3 * hpg) & ((krow % hpg) == (jnp.arange(hpg * LANES)[None, :] // LANES))).astype(BF16)
    cbias = conv_b.reshape(1, -1)
    xoff, boff, coff = inner // gw, (2 * inner) // n, (2 * inner + groups * n) // n
    cboff, ccoff = inner // n, (inner + groups * n) // n
    col = lambda: pl.BlockSpec((None, hpg, 1), lambda i, g, j: (g, 0, 0))
    const = lambda a: pl.BlockSpec(a.shape, lambda i, g, j: (0, 0))
    return pl.pallas_call(
        _ssd2_kernel,
        grid=(b, groups, s // t),
        in_specs=[
            pl.BlockSpec((1, t, gw), lambda i, g, j: (i, j, g)),
            pl.BlockSpec((1, t, gw), lambda i, g, j: (i, j, xoff + g)),
            pl.BlockSpec((1, t, n), lambda i, g, j: (i, j, boff + g)),
            pl.BlockSpec((1, t, n), lambda i, g, j: (i, j, coff + g)),
            pl.BlockSpec((1, 1, hpg, t), lambda i, g, j: (i, g, 0, j)),
            pl.BlockSpec((CONV_K, gw), lambda i, g, j: (0, g)),
            pl.BlockSpec((1, gw), lambda i, g, j: (0, g)),
            pl.BlockSpec((CONV_K, n), lambda i, g, j: (0, cboff + g)),
            pl.BlockSpec((1, n), lambda i, g, j: (0, cboff + g)),
            pl.BlockSpec((CONV_K, n), lambda i, g, j: (0, ccoff + g)),
            pl.BlockSpec((1, n), lambda i, g, j: (0, ccoff + g)),
            col(), col(),
            pl.BlockSpec((1, gw), lambda i, g, j: (0, g)),
            pl.BlockSpec((1, gw), lambda i, g, j: (0, g)),
            const(tri), const(sel),
        ],
        out_specs=pl.BlockSpec((1, t, gw), lambda i, g, j: (i, j, g)),
        out_shape=jax.ShapeDtypeStruct((b, s, inner), BF16),
        scratch_shapes=[
            pltpu.VMEM((n, gw), F32),
            pltpu.VMEM((t, gw), F32),
            pltpu.VMEM((t + SUBLANES, gw), F32),
            pltpu.VMEM((t + SUBLANES, n), F32),
            pltpu.VMEM((t + SUBLANES, n), F32),
        ],
        compiler_params=_cparams("parallel", "parallel", "arbitrary"),
        name="ssd",
    )(proj, proj, proj, proj, dt_row, conv_w, cbias, conv_w, cbias, conv_w, cbias,
      db_r, an_r, dsk, norm_w.reshape(1, inner), tri, sel)


def _rmsnorm_kernel(x_ref, g_ref, o_ref):
    x = x_ref[...]
    o_ref[...] = x * _rms_scale(x) * g_ref[...]


def _rmsnorm(x, g, tm):
    m, d = x.shape
    tm = min(tm, m)
    return pl.pallas_call(
        _rmsnorm_kernel,
        grid=(m // tm,),
        in_specs=[pl.BlockSpec((tm, d), lambda i: (i, 0)), pl.BlockSpec((1, d), lambda i: (0, 0))],
        out_specs=pl.BlockSpec((tm, d), lambda i: (i, 0)),
        out_shape=jax.ShapeDtypeStruct((m, d), F32),
        compiler_params=_cparams("parallel"),
    )(x, g.reshape(1, d))


TM = 1024
TN = 1024
FFN_TH = 512
XA_TM = 512
LRU_TB = 256
HGRN_TB = 256


def kernel(x, mem, norm_mix, norm_xattn, norm_ffn, norm_mem, norm_final, ab_w_in, ab_w_out, lru_conv_w, lru_conv_b, lru_w_r, lru_b_r, lru_w_i, lru_b_i, lru_lambda, hgrn_lower_bounds, hgrn_norm, ssd_w_in, ssd_w_out, ssd_conv_w, ssd_conv_b, ssd_dt_bias, ssd_a_log, ssd_d, ssd_norm, xa_w_q, xa_w_kv, xa_w_o, ffn_w_gate, ffn_w_up, ffn_w_down):
    bsz, seq, d = x.shape
    depth = norm_mix.shape[0]
    m = bsz * seq
    lru_w = lru_conv_w.shape[2]
    hgrn_w = hgrn_norm.shape[1]
    ssd_heads = ssd_a_log.shape[1]
    ssd_inner = ssd_norm.shape[1]
    ssd_groups = ssd_heads // SSD_HPG
    ssd_main = ssd_w_in.shape[2] - ssd_heads

    sm = jax.nn.softmax(hgrn_lower_bounds.astype(F32), axis=0)
    lower_bounds = jnp.cumsum(sm, axis=0) - sm[0]
    lru_sp = jax.nn.softplus(-lru_lambda.astype(F32))
    ssd_a_neg = -jnp.exp(ssd_a_log.astype(F32))
    lru_w_ri = jnp.concatenate([lru_w_r, lru_w_i], axis=-1).astype(BF16)
    ab_in_b = _cast_bf16(ab_w_in)
    ab_out_b = _cast_bf16(ab_w_out)
    ssd_in_b, ssd_dt_b = _cast_bf16(ssd_w_in, (ssd_main, ssd_heads))
    ssd_out_b = _cast_bf16(ssd_w_out)
    wq_b = _cast_bf16(xa_w_q)
    wkv_b = _cast_bf16(xa_w_kv)
    wo_b = _cast_bf16(xa_w_o)
    wg_b = _cast_bf16(ffn_w_gate)
    wu_b = _cast_bf16(ffn_w_up)
    wd_b = _cast_bf16(ffn_w_down)

    x2 = x.reshape(m, d)
    mem_len = mem.shape[1]
    kv = _kv_proj(mem.reshape(bsz * mem_len, d), norm_mem, wkv_b, TN).reshape(depth, bsz, mem_len, 2 * d)

    for layer in range(depth):
        if layer % 2 == 0:
            e = layer // 2
            proj = _norm_matmul(x2, norm_mix[layer], ab_in_b, e, F32, TM, TN, "ab_in_proj")
            proj = proj.reshape(bsz, seq, -1)
            ya = _lru(proj, lru_conv_w[e], lru_conv_b[e], lru_w_ri[e], lru_b_r[e], lru_b_i[e], lru_sp[e], LRU_TB)
            yb = _hgrn(proj, lower_bounds[e], hgrn_norm[e], 2 * lru_w, hgrn_w // HGRN_DK, HGRN_TB)
            x2 = _matmul2_res(ya.reshape(m, lru_w), yb.reshape(m, hgrn_w), ab_out_b, e, x2, TM, TN, "ab_out_proj")
        else:
            o = layer // 2
            proj = _norm_matmul(x2, norm_mix[layer], ssd_in_b, o, F32, TM, TN, "ssd_in_proj")
            dt_raw = _norm_matmul(x2, norm_mix[layer], ssd_dt_b, o, F32, TM, TN, "ssd_dt_proj")
            y = _ssd2(proj.reshape(bsz, seq, ssd_main), dt_raw.reshape(bsz, seq, ssd_heads),
                      ssd_conv_w[o], ssd_conv_b[o], ssd_dt_bias[o], ssd_a_neg[o], ssd_d[o], ssd_norm[o],
                      ssd_inner, ssd_groups)
            x2 = _matmul_res(y.reshape(m, ssd_inner), ssd_out_b, o, x2, TM, TN // 2, "ssd_out_proj")
        q = _norm_matmul(x2, norm_xattn[layer], wq_b, layer, BF16, TM, TN, "xa_q_proj")
        x2 = _xattn(q.reshape(bsz, seq, d), kv, layer, wo_b, x2.reshape(bsz, seq, d), XA_TM)
        x2 = x2.reshape(m, d)
        x2 = _ffn(x2, norm_ffn[layer], wg_b, wu_b, wd_b, layer, TM, FFN_TH)
    return _rmsnorm(x2, norm_final, TM).reshape(bsz, seq, d)
```

```python
import functools
import math

import jax
import jax.numpy as jnp
from jax import lax
from jax.experimental import pallas as pl
from jax.experimental.pallas import tpu as pltpu

F32 = jnp.float32
BF16 = jnp.bfloat16
EPS = 1e-6
NEG_INF = float("-inf")

VMEM_LIMIT_BYTES = 56 * 1024 * 1024
CAST_BLOCK_BYTES = 4 * 1024 * 1024
SUBLANES = 8
LANES = 128

CONV_K = 4
LRU_BLOCK = 128
LRU_C = 8.0
HGRN_DK = 128
HGRN_CHUNK = 64
HGRN_SUB = SUBLANES
SSD_HEADDIM = 64
SSD_HPG = 8
SSD_STATE = 128
SSD_CHUNK = 128
XA_HEADS = 4
MEM_LEN = 256


def _cparams(*sem):
    return pltpu.CompilerParams(dimension_semantics=sem, vmem_limit_bytes=VMEM_LIMIT_BYTES)


def _rms_scale(x):
    return lax.rsqrt(jnp.mean(x * x, axis=-1, keepdims=True) + EPS)


def _silu(x):
    return x * jax.nn.sigmoid(x)


def _gelu_tanh(x):
    c = math.sqrt(2.0 / math.pi)
    return x * (0.5 * (1.0 + jnp.tanh(c * (x + 0.044715 * (x * x * x)))))


def _softplus(x):
    return jnp.maximum(x, 0.0) + jnp.log1p(jnp.exp(-jnp.abs(x)))


def _prefix_rows(a, b):
    n = a.shape[0]
    row = lax.broadcasted_iota(jnp.int32, a.shape, 0)
    s = 1
    while s < n:
        keep = row >= s
        a_sh = pltpu.roll(a, s, 0)
        b_sh = pltpu.roll(b, s, 0)
        b = jnp.where(keep, a * b_sh + b, b)
        a = jnp.where(keep, a * a_sh, a)
        s *= 2
    return a, b


def _cumsum(x, axis):
    n = x.shape[axis]
    idx = lax.broadcasted_iota(jnp.int32, x.shape, axis)
    s = 1
    while s < n:
        x = jnp.where(idx >= s, x + pltpu.roll(x, s, axis), x)
        s *= 2
    return x


def _causal_conv(u, buf_ref, w_ref, b_ref):
    t = u.shape[0]
    buf_ref[SUBLANES:SUBLANES + t, :] = u
    out = b_ref[...] + w_ref[3:4, :] * u
    for j in range(1, CONV_K):
        out = out + w_ref[CONV_K - 1 - j:CONV_K - j, :] * buf_ref[SUBLANES - j:SUBLANES - j + t, :]
    buf_ref[0:SUBLANES, :] = u[t - SUBLANES:t, :]
    return out


def _norm_matmul_kernel(x_ref, g_ref, w_ref, o_ref, xn_ref):
    @pl.when(pl.program_id(1) == 0)
    def _():
        x = x_ref[...]
        xn_ref[...] = (x * _rms_scale(x) * g_ref[...]).astype(BF16)

    o_ref[...] = jnp.dot(xn_ref[...], w_ref[...], preferred_element_type=F32).astype(o_ref.dtype)


def _norm_matmul(x, g, w, layer, out_dtype, tm, tn, name):
    m, k = x.shape
    n = w.shape[2]
    tm = min(tm, m)
    tn = min(tn, n)
    return pl.pallas_call(
        _norm_matmul_kernel,
        grid=(m // tm, n // tn),
        in_specs=[
            pl.BlockSpec((tm, k), lambda i, j: (i, 0)),
            pl.BlockSpec((1, k), lambda i, j: (0, 0)),
            pl.BlockSpec((None, k, tn), lambda i, j: (layer, 0, j)),
        ],
        out_specs=pl.BlockSpec((tm, tn), lambda i, j: (i, j)),
        out_shape=jax.ShapeDtypeStruct((m, n), out_dtype),
        scratch_shapes=[pltpu.VMEM((tm, k), BF16)],
        compiler_params=_cparams("parallel", "arbitrary"),
        name=name,
    )(x, g.reshape(1, k), w)


def _norm_matmul_side_kernel(x_ref, g_ref, w_ref, ws_ref, o_ref, os_ref, xn_ref):
    @pl.when(pl.program_id(1) == 0)
    def _():
        x = x_ref[...]
        xn = (x * _rms_scale(x) * g_ref[...]).astype(BF16)
        xn_ref[...] = xn
        os_ref[...] = jnp.dot(xn, ws_ref[...], preferred_element_type=F32)

    o_ref[...] = jnp.dot(xn_ref[...], w_ref[...], preferred_element_type=F32).astype(o_ref.dtype)


def _norm_matmul_side(x, g, w, w_side, layer, tm, tn, name):
    m, k = x.shape
    n = w.shape[2]
    ns = w_side.shape[2]
    tm = min(tm, m)
    tn = min(tn, n)
    return pl.pallas_call(
        _norm_matmul_side_kernel,
        grid=(m // tm, n // tn),
        in_specs=[
            pl.BlockSpec((tm, k), lambda i, j: (i, 0)),
            pl.BlockSpec((1, k), lambda i, j: (0, 0)),
            pl.BlockSpec((None, k, tn), lambda i, j: (layer, 0, j)),
            pl.BlockSpec((None, k, ns), lambda i, j: (layer, 0, 0)),
        ],
        out_specs=[pl.BlockSpec((tm, tn), lambda i, j: (i, j)), pl.BlockSpec((tm, ns), lambda i, j: (i, 0))],
        out_shape=[jax.ShapeDtypeStruct((m, n), F32), jax.ShapeDtypeStruct((m, ns), F32)],
        scratch_shapes=[pltpu.VMEM((tm, k), BF16)],
        compiler_params=_cparams("parallel", "arbitrary"),
        name=name,
    )(x, g.reshape(1, k), w, w_side)


def _kv_proj_kernel(x_ref, g_ref, w_ref, o_ref, xn_ref):
    @pl.when((pl.program_id(0) == 0) & (pl.program_id(1) == 0))
    def _():
        x = x_ref[...]
        xn_ref[...] = (x * _rms_scale(x) * g_ref[...]).astype(BF16)

    o_ref[...] = jnp.dot(xn_ref[...], w_ref[...], preferred_element_type=F32).astype(o_ref.dtype)


def _kv_proj(x, g, w, tn):
    m, k = x.shape
    nl, _, n = w.shape
    return pl.pallas_call(
        _kv_proj_kernel,
        grid=(nl, n // tn),
        in_specs=[
            pl.BlockSpec((m, k), lambda l, j: (0, 0)),
            pl.BlockSpec((1, k), lambda l, j: (0, 0)),
            pl.BlockSpec((None, k, tn), lambda l, j: (l, 0, j)),
        ],
        out_specs=pl.BlockSpec((None, m, tn), lambda l, j: (l, 0, j)),
        out_shape=jax.ShapeDtypeStruct((nl, m, n), BF16),
        scratch_shapes=[pltpu.VMEM((m, k), BF16)],
        compiler_params=_cparams("arbitrary", "arbitrary"),
        name="kv_proj",
    )(x, g.reshape(1, k), w)


def _cast_kernel(x_ref, *o_refs):
    off = 0
    for o_ref in o_refs:
        width = o_ref.shape[-1]
        o_ref[...] = x_ref[:, off:off + width].astype(BF16)
        off += width


def _cast_bf16(w, splits=None):
    nl, k, n = w.shape
    splits = splits or (n,)
    assert sum(splits) == n
    tk = SUBLANES
    while tk * 2 <= k and k % (tk * 2) == 0 and tk * 2 * n * 4 <= CAST_BLOCK_BYTES:
        tk *= 2
    outs = pl.pallas_call(
        _cast_kernel,
        grid=(nl, k // tk),
        in_specs=[pl.BlockSpec((None, tk, n), lambda l, i: (l, i, 0))],
        out_specs=[pl.BlockSpec((None, tk, s), lambda l, i: (l, i, 0)) for s in splits],
        out_shape=[jax.ShapeDtypeStruct((nl, k, s), BF16) for s in splits],
        compiler_params=_cparams("parallel", "parallel"),
        name="cast_bf16",
    )(w)
    return outs if len(splits) > 1 else outs[0]


def _matmul_res_kernel(a_ref, w_ref, r_ref, o_ref):
    o_ref[...] = r_ref[...] + jnp.dot(a_ref[...], w_ref[...], preferred_element_type=F32)


def _matmul_res(a, w, layer, res, tm, tn, name):
    m, k = a.shape
    n = w.shape[2]
    tm = min(tm, m)
    tn = min(tn, n)
    return pl.pallas_call(
        _matmul_res_kernel,
        grid=(m // tm, n // tn),
        in_specs=[
            pl.BlockSpec((tm, k), lambda i, j: (i, 0)),
            pl.BlockSpec((None, k, tn), lambda i, j: (layer, 0, j)),
            pl.BlockSpec((tm, tn), lambda i, j: (i, j)),
        ],
        out_specs=pl.BlockSpec((tm, tn), lambda i, j: (i, j)),
        out_shape=jax.ShapeDtypeStruct((m, n), F32),
        compiler_params=_cparams("parallel", "arbitrary"),
        name=name,
    )(a, w, res)


def _matmul2_res_kernel(a1_ref, a2_ref, w1_ref, w2_ref, r_ref, o_ref):
    acc = jnp.dot(a1_ref[...], w1_ref[...], preferred_element_type=F32)
    acc = acc + jnp.dot(a2_ref[...], w2_ref[...], preferred_element_type=F32)
    o_ref[...] = r_ref[...] + acc


def _matmul2_res(a1, a2, w, layer, res, tm, tn, name):
    m, k1 = a1.shape
    k2 = a2.shape[1]
    assert k1 == k2
    n = w.shape[2]
    tm = min(tm, m)
    tn = min(tn, n)
    return pl.pallas_call(
        _matmul2_res_kernel,
        grid=(m // tm, n // tn),
        in_specs=[
            pl.BlockSpec((tm, k1), lambda i, j: (i, 0)),
            pl.BlockSpec((tm, k2), lambda i, j: (i, 0)),
            pl.BlockSpec((None, k1, tn), lambda i, j: (layer, 0, j)),
            pl.BlockSpec((None, k2, tn), lambda i, j: (layer, 1, j)),
            pl.BlockSpec((tm, tn), lambda i, j: (i, j)),
        ],
        out_specs=pl.BlockSpec((tm, tn), lambda i, j: (i, j)),
        out_shape=jax.ShapeDtypeStruct((m, n), F32),
        compiler_params=_cparams("parallel", "arbitrary"),
        name=name,
    )(a1, a2, w, w, res)


def _ffn_kernel(x_ref, g_ref, go_ref, wg_ref, wu_ref, wd_ref, o_ref, xn_ref, *, out_norm):
    @pl.when(pl.program_id(1) == 0)
    def _():
        x = x_ref[...]
        xn_ref[...] = (x * _rms_scale(x) * g_ref[...]).astype(BF16)
        o_ref[...] = x

    xn = xn_ref[...]
    gate = jnp.dot(xn, wg_ref[...], preferred_element_type=F32)
    up = jnp.dot(xn, wu_ref[...], preferred_element_type=F32)
    hid = (_silu(gate) * up).astype(BF16)
    o_ref[...] += jnp.dot(hid, wd_ref[...], preferred_element_type=F32)

    if out_norm:
        @pl.when(pl.program_id(1) == pl.num_programs(1) - 1)
        def _():
            y = o_ref[...]
            o_ref[...] = y * _rms_scale(y) * go_ref[...]


def _ffn(x, g, wg, wu, wd, layer, tm, th, g_out=None):
    m, d = x.shape
    hdim = wg.shape[2]
    tm = min(tm, m)
    out_norm = g_out is not None
    g_out = g if g_out is None else g_out
    return pl.pallas_call(
        functools.partial(_ffn_kernel, out_norm=out_norm),
        grid=(m // tm, hdim // th),
        in_specs=[
            pl.BlockSpec((tm, d), lambda i, j: (i, 0)),
            pl.BlockSpec((1, d), lambda i, j: (0, 0)),
            pl.BlockSpec((1, d), lambda i, j: (0, 0)),
            pl.BlockSpec((None, d, th), lambda i, j: (layer, 0, j)),
            pl.BlockSpec((None, d, th), lambda i, j: (layer, 0, j)),
            pl.BlockSpec((None, th, d), lambda i, j: (layer, j, 0)),
        ],
        out_specs=pl.BlockSpec((tm, d), lambda i, j: (i, 0)),
        out_shape=jax.ShapeDtypeStruct((m, d), F32),
        scratch_shapes=[pltpu.VMEM((tm, d), BF16)],
        compiler_params=_cparams("parallel", "arbitrary"),
        name="ffn",
    )(x, g.reshape(1, d), g_out.reshape(1, d), wg, wu, wd)


def _xattn_kernel(q_ref, k_ref, v_ref, wo_ref, x_ref, o_ref, ob_ref, *, heads, scale):
    hd = q_ref.shape[2] // heads
    for h in range(heads):
        sl = slice(h * hd, (h + 1) * hd)
        s = lax.dot_general(q_ref[0, :, sl], k_ref[0, :, sl], (((1,), (1,)), ((), ())),
                            preferred_element_type=F32) * scale
        p = jnp.exp(s - jnp.max(s, axis=-1, keepdims=True))
        p = p / jnp.sum(p, axis=-1, keepdims=True)
        ob_ref[:, sl] = jnp.dot(p.astype(BF16), v_ref[0, :, sl], preferred_element_type=F32).astype(BF16)
    o_ref[0] = x_ref[0] + jnp.dot(ob_ref[...], wo_ref[...], preferred_element_type=F32)


def _xattn(q, kv, layer, wo, x, tm):
    b, s, d = x.shape
    mem = kv.shape[2]
    tm = min(tm, s)
    kern = functools.partial(_xattn_kernel, heads=XA_HEADS, scale=(d // XA_HEADS) ** -0.5)
    return pl.pallas_call(
        kern,
        grid=(b, s // tm),
        in_specs=[
            pl.BlockSpec((1, tm, d), lambda i, j: (i, j, 0)),
            pl.BlockSpec((None, 1, mem, d), lambda i, j: (layer, i, 0, 0)),
            pl.BlockSpec((None, 1, mem, d), lambda i, j: (layer, i, 0, 1)),
            pl.BlockSpec((None, d, d), lambda i, j: (layer, 0, 0)),
            pl.BlockSpec((1, tm, d), lambda i, j: (i, j, 0)),
        ],
        out_specs=pl.BlockSpec((1, tm, d), lambda i, j: (i, j, 0)),
        out_shape=jax.ShapeDtypeStruct((b, s, d), F32),
        scratch_shapes=[pltpu.VMEM((tm, d), BF16)],
        compiler_params=_cparams("parallel", "arbitrary"),
        name="xattn",
    )(q, kv, kv, wo, x)


def _lru_kernel(xa_ref, ga_ref, cw_ref, cb_ref, wri_ref, br_ref, bi_ref, sp_ref, o_ref, buf_ref, h_ref):
    @pl.when(pl.program_id(1) == 0)
    def _():
        buf_ref[0:SUBLANES, :] = jnp.zeros((SUBLANES, buf_ref.shape[1]), F32)
        h_ref[...] = jnp.zeros(h_ref.shape, F32)

    xc = _causal_conv(xa_ref[0], buf_ref, cw_ref, cb_ref)
    t = xc.shape[0]
    for blk in range(xc.shape[1] // LRU_BLOCK):
        sl = slice(blk * LRU_BLOCK, (blk + 1) * LRU_BLOCK)
        xb = xc[:, sl]
        pre = jnp.dot(xb.astype(BF16), wri_ref[blk], preferred_element_type=F32)
        r_gate = jax.nn.sigmoid(pre[:, :LRU_BLOCK] + br_ref[:, sl])
        i_gate = jax.nn.sigmoid(pre[:, LRU_BLOCK:] + bi_ref[:, sl])
        log_a = (-LRU_C) * r_gate * sp_ref[:, sl]
        a = jnp.exp(log_a)
        mult = jnp.sqrt(-jnp.tanh(log_a) * (a * a + 1.0))
        a_cum, h_loc = _prefix_rows(a, mult * i_gate * xb)
        h = a_cum * h_ref[0:1, sl] + h_loc
        h_ref[0:1, sl] = h[t - 1:t, :]
        o_ref[0, :, sl] = (_gelu_tanh(ga_ref[0, :, sl]) * h).astype(BF16)


def _lru(proj, conv_w, conv_b, w_ri, b_r, b_i, sp, tb):
    b, s, _ = proj.shape
    w = conv_w.shape[1]
    tb = min(tb, s)
    vec = lambda: pl.BlockSpec((1, w), lambda i, j: (0, 0))
    return pl.pallas_call(
        _lru_kernel,
        grid=(b, s // tb),
        in_specs=[
            pl.BlockSpec((1, tb, w), lambda i, j: (i, j, 0)),
            pl.BlockSpec((1, tb, w), lambda i, j: (i, j, 1)),
            pl.BlockSpec((CONV_K, w), lambda i, j: (0, 0)),
            vec(),
            pl.BlockSpec(w_ri.shape, lambda i, j: (0, 0, 0)),
            vec(), vec(), vec(),
        ],
        out_specs=pl.BlockSpec((1, tb, w), lambda i, j: (i, j, 0)),
        out_shape=jax.ShapeDtypeStruct((b, s, w), BF16),
        scratch_shapes=[pltpu.VMEM((tb + SUBLANES, w), F32), pltpu.VMEM((SUBLANES, w), F32)],
        compiler_params=_cparams("parallel", "arbitrary"),
        name="lru",
    )(proj, proj, conv_w, conv_b.reshape(1, w), w_ri, b_r.reshape(1, w), b_i.reshape(1, w), sp.reshape(1, w))


def _hgrn_chunk(q, f, v, lb, st):
    c, dk = q.shape
    nsub = c // HGRN_SUB
    qh = _silu(q)
    fg = lb + (1.0 - lb) * jax.nn.sigmoid(f)
    kh = 1.0 - fg
    g = jnp.log(fg)
    cum = _cumsum(g, 0)
    ex = cum - g
    cum3 = cum.reshape(nsub, HGRN_SUB, dk)
    ex3 = ex.reshape(nsub, HGRN_SUB, dk)
    base3 = jnp.broadcast_to(ex3[:, 0:1, :], cum3.shape)
    q3 = qh.reshape(nsub, HGRN_SUB, dk)
    k3 = kh.reshape(nsub, HGRN_SUB, dk)
    v3 = v.reshape(nsub, HGRN_SUB, dk)
    vb = v.astype(BF16)

    row3 = lax.broadcasted_iota(jnp.int32, cum3.shape, 1)
    terms = []
    for s in range(HGRN_SUB):
        diff = cum3 - cum3[:, s:s + 1, :]
        dec = jnp.exp(jnp.where(row3 >= s, diff, NEG_INF))
        terms.append((dec * q3 * k3[:, s:s + 1, :]).reshape(c, dk))
    stacked = jnp.concatenate(terms, axis=0).astype(BF16)
    ones = jnp.ones((dk, dk), BF16)
    summed = jnp.dot(stacked, ones, preferred_element_type=F32)
    o3 = jnp.zeros(cum3.shape, F32)
    for s in range(HGRN_SUB):
        o3 = o3 + summed[s * c:(s + 1) * c, :].reshape(nsub, HGRN_SUB, dk) * v3[:, s:s + 1, :]
    o = o3.reshape(c, dk)

    q_loc = (qh * jnp.exp(cum - base3.reshape(c, dk))).astype(BF16)
    pad = jnp.zeros((LANES, dk), F32)
    k_parts = []
    for i in range(1, nsub):
        n = i * HGRN_SUB
        base_i = ex[n:n + 1, :]
        k_parts.append(kh[0:n, :] * jnp.exp(base_i - cum[0:n, :]))
        k_parts.append(pad[0:LANES - n, :])
    k_hat = jnp.concatenate(k_parts, axis=0).astype(BF16)
    a_all = lax.dot_general(q_loc, k_hat, (((1,), (1,)), ((), ())), preferred_element_type=F32)
    a_rows = [jnp.zeros((HGRN_SUB, c), F32)]
    for i in range(1, nsub):
        a_rows.append(a_all[i * HGRN_SUB:(i + 1) * HGRN_SUB, (i - 1) * LANES:(i - 1) * LANES + c])
    a_off = jnp.concatenate(a_rows, axis=0).astype(BF16)
    o = o + jnp.dot(a_off, vb, preferred_element_type=F32)

    q_in = (qh * jnp.exp(cum)).astype(BF16)
    o = o + lax.dot_general(q_in, st.astype(BF16), (((1,), (1,)), ((), ())), preferred_element_type=F32)
    last = cum[c - 1:c, :]
    k_out = (kh * jnp.exp(last - cum)).astype(BF16)
    st_new = st * jnp.exp(last) + lax.dot_general(vb, k_out, (((0,), (0,)), ((), ())),
                                                  preferred_element_type=F32)
    return o, st_new


def _hgrn_kernel(q_ref, f_ref, v_ref, gb_ref, lb_ref, gn_ref, o_ref, st_ref):
    @pl.when(pl.program_id(2) == 0)
    def _():
        st_ref[...] = jnp.zeros(st_ref.shape, F32)

    lb = lb_ref[...]
    st = st_ref[...]
    for ci in range(q_ref.shape[1] // HGRN_CHUNK):
        rows = slice(ci * HGRN_CHUNK, (ci + 1) * HGRN_CHUNK)
        o, st = _hgrn_chunk(q_ref[0, rows, :], f_ref[0, rows, :], v_ref[0, rows, :], lb, st)
        o = o * _rms_scale(o) * gn_ref[...]
        o_ref[0, rows, :] = (o * _silu(gb_ref[0, rows, :])).astype(BF16)
    st_ref[...] = st


def _hgrn(proj, lower_bound, head_norm, col0, heads, tb):
    b, s, _ = proj.shape
    tb = min(tb, s)
    c0 = col0 // HGRN_DK
    part = lambda p: pl.BlockSpec((1, tb, HGRN_DK), lambda i, h, j: (i, j, c0 + p * heads + h))
    vec = lambda: pl.BlockSpec((1, HGRN_DK), lambda i, h, j: (0, h))
    w = heads * HGRN_DK
    return pl.pallas_call(
        _hgrn_kernel,
        grid=(b, heads, s // tb),
        in_specs=[part(0), part(1), part(2), part(3), vec(), vec()],
        out_specs=pl.BlockSpec((1, tb, HGRN_DK), lambda i, h, j: (i, j, h)),
        out_shape=jax.ShapeDtypeStruct((b, s, w), BF16),
        scratch_shapes=[pltpu.VMEM((HGRN_DK, HGRN_DK), F32)],
        compiler_params=_cparams("parallel", "parallel", "arbitrary"),
        name="hgrn",
    )(proj, proj, proj, proj, lower_bound.reshape(1, w), head_norm.reshape(1, w))


def _split3(x):
    hi = x.astype(BF16).astype(F32)
    rem = x - hi
    mid = rem.astype(BF16).astype(F32)
    return hi, mid, rem - mid


def _ssd2_kernel(z_ref, x_ref, b_ref, c_ref, dtr_ref,
                 cwx_ref, cbx_ref, cwb_ref, cbb_ref, cwc_ref, cbc_ref,
                 dbr_ref, anr_ref, dsk_ref, nw_ref, tri_ref, sel_ref,
                 o_ref, st_ref, y_ref, bufx_ref, bufb_ref, bufc_ref):
    @pl.when(pl.program_id(2) == 0)
    def _():
        st_ref[...] = jnp.zeros(st_ref.shape, F32)
        bufx_ref[0:SUBLANES, :] = jnp.zeros((SUBLANES, bufx_ref.shape[1]), F32)
        bufb_ref[0:SUBLANES, :] = jnp.zeros((SUBLANES, bufb_ref.shape[1]), F32)
        bufc_ref[0:SUBLANES, :] = jnp.zeros((SUBLANES, bufc_ref.shape[1]), F32)

    tb = x_ref.shape[1]
    t = tri_ref.shape[0]
    n = st_ref.shape[0]
    hpg = dbr_ref.shape[0]
    xs_all = _silu(_causal_conv(x_ref[0], bufx_ref, cwx_ref, cbx_ref))
    bm_all = _silu(_causal_conv(b_ref[0], bufb_ref, cwb_ref, cbb_ref))
    cm_all = _silu(_causal_conv(c_ref[0], bufc_ref, cwc_ref, cbc_ref))
    dtr_all = _softplus(dtr_ref[0, 0] + dbr_ref[...])
    causal = (lax.broadcasted_iota(jnp.int32, (t, t), 0) >= lax.broadcasted_iota(jnp.int32, (t, t), 1))
    low_t = lax.broadcasted_iota(jnp.int32, (t, LANES), 1) < SSD_HEADDIM
    low_n = lax.broadcasted_iota(jnp.int32, (n, LANES), 1) < SSD_HEADDIM
    pad = jnp.zeros((sel_ref.shape[0] - 3 * hpg, t), F32)

    for ci in range(tb // t):
        rows = slice(ci * t, (ci + 1) * t)
        xs, bm, cm, dtr = xs_all[rows], bm_all[rows], cm_all[rows], dtr_all[:, rows]
        xsb = xs.astype(BF16)
        dta = dtr * anr_ref[...]
        parts = jnp.concatenate(list(_split3(dta)) + [jnp.zeros_like(dta)], axis=0).astype(BF16)
        c3 = jnp.dot(parts, tri_ref[...], preferred_element_type=F32)
        cum = c3[0:hpg] + c3[hpg:2 * hpg] + c3[2 * hpg:3 * hpg]
        cols = jnp.concatenate(list(_split3(cum)) + [pad], axis=0).astype(BF16)
        bc_all = lax.dot_general(cols, sel_ref[...], (((0,), (0,)), ((), ())), preferred_element_type=F32)
        row_term = cum - jnp.log(dtr)
        last = jnp.broadcast_to(cum[:, t - 1:t], cum.shape)
        row_scale = jnp.exp(last - cum) * dtr
        e_last = jnp.exp(bc_all[t - 1:t, :])
        cb = lax.dot_general(cm.astype(BF16), bm.astype(BF16), (((1,), (1,)), ((), ())),
                             preferred_element_type=F32)
        bt = bm.T

        for pair in range(hpg // 2):
            pc = slice(pair * LANES, (pair + 1) * LANES)
            rhs = jnp.concatenate([xsb[:, pc], st_ref[:, pc].astype(BF16)], axis=0)
            lhs, zl = [], []
            for h in (2 * pair, 2 * pair + 1):
                bc = bc_all[:, h * LANES:(h + 1) * LANES]
                dec = jnp.exp(jnp.where(causal, bc - row_term[h:h + 1, :], NEG_INF))
                lhs.append(jnp.concatenate([cb * dec, cm * jnp.exp(bc)], axis=1).astype(BF16))
                zl.append((bt * row_scale[h:h + 1, :]).astype(BF16))
            y2 = jnp.dot(jnp.concatenate(lhs, axis=0), rhs, preferred_element_type=F32)
            y_ref[rows, pc] = jnp.where(low_t, y2[:t], y2[t:]) + dsk_ref[:, pc] * xs[:, pc]
            z2 = jnp.dot(jnp.concatenate(zl, axis=0), xsb[:, pc], preferred_element_type=F32)
            st = st_ref[:, pc]
            ea = e_last[:, 2 * pair * LANES:(2 * pair + 1) * LANES]
            eb = e_last[:, (2 * pair + 1) * LANES:(2 * pair + 2) * LANES]
            st_ref[:, pc] = jnp.where(low_n, st * ea + z2[:n], st * eb + z2[n:])

    y = y_ref[...] * _silu(z_ref[0])
    o_ref[0] = (y * _rms_scale(y) * nw_ref[...]).astype(BF16)


def _ssd2(proj, dt_raw, conv_w, conv_b, dt_bias, a_neg, d_skip, norm_w, inner, groups):
    b, s, _ = proj.shape
    t = min(SSD_CHUNK, s)
    tb = min(SSD_TB, s)
    gw = inner // groups
    n = SSD_STATE
    hpg = SSD_HPG
    assert gw == hpg * SSD_HEADDIM and 2 * SSD_HEADDIM == LANES and tb % t == 0
    dt_row = dt_raw.reshape(b, s, groups, hpg).transpose(0, 2, 3, 1)
    db_r = dt_bias.reshape(groups, hpg, 1)
    an_r = a_neg.reshape(groups, hpg, 1)
    dsk = jnp.repeat(d_skip, SSD_HEADDIM).reshape(1, inner)
    tri = (jnp.arange(t)[:, None] <= jnp.arange(t)[None, :]).astype(BF16)
    krow = jnp.arange(LANES)[:, None]
    sel = ((krow < 3 * hpg) & ((krow % hpg) == (jnp.arange(hpg * LANES)[None, :] // LANES))).astype(BF16)
    cbias = conv_b.reshape(1, -1)
    xoff, boff, coff = inner // gw, (2 * inner) // n, (2 * inner + groups * n) // n
    cboff, ccoff = inner // n, (inner + groups * n) // n
    col = lambda: pl.BlockSpec((None, hpg, 1), lambda i, g, j: (g, 0, 0))
    const = lambda a: pl.BlockSpec(a.shape, lambda i, g, j: (0, 0))
    return pl.pallas_call(
        _ssd2_kernel,
        grid=(b, groups, s // tb),
        in_specs=[
            pl.BlockSpec((1, tb, gw), lambda i, g, j: (i, j, g)),
            pl.BlockSpec((1, tb, gw), lambda i, g, j: (i, j, xoff + g)),
            pl.BlockSpec((1, tb, n), lambda i, g, j: (i, j, boff + g)),
            pl.BlockSpec((1, tb, n), lambda i, g, j: (i, j, coff + g)),
            pl.BlockSpec((1, 1, hpg, tb), lambda i, g, j: (i, g, 0, j)),
            pl.BlockSpec((CONV_K, gw), lambda i, g, j: (0, g)),
            pl.BlockSpec((1, gw), lambda i, g, j: (0, g)),
            pl.BlockSpec((CONV_K, n), lambda i, g, j: (0, cboff + g)),
            pl.BlockSpec((1, n), lambda i, g, j: (0, cboff + g)),
            pl.BlockSpec((CONV_K, n), lambda i, g, j: (0, ccoff + g)),
            pl.BlockSpec((1, n), lambda i, g, j: (0, ccoff + g)),
            col(), col(),
            pl.BlockSpec((1, gw), lambda i, g, j: (0, g)),
            pl.BlockSpec((1, gw), lambda i, g, j: (0, g)),
            const(tri), const(sel),
        ],
        out_specs=pl.BlockSpec((1, tb, gw), lambda i, g, j: (i, j, g)),
        out_shape=jax.ShapeDtypeStruct((b, s, inner), BF16),
        scratch_shapes=[
            pltpu.VMEM((n, gw), F32),
            pltpu.VMEM((tb, gw), F32),
            pltpu.VMEM((tb + SUBLANES, gw), F32),
            pltpu.VMEM((tb + SUBLANES, n), F32),
            pltpu.VMEM((tb + SUBLANES, n), F32),
        ],
        compiler_params=_cparams("parallel", "parallel", "arbitrary"),
        name="ssd",
    )(proj, proj, proj, proj, dt_row, conv_w, cbias, conv_w, cbias, conv_w, cbias,
      db_r, an_r, dsk, norm_w.reshape(1, inner), tri, sel)


TM = 1024
TN = 1024
FFN_TH = 512
XA_TM = 512
LRU_TB = 256
HGRN_TB = 512
SSD_TB = 256


def kernel(x, mem, norm_mix, norm_xattn, norm_ffn, norm_mem, norm_final, ab_w_in, ab_w_out, lru_conv_w, lru_conv_b, lru_w_r, lru_b_r, lru_w_i, lru_b_i, lru_lambda, hgrn_lower_bounds, hgrn_norm, ssd_w_in, ssd_w_out, ssd_conv_w, ssd_conv_b, ssd_dt_bias, ssd_a_log, ssd_d, ssd_norm, xa_w_q, xa_w_kv, xa_w_o, ffn_w_gate, ffn_w_up, ffn_w_down):
    bsz, seq, d = x.shape
    depth = norm_mix.shape[0]
    m = bsz * seq
    lru_w = lru_conv_w.shape[2]
    hgrn_w = hgrn_norm.shape[1]
    ssd_heads = ssd_a_log.shape[1]
    ssd_inner = ssd_norm.shape[1]
    ssd_groups = ssd_heads // SSD_HPG
    ssd_main = ssd_w_in.shape[2] - ssd_heads

    sm = jax.nn.softmax(hgrn_lower_bounds.astype(F32), axis=0)
    lower_bounds = jnp.cumsum(sm, axis=0) - sm[0]
    lru_sp = jax.nn.softplus(-lru_lambda.astype(F32))
    ssd_a_neg = -jnp.exp(ssd_a_log.astype(F32))
    lru_w_ri = jnp.concatenate([lru_w_r, lru_w_i], axis=-1).astype(BF16)
    ab_in_b = _cast_bf16(ab_w_in)
    ab_out_b = _cast_bf16(ab_w_out)
    ssd_in_b, ssd_dt_b = _cast_bf16(ssd_w_in, (ssd_main, ssd_heads))
    ssd_out_b = _cast_bf16(ssd_w_out)
    wq_b = _cast_bf16(xa_w_q)
    wkv_b = _cast_bf16(xa_w_kv)
    wo_b = _cast_bf16(xa_w_o)
    wg_b = _cast_bf16(ffn_w_gate)
    wu_b = _cast_bf16(ffn_w_up)
    wd_b = _cast_bf16(ffn_w_down)

    x2 = x.reshape(m, d)
    mem_len = mem.shape[1]
    kv = _kv_proj(mem.reshape(bsz * mem_len, d), norm_mem, wkv_b, TN).reshape(depth, bsz, mem_len, 2 * d)

    for layer in range(depth):
        if layer % 2 == 0:
            e = layer // 2
            proj = _norm_matmul(x2, norm_mix[layer], ab_in_b, e, F32, TM, TN, "ab_in_proj")
            proj = proj.reshape(bsz, seq, -1)
            ya = _lru(proj, lru_conv_w[e], lru_conv_b[e], lru_w_ri[e], lru_b_r[e], lru_b_i[e], lru_sp[e], LRU_TB)
            yb = _hgrn(proj, lower_bounds[e], hgrn_norm[e], 2 * lru_w, hgrn_w // HGRN_DK, HGRN_TB)
            x2 = _matmul2_res(ya.reshape(m, lru_w), yb.reshape(m, hgrn_w), ab_out_b, e, x2, TM, TN, "ab_out_proj")
        else:
            o = layer // 2
            proj, dt_raw = _norm_matmul_side(x2, norm_mix[layer], ssd_in_b, ssd_dt_b, o, TM, TN, "ssd_in_proj")
            y = _ssd2(proj.reshape(bsz, seq, ssd_main), dt_raw.reshape(bsz, seq, ssd_heads),
                      ssd_conv_w[o], ssd_conv_b[o], ssd_dt_bias[o], ssd_a_neg[o], ssd_d[o], ssd_norm[o],
                      ssd_inner, ssd_groups)
            x2 = _matmul_res(y.reshape(m, ssd_inner), ssd_out_b, o, x2, TM, TN // 2, "ssd_out_proj")
        q = _norm_matmul(x2, norm_xattn[layer], wq_b, layer, BF16, TM, TN, "xa_q_proj")
        x2 = _xattn(q.reshape(bsz, seq, d), kv, layer, wo_b, x2.reshape(bsz, seq, d), XA_TM)
        x2 = x2.reshape(m, d)
        g_out = norm_final if layer == depth - 1 else None
        x2 = _ffn(x2, norm_ffn[layer], wg_b, wu_b, wd_b, layer, TM, FFN_TH, g_out)
    return x2.reshape(bsz, seq, d)
```

```python
import functools
import math

import jax
import jax.numpy as jnp
from jax import lax
from jax.experimental import pallas as pl
from jax.experimental.pallas import tpu as pltpu

F32 = jnp.float32
BF16 = jnp.bfloat16
EPS = 1e-6
NEG_INF = float("-inf")

VMEM_LIMIT_BYTES = 56 * 1024 * 1024
CAST_BLOCK_BYTES = 4 * 1024 * 1024
SUBLANES = 8
LANES = 128

CONV_K = 4
LRU_BLOCK = 128
LRU_C = 8.0
HGRN_DK = 128
HGRN_CHUNK = 64
HGRN_SUB = SUBLANES
SSD_HEADDIM = 64
SSD_HPG = 8
SSD_STATE = 128
SSD_CHUNK = 128
XA_HEADS = 4
MEM_LEN = 256


def _cparams(*sem):
    return pltpu.CompilerParams(dimension_semantics=sem, vmem_limit_bytes=VMEM_LIMIT_BYTES)


def _rms_scale(x):
    return lax.rsqrt(jnp.mean(x * x, axis=-1, keepdims=True) + EPS)


def _sigmoid(x):
    return 0.5 * (1.0 + jnp.tanh(0.5 * x))


def _silu(x):
    return x * _sigmoid(x)


def _gelu_tanh(x):
    c = math.sqrt(2.0 / math.pi)
    return x * (0.5 * (1.0 + jnp.tanh(c * (x + 0.044715 * (x * x * x)))))


def _softplus(x):
    return jnp.maximum(x, 0.0) + jnp.log1p(jnp.exp(-jnp.abs(x)))


def _linear_scan_rows(a, b, h0):
    t, c = a.shape
    groups = t // SUBLANES
    a3 = a.reshape(groups, SUBLANES, c)
    b3 = b.reshape(groups, SUBLANES, c)
    row = lax.broadcasted_iota(jnp.int32, a3.shape, 1)
    s = 1
    while s < SUBLANES:
        keep = row >= s
        a_sh = pltpu.roll(a3, s, 1)
        b_sh = pltpu.roll(b3, s, 1)
        b3 = jnp.where(keep, a3 * b_sh + b3, b3)
        a3 = jnp.where(keep, a3 * a_sh, a3)
        s *= 2
    out = []
    carry = h0
    for g in range(groups):
        h = a3[g] * carry + b3[g]
        out.append(h)
        carry = h[SUBLANES - 1:SUBLANES, :]
    return jnp.concatenate(out, axis=0)


def _sqrt_nonneg(y):
    return jnp.where(y > 0.0, y * lax.rsqrt(y), 0.0)


def _cumsum(x, axis):
    n = x.shape[axis]
    idx = lax.broadcasted_iota(jnp.int32, x.shape, axis)
    s = 1
    while s < n:
        x = jnp.where(idx >= s, x + pltpu.roll(x, s, axis), x)
        s *= 2
    return x


def _causal_conv(u, buf_ref, w_ref, b_ref):
    t = u.shape[0]
    buf_ref[SUBLANES:SUBLANES + t, :] = u
    out = b_ref[...] + w_ref[3:4, :] * u
    for j in range(1, CONV_K):
        out = out + w_ref[CONV_K - 1 - j:CONV_K - j, :] * buf_ref[SUBLANES - j:SUBLANES - j + t, :]
    buf_ref[0:SUBLANES, :] = u[t - SUBLANES:t, :]
    return out


def _norm_matmul_kernel(x_ref, g_ref, w_ref, o_ref, xn_ref):
    @pl.when(pl.program_id(1) == 0)
    def _():
        x = x_ref[...]
        xn_ref[...] = (x * _rms_scale(x) * g_ref[...]).astype(BF16)

    o_ref[...] = jnp.dot(xn_ref[...], w_ref[...], preferred_element_type=F32).astype(o_ref.dtype)


def _norm_matmul(x, g, w, layer, out_dtype, tm, tn, name):
    m, k = x.shape
    n = w.shape[2]
    tm = min(tm, m)
    tn = min(tn, n)
    return pl.pallas_call(
        _norm_matmul_kernel,
        grid=(m // tm, n // tn),
        in_specs=[
            pl.BlockSpec((tm, k), lambda i, j: (i, 0)),
            pl.BlockSpec((1, k), lambda i, j: (0, 0)),
            pl.BlockSpec((None, k, tn), lambda i, j: (layer, 0, j)),
        ],
        out_specs=pl.BlockSpec((tm, tn), lambda i, j: (i, j)),
        out_shape=jax.ShapeDtypeStruct((m, n), out_dtype),
        scratch_shapes=[pltpu.VMEM((tm, k), BF16)],
        compiler_params=_cparams("parallel", "arbitrary"),
        name=name,
    )(x, g.reshape(1, k), w)


def _norm_matmul_side_kernel(x_ref, g_ref, w_ref, ws_ref, o_ref, os_ref, xn_ref):
    @pl.when(pl.program_id(1) == 0)
    def _():
        x = x_ref[...]
        xn = (x * _rms_scale(x) * g_ref[...]).astype(BF16)
        xn_ref[...] = xn
        os_ref[...] = jnp.dot(xn, ws_ref[...], preferred_element_type=F32)

    o_ref[...] = jnp.dot(xn_ref[...], w_ref[...], preferred_element_type=F32).astype(o_ref.dtype)


def _norm_matmul_side(x, g, w, w_side, layer, tm, tn, name):
    m, k = x.shape
    n = w.shape[2]
    ns = w_side.shape[2]
    tm = min(tm, m)
    tn = min(tn, n)
    return pl.pallas_call(
        _norm_matmul_side_kernel,
        grid=(m // tm, n // tn),
        in_specs=[
            pl.BlockSpec((tm, k), lambda i, j: (i, 0)),
            pl.BlockSpec((1, k), lambda i, j: (0, 0)),
            pl.BlockSpec((None, k, tn), lambda i, j: (layer, 0, j)),
            pl.BlockSpec((None, k, ns), lambda i, j: (layer, 0, 0)),
        ],
        out_specs=[pl.BlockSpec((tm, tn), lambda i, j: (i, j)), pl.BlockSpec((tm, ns), lambda i, j: (i, 0))],
        out_shape=[jax.ShapeDtypeStruct((m, n), F32), jax.ShapeDtypeStruct((m, ns), F32)],
        scratch_shapes=[pltpu.VMEM((tm, k), BF16)],
        compiler_params=_cparams("parallel", "arbitrary"),
        name=name,
    )(x, g.reshape(1, k), w, w_side)


def _kv_proj_kernel(x_ref, g_ref, w_ref, o_ref, xn_ref):
    @pl.when((pl.program_id(0) == 0) & (pl.program_id(1) == 0))
    def _():
        x = x_ref[...]
        xn_ref[...] = (x * _rms_scale(x) * g_ref[...]).astype(BF16)

    o_ref[...] = jnp.dot(xn_ref[...], w_ref[...], preferred_element_type=F32).astype(o_ref.dtype)


def _kv_proj(x, g, w, tn):
    m, k = x.shape
    nl, _, n = w.shape
    return pl.pallas_call(
        _kv_proj_kernel,
        grid=(nl, n // tn),
        in_specs=[
            pl.BlockSpec((m, k), lambda l, j: (0, 0)),
            pl.BlockSpec((1, k), lambda l, j: (0, 0)),
            pl.BlockSpec((None, k, tn), lambda l, j: (l, 0, j)),
        ],
        out_specs=pl.BlockSpec((None, m, tn), lambda l, j: (l, 0, j)),
        out_shape=jax.ShapeDtypeStruct((nl, m, n), BF16),
        scratch_shapes=[pltpu.VMEM((m, k), BF16)],
        compiler_params=_cparams("arbitrary", "arbitrary"),
        name="kv_proj",
    )(x, g.reshape(1, k), w)


def _cast_kernel(x_ref, *o_refs):
    off = 0
    for o_ref in o_refs:
        width = o_ref.shape[-1]
        o_ref[...] = x_ref[:, off:off + width].astype(BF16)
        off += width


def _cast_bf16(w, splits=None):
    nl, k, n = w.shape
    splits = splits or (n,)
    assert sum(splits) == n
    tk = SUBLANES
    while tk * 2 <= k and k % (tk * 2) == 0 and tk * 2 * n * 4 <= CAST_BLOCK_BYTES:
        tk *= 2
    outs = pl.pallas_call(
        _cast_kernel,
        grid=(nl, k // tk),
        in_specs=[pl.BlockSpec((None, tk, n), lambda l, i: (l, i, 0))],
        out_specs=[pl.BlockSpec((None, tk, s), lambda l, i: (l, i, 0)) for s in splits],
        out_shape=[jax.ShapeDtypeStruct((nl, k, s), BF16) for s in splits],
        compiler_params=_cparams("parallel", "parallel"),
        name="cast_bf16",
    )(w)
    return outs if len(splits) > 1 else outs[0]


def _matmul_res_kernel(a_ref, w_ref, r_ref, o_ref):
    o_ref[...] = r_ref[...] + jnp.dot(a_ref[...], w_ref[...], preferred_element_type=F32)


def _matmul_res(a, w, layer, res, tm, tn, name):
    m, k = a.shape
    n = w.shape[2]
    tm = min(tm, m)
    tn = min(tn, n)
    return pl.pallas_call(
        _matmul_res_kernel,
        grid=(m // tm, n // tn),
        in_specs=[
            pl.BlockSpec((tm, k), lambda i, j: (i, 0)),
            pl.BlockSpec((None, k, tn), lambda i, j: (layer, 0, j)),
            pl.BlockSpec((tm, tn), lambda i, j: (i, j)),
        ],
        out_specs=pl.BlockSpec((tm, tn), lambda i, j: (i, j)),
        out_shape=jax.ShapeDtypeStruct((m, n), F32),
        compiler_params=_cparams("parallel", "arbitrary"),
        name=name,
    )(a, w, res)


def _matmul2_res_kernel(a1_ref, a2_ref, w1_ref, w2_ref, r_ref, o_ref):
    acc = jnp.dot(a1_ref[...], w1_ref[...], preferred_element_type=F32)
    acc = acc + jnp.dot(a2_ref[...], w2_ref[...], preferred_element_type=F32)
    o_ref[...] = r_ref[...] + acc


def _matmul2_res(a1, a2, w, layer, res, tm, tn, name):
    m, k1 = a1.shape
    k2 = a2.shape[1]
    assert k1 == k2
    n = w.shape[2]
    tm = min(tm, m)
    tn = min(tn, n)
    return pl.pallas_call(
        _matmul2_res_kernel,
        grid=(m // tm, n // tn),
        in_specs=[
            pl.BlockSpec((tm, k1), lambda i, j: (i, 0)),
            pl.BlockSpec((tm, k2), lambda i, j: (i, 0)),
            pl.BlockSpec((None, k1, tn), lambda i, j: (layer, 0, j)),
            pl.BlockSpec((None, k2, tn), lambda i, j: (layer, 1, j)),
            pl.BlockSpec((tm, tn), lambda i, j: (i, j)),
        ],
        out_specs=pl.BlockSpec((tm, tn), lambda i, j: (i, j)),
        out_shape=jax.ShapeDtypeStruct((m, n), F32),
        compiler_params=_cparams("parallel", "arbitrary"),
        name=name,
    )(a1, a2, w, w, res)


def _ffn_kernel(x_ref, g_ref, go_ref, wg_ref, wu_ref, wd_ref, o_ref, xn_ref, *, out_norm):
    @pl.when(pl.program_id(1) == 0)
    def _():
        x = x_ref[...]
        xn_ref[...] = (x * _rms_scale(x) * g_ref[...]).astype(BF16)
        o_ref[...] = x

    xn = xn_ref[...]
    gate = jnp.dot(xn, wg_ref[...], preferred_element_type=F32)
    up = jnp.dot(xn, wu_ref[...], preferred_element_type=F32)
    hid = (_silu(gate) * up).astype(BF16)
    o_ref[...] += jnp.dot(hid, wd_ref[...], preferred_element_type=F32)

    if out_norm:
        @pl.when(pl.program_id(1) == pl.num_programs(1) - 1)
        def _():
            y = o_ref[...]
            o_ref[...] = y * _rms_scale(y) * go_ref[...]


def _ffn(x, g, wg, wu, wd, layer, tm, th, g_out=None):
    m, d = x.shape
    hdim = wg.shape[2]
    tm = min(tm, m)
    out_norm = g_out is not None
    g_out = g if g_out is None else g_out
    return pl.pallas_call(
        functools.partial(_ffn_kernel, out_norm=out_norm),
        grid=(m // tm, hdim // th),
        in_specs=[
            pl.BlockSpec((tm, d), lambda i, j: (i, 0)),
            pl.BlockSpec((1, d), lambda i, j: (0, 0)),
            pl.BlockSpec((1, d), lambda i, j: (0, 0)),
            pl.BlockSpec((None, d, th), lambda i, j: (layer, 0, j)),
            pl.BlockSpec((None, d, th), lambda i, j: (layer, 0, j)),
            pl.BlockSpec((None, th, d), lambda i, j: (layer, j, 0)),
        ],
        out_specs=pl.BlockSpec((tm, d), lambda i, j: (i, 0)),
        out_shape=jax.ShapeDtypeStruct((m, d), F32),
        scratch_shapes=[pltpu.VMEM((tm, d), BF16)],
        compiler_params=_cparams("parallel", "arbitrary"),
        name="ffn",
    )(x, g.reshape(1, d), g_out.reshape(1, d), wg, wu, wd)


def _xattn_kernel(q_ref, k_ref, v_ref, wo_ref, x_ref, o_ref, ob_ref, *, heads, scale):
    hd = q_ref.shape[2] // heads
    for h in range(heads):
        sl = slice(h * hd, (h + 1) * hd)
        s = lax.dot_general(q_ref[0, :, sl], k_ref[0, :, sl], (((1,), (1,)), ((), ())),
                            preferred_element_type=F32) * scale
        p = jnp.exp(s - jnp.max(s, axis=-1, keepdims=True))
        p = p / jnp.sum(p, axis=-1, keepdims=True)
        ob_ref[:, sl] = jnp.dot(p.astype(BF16), v_ref[0, :, sl], preferred_element_type=F32).astype(BF16)
    o_ref[0] = x_ref[0] + jnp.dot(ob_ref[...], wo_ref[...], preferred_element_type=F32)


def _xattn(q, kv, layer, wo, x, tm):
    b, s, d = x.shape
    mem = kv.shape[2]
    tm = min(tm, s)
    kern = functools.partial(_xattn_kernel, heads=XA_HEADS, scale=(d // XA_HEADS) ** -0.5)
    return pl.pallas_call(
        kern,
        grid=(b, s // tm),
        in_specs=[
            pl.BlockSpec((1, tm, d), lambda i, j: (i, j, 0)),
            pl.BlockSpec((None, 1, mem, d), lambda i, j: (layer, i, 0, 0)),
            pl.BlockSpec((None, 1, mem, d), lambda i, j: (layer, i, 0, 1)),
            pl.BlockSpec((None, d, d), lambda i, j: (layer, 0, 0)),
            pl.BlockSpec((1, tm, d), lambda i, j: (i, j, 0)),
        ],
        out_specs=pl.BlockSpec((1, tm, d), lambda i, j: (i, j, 0)),
        out_shape=jax.ShapeDtypeStruct((b, s, d), F32),
        scratch_shapes=[pltpu.VMEM((tm, d), BF16)],
        compiler_params=_cparams("parallel", "arbitrary"),
        name="xattn",
    )(q, kv, kv, wo, x)


def _lru_kernel(xa_ref, ga_ref, cw_ref, cb_ref, wri_ref, br_ref, bi_ref, sp_ref, o_ref, buf_ref, h_ref):
    @pl.when(pl.program_id(1) == 0)
    def _():
        buf_ref[0:SUBLANES, :] = jnp.zeros((SUBLANES, buf_ref.shape[1]), F32)
        h_ref[...] = jnp.zeros(h_ref.shape, F32)

    xc = _causal_conv(xa_ref[0], buf_ref, cw_ref, cb_ref)
    t = xc.shape[0]
    for blk in range(xc.shape[1] // LRU_BLOCK):
        sl = slice(blk * LRU_BLOCK, (blk + 1) * LRU_BLOCK)
        xb = xc[:, sl]
        pre = jnp.dot(xb.astype(BF16), wri_ref[blk], preferred_element_type=F32)
        r_gate = _sigmoid(pre[:, :LRU_BLOCK] + br_ref[:, sl])
        i_gate = _sigmoid(pre[:, LRU_BLOCK:] + bi_ref[:, sl])
        log_a = (-LRU_C) * r_gate * sp_ref[:, sl]
        a = jnp.exp(log_a)
        mult = _sqrt_nonneg(-jnp.tanh(log_a) * (a * a + 1.0))
        h = _linear_scan_rows(a, mult * i_gate * xb, h_ref[0:1, sl])
        h_ref[0:1, sl] = h[t - 1:t, :]
        o_ref[0, :, sl] = (_gelu_tanh(ga_ref[0, :, sl]) * h).astype(BF16)


def _lru(proj, conv_w, conv_b, w_ri, b_r, b_i, sp, tb):
    b, s, _ = proj.shape
    w = conv_w.shape[1]
    tb = min(tb, s)
    vec = lambda: pl.BlockSpec((1, w), lambda i, j: (0, 0))
    return pl.pallas_call(
        _lru_kernel,
        grid=(b, s // tb),
        in_specs=[
            pl.BlockSpec((1, tb, w), lambda i, j: (i, j, 0)),
            pl.BlockSpec((1, tb, w), lambda i, j: (i, j, 1)),
            pl.BlockSpec((CONV_K, w), lambda i, j: (0, 0)),
            vec(),
            pl.BlockSpec(w_ri.shape, lambda i, j: (0, 0, 0)),
            vec(), vec(), vec(),
        ],
        out_specs=pl.BlockSpec((1, tb, w), lambda i, j: (i, j, 0)),
        out_shape=jax.ShapeDtypeStruct((b, s, w), BF16),
        scratch_shapes=[pltpu.VMEM((tb + SUBLANES, w), F32), pltpu.VMEM((SUBLANES, w), F32)],
        compiler_params=_cparams("parallel", "arbitrary"),
        name="lru",
    )(proj, proj, conv_w, conv_b.reshape(1, w), w_ri, b_r.reshape(1, w), b_i.reshape(1, w), sp.reshape(1, w))


def _hgrn_chunk(q, f, v, lb, st):
    c, dk = q.shape
    nsub = c // HGRN_SUB
    qh = _silu(q)
    fg = lb + (1.0 - lb) * _sigmoid(f)
    kh = 1.0 - fg
    g = jnp.log(fg)
    cum = _cumsum(g, 0)
    ex = cum - g
    cum3 = cum.reshape(nsub, HGRN_SUB, dk)
    ex3 = ex.reshape(nsub, HGRN_SUB, dk)
    base3 = jnp.broadcast_to(ex3[:, 0:1, :], cum3.shape)
    q3 = qh.reshape(nsub, HGRN_SUB, dk)
    k3 = kh.reshape(nsub, HGRN_SUB, dk)
    v3 = v.reshape(nsub, HGRN_SUB, dk)
    vb = v.astype(BF16)

    row3 = lax.broadcasted_iota(jnp.int32, cum3.shape, 1)
    terms = []
    for s in range(HGRN_SUB):
        diff = cum3 - cum3[:, s:s + 1, :]
        dec = jnp.exp(jnp.where(row3 >= s, diff, NEG_INF))
        terms.append((dec * q3 * k3[:, s:s + 1, :]).reshape(c, dk))
    stacked = jnp.concatenate(terms, axis=0).astype(BF16)
    ones = jnp.ones((dk, dk), BF16)
    summed = jnp.dot(stacked, ones, preferred_element_type=F32)
    o3 = jnp.zeros(cum3.shape, F32)
    for s in range(HGRN_SUB):
        o3 = o3 + summed[s * c:(s + 1) * c, :].reshape(nsub, HGRN_SUB, dk) * v3[:, s:s + 1, :]
    o = o3.reshape(c, dk)

    q_loc = (qh * jnp.exp(cum - base3.reshape(c, dk))).astype(BF16)
    pad = jnp.zeros((LANES, dk), F32)
    k_parts = []
    for i in range(1, nsub):
        n = i * HGRN_SUB
        base_i = ex[n:n + 1, :]
        k_parts.append(kh[0:n, :] * jnp.exp(base_i - cum[0:n, :]))
        k_parts.append(pad[0:LANES - n, :])
    k_hat = jnp.concatenate(k_parts, axis=0).astype(BF16)
    a_all = lax.dot_general(q_loc, k_hat, (((1,), (1,)), ((), ())), preferred_element_type=F32)
    a_rows = [jnp.zeros((HGRN_SUB, c), F32)]
    for i in range(1, nsub):
        a_rows.append(a_all[i * HGRN_SUB:(i + 1) * HGRN_SUB, (i - 1) * LANES:(i - 1) * LANES + c])
    a_off = jnp.concatenate(a_rows, axis=0).astype(BF16)
    o = o + jnp.dot(a_off, vb, preferred_element_type=F32)

    q_in = (qh * jnp.exp(cum)).astype(BF16)
    o = o + lax.dot_general(q_in, st.astype(BF16), (((1,), (1,)), ((), ())), preferred_element_type=F32)
    last = cum[c - 1:c, :]
    k_out = (kh * jnp.exp(last - cum)).astype(BF16)
    st_new = st * jnp.exp(last) + lax.dot_general(vb, k_out, (((0,), (0,)), ((), ())),
                                                  preferred_element_type=F32)
    return o, st_new


def _hgrn_kernel(q_ref, f_ref, v_ref, gb_ref, lb_ref, gn_ref, o_ref, st_ref):
    @pl.when(pl.program_id(2) == 0)
    def _():
        st_ref[...] = jnp.zeros(st_ref.shape, F32)

    lb = lb_ref[...]
    st = st_ref[...]
    for ci in range(q_ref.shape[1] // HGRN_CHUNK):
        rows = slice(ci * HGRN_CHUNK, (ci + 1) * HGRN_CHUNK)
        o, st = _hgrn_chunk(q_ref[0, rows, :], f_ref[0, rows, :], v_ref[0, rows, :], lb, st)
        o = o * _rms_scale(o) * gn_ref[...]
        o_ref[0, rows, :] = (o * _silu(gb_ref[0, rows, :])).astype(BF16)
    st_ref[...] = st


def _hgrn(proj, lower_bound, head_norm, col0, heads, tb):
    b, s, _ = proj.shape
    tb = min(tb, s)
    c0 = col0 // HGRN_DK
    part = lambda p: pl.BlockSpec((1, tb, HGRN_DK), lambda i, h, j: (i, j, c0 + p * heads + h))
    vec = lambda: pl.BlockSpec((1, HGRN_DK), lambda i, h, j: (0, h))
    w = heads * HGRN_DK
    return pl.pallas_call(
        _hgrn_kernel,
        grid=(b, heads, s // tb),
        in_specs=[part(0), part(1), part(2), part(3), vec(), vec()],
        out_specs=pl.BlockSpec((1, tb, HGRN_DK), lambda i, h, j: (i, j, h)),
        out_shape=jax.ShapeDtypeStruct((b, s, w), BF16),
        scratch_shapes=[pltpu.VMEM((HGRN_DK, HGRN_DK), F32)],
        compiler_params=_cparams("parallel", "parallel", "arbitrary"),
        name="hgrn",
    )(proj, proj, proj, proj, lower_bound.reshape(1, w), head_norm.reshape(1, w))


def _split3(x):
    hi = x.astype(BF16).astype(F32)
    rem = x - hi
    mid = rem.astype(BF16).astype(F32)
    return hi, mid, rem - mid


def _ssd2_kernel(z_ref, x_ref, b_ref, c_ref, dtr_ref,
                 cwx_ref, cbx_ref, cwb_ref, cbb_ref, cwc_ref, cbc_ref,
                 dbr_ref, anr_ref, dsk_ref, nw_ref, tri_ref, sel_ref,
                 o_ref, st_ref, y_ref, bufx_ref, bufb_ref, bufc_ref):
    @pl.when(pl.program_id(2) == 0)
    def _():
        st_ref[...] = jnp.zeros(st_ref.shape, F32)
        bufx_ref[0:SUBLANES, :] = jnp.zeros((SUBLANES, bufx_ref.shape[1]), F32)
        bufb_ref[0:SUBLANES, :] = jnp.zeros((SUBLANES, bufb_ref.shape[1]), F32)
        bufc_ref[0:SUBLANES, :] = jnp.zeros((SUBLANES, bufc_ref.shape[1]), F32)

    tb = x_ref.shape[1]
    t = tri_ref.shape[0]
    n = st_ref.shape[0]
    hpg = dbr_ref.shape[0]
    xs_all = _silu(_causal_conv(x_ref[0], bufx_ref, cwx_ref, cbx_ref))
    bm_all = _silu(_causal_conv(b_ref[0], bufb_ref, cwb_ref, cbb_ref))
    cm_all = _silu(_causal_conv(c_ref[0], bufc_ref, cwc_ref, cbc_ref))
    dtr_all = _softplus(dtr_ref[0, 0] + dbr_ref[...])
    causal = (lax.broadcasted_iota(jnp.int32, (t, t), 0) >= lax.broadcasted_iota(jnp.int32, (t, t), 1))
    low_t = lax.broadcasted_iota(jnp.int32, (t, LANES), 1) < SSD_HEADDIM
    low_n = lax.broadcasted_iota(jnp.int32, (n, LANES), 1) < SSD_HEADDIM
    pad = jnp.zeros((sel_ref.shape[0] - 3 * hpg, t), F32)

    for ci in range(tb // t):
        rows = slice(ci * t, (ci + 1) * t)
        xs, bm, cm, dtr = xs_all[rows], bm_all[rows], cm_all[rows], dtr_all[:, rows]
        xsb = xs.astype(BF16)
        dta = dtr * anr_ref[...]
        parts = jnp.concatenate(list(_split3(dta)) + [jnp.zeros_like(dta)], axis=0).astype(BF16)
        c3 = jnp.dot(parts, tri_ref[...], preferred_element_type=F32)
        cum = c3[0:hpg] + c3[hpg:2 * hpg] + c3[2 * hpg:3 * hpg]
        cols = jnp.concatenate(list(_split3(cum)) + [pad], axis=0).astype(BF16)
        bc_all = lax.dot_general(cols, sel_ref[...], (((0,), (0,)), ((), ())), preferred_element_type=F32)
        row_term = cum - jnp.log(dtr)
        last = jnp.broadcast_to(cum[:, t - 1:t], cum.shape)
        row_scale = jnp.exp(last - cum) * dtr
        e_last = jnp.exp(bc_all[t - 1:t, :])
        cb = lax.dot_general(cm.astype(BF16), bm.astype(BF16), (((1,), (1,)), ((), ())),
                             preferred_element_type=F32)
        bt = bm.T

        for pair in range(hpg // 2):
            pc = slice(pair * LANES, (pair + 1) * LANES)
            rhs = jnp.concatenate([xsb[:, pc], st_ref[:, pc].astype(BF16)], axis=0)
            lhs, zl = [], []
            for h in (2 * pair, 2 * pair + 1):
                bc = bc_all[:, h * LANES:(h + 1) * LANES]
                dec = jnp.exp(jnp.where(causal, bc - row_term[h:h + 1, :], NEG_INF))
                lhs.append(jnp.concatenate([cb * dec, cm * jnp.exp(bc)], axis=1).astype(BF16))
                zl.append((bt * row_scale[h:h + 1, :]).astype(BF16))
            y2 = jnp.dot(jnp.concatenate(lhs, axis=0), rhs, preferred_element_type=F32)
            y_ref[rows, pc] = jnp.where(low_t, y2[:t], y2[t:]) + dsk_ref[:, pc] * xs[:, pc]
            z2 = jnp.dot(jnp.concatenate(zl, axis=0), xsb[:, pc], preferred_element_type=F32)
            e_pair = jnp.where(low_n[0:1], e_last[:, 2 * pair * LANES:(2 * pair + 1) * LANES],
                               e_last[:, (2 * pair + 1) * LANES:(2 * pair + 2) * LANES])
            st_ref[:, pc] = st_ref[:, pc] * e_pair + jnp.where(low_n, z2[:n], z2[n:])

    y = y_ref[...] * _silu(z_ref[0])
    o_ref[0] = (y * _rms_scale(y) * nw_ref[...]).astype(BF16)


def _ssd2(proj, dt_raw, conv_w, conv_b, dt_bias, a_neg, d_skip, norm_w, inner, groups):
    b, s, _ = proj.shape
    t = min(SSD_CHUNK, s)
    tb = min(SSD_TB, s)
    gw = inner // groups
    n = SSD_STATE
    hpg = SSD_HPG
    assert gw == hpg * SSD_HEADDIM and 2 * SSD_HEADDIM == LANES and tb % t == 0
    dt_row = dt_raw.reshape(b, s, groups, hpg).transpose(0, 2, 3, 1)
    db_r = dt_bias.reshape(groups, hpg, 1)
    an_r = a_neg.reshape(groups, hpg, 1)
    dsk = jnp.repeat(d_skip, SSD_HEADDIM).reshape(1, inner)
    tri = (jnp.arange(t)[:, None] <= jnp.arange(t)[None, :]).astype(BF16)
    krow = jnp.arange(LANES)[:, None]
    sel = ((krow < 3 * hpg) & ((krow % hpg) == (jnp.arange(hpg * LANES)[None, :] // LANES))).astype(BF16)
    cbias = conv_b.reshape(1, -1)
    xoff, boff, coff = inner // gw, (2 * inner) // n, (2 * inner + groups * n) // n
    cboff, ccoff = inner // n, (inner + groups * n) // n
    col = lambda: pl.BlockSpec((None, hpg, 1), lambda i, g, j: (g, 0, 0))
    const = lambda a: pl.BlockSpec(a.shape, lambda i, g, j: (0, 0))
    return pl.pallas_call(
        _ssd2_kernel,
        grid=(b, groups, s // tb),
        in_specs=[
            pl.BlockSpec((1, tb, gw), lambda i, g, j: (i, j, g)),
            pl.BlockSpec((1, tb, gw), lambda i, g, j: (i, j, xoff + g)),
            pl.BlockSpec((1, tb, n), lambda i, g, j: (i, j, boff + g)),
            pl.BlockSpec((1, tb, n), lambda i, g, j: (i, j, coff + g)),
            pl.BlockSpec((1, 1, hpg, tb), lambda i, g, j: (i, g, 0, j)),
            pl.BlockSpec((CONV_K, gw), lambda i, g, j: (0, g)),
            pl.BlockSpec((1, gw), lambda i, g, j: (0, g)),
            pl.BlockSpec((CONV_K, n), lambda i, g, j: (0, cboff + g)),
            pl.BlockSpec((1, n), lambda i, g, j: (0, cboff + g)),
            pl.BlockSpec((CONV_K, n), lambda i, g, j: (0, ccoff + g)),
            pl.BlockSpec((1, n), lambda i, g, j: (0, ccoff + g)),
            col(), col(),
            pl.BlockSpec((1, gw), lambda i, g, j: (0, g)),
            pl.BlockSpec((1, gw), lambda i, g, j: (0, g)),
            const(tri), const(sel),
        ],
        out_specs=pl.BlockSpec((1, tb, gw), lambda i, g, j: (i, j, g)),
        out_shape=jax.ShapeDtypeStruct((b, s, inner), BF16),
        scratch_shapes=[
            pltpu.VMEM((n, gw), F32),
            pltpu.VMEM((tb, gw), F32),
            pltpu.VMEM((tb + SUBLANES, gw), F32),
            pltpu.VMEM((tb + SUBLANES, n), F32),
            pltpu.VMEM((tb + SUBLANES, n), F32),
        ],
        compiler_params=_cparams("parallel", "parallel", "arbitrary"),
        name="ssd",
    )(proj, proj, proj, proj, dt_row, conv_w, cbias, conv_w, cbias, conv_w, cbias,
      db_r, an_r, dsk, norm_w.reshape(1, inner), tri, sel)


TM = 1024
TN = 1024
FFN_TH = 512
XA_TM = 512
LRU_TB = 256
HGRN_TB = 512
SSD_TB = 512


def kernel(x, mem, norm_mix, norm_xattn, norm_ffn, norm_mem, norm_final, ab_w_in, ab_w_out, lru_conv_w, lru_conv_b, lru_w_r, lru_b_r, lru_w_i, lru_b_i, lru_lambda, hgrn_lower_bounds, hgrn_norm, ssd_w_in, ssd_w_out, ssd_conv_w, ssd_conv_b, ssd_dt_bias, ssd_a_log, ssd_d, ssd_norm, xa_w_q, xa_w_kv, xa_w_o, ffn_w_gate, ffn_w_up, ffn_w_down):
    bsz, seq, d = x.shape
    depth = norm_mix.shape[0]
    m = bsz * seq
    lru_w = lru_conv_w.shape[2]
    hgrn_w = hgrn_norm.shape[1]
    ssd_heads = ssd_a_log.shape[1]
    ssd_inner = ssd_norm.shape[1]
    ssd_groups = ssd_heads // SSD_HPG
    ssd_main = ssd_w_in.shape[2] - ssd_heads

    sm = jax.nn.softmax(hgrn_lower_bounds.astype(F32), axis=0)
    lower_bounds = jnp.cumsum(sm, axis=0) - sm[0]
    lru_sp = jax.nn.softplus(-lru_lambda.astype(F32))
    ssd_a_neg = -jnp.exp(ssd_a_log.astype(F32))
    lru_w_ri = jnp.concatenate([lru_w_r, lru_w_i], axis=-1).astype(BF16)
    ab_in_b = _cast_bf16(ab_w_in)
    ab_out_b = _cast_bf16(ab_w_out)
    ssd_in_b, ssd_dt_b = _cast_bf16(ssd_w_in, (ssd_main, ssd_heads))
    ssd_out_b = _cast_bf16(ssd_w_out)
    wq_b = _cast_bf16(xa_w_q)
    wkv_b = _cast_bf16(xa_w_kv)
    wo_b = _cast_bf16(xa_w_o)
    wg_b = _cast_bf16(ffn_w_gate)
    wu_b = _cast_bf16(ffn_w_up)
    wd_b = _cast_bf16(ffn_w_down)

    x2 = x.reshape(m, d)
    mem_len = mem.shape[1]
    kv = _kv_proj(mem.reshape(bsz * mem_len, d), norm_mem, wkv_b, TN).reshape(depth, bsz, mem_len, 2 * d)

    for layer in range(depth):
        if layer % 2 == 0:
            e = layer // 2
            proj = _norm_matmul(x2, norm_mix[layer], ab_in_b, e, F32, TM, TN, "ab_in_proj")
            proj = proj.reshape(bsz, seq, -1)
            ya = _lru(proj, lru_conv_w[e], lru_conv_b[e], lru_w_ri[e], lru_b_r[e], lru_b_i[e], lru_sp[e], LRU_TB)
            yb = _hgrn(proj, lower_bounds[e], hgrn_norm[e], 2 * lru_w, hgrn_w // HGRN_DK, HGRN_TB)
            x2 = _matmul2_res(ya.reshape(m, lru_w), yb.reshape(m, hgrn_w), ab_out_b, e, x2, TM, TN, "ab_out_proj")
        else:
            o = layer // 2
            proj, dt_raw = _norm_matmul_side(x2, norm_mix[layer], ssd_in_b, ssd_dt_b, o, TM, TN, "ssd_in_proj")
            y = _ssd2(proj.reshape(bsz, seq, ssd_main), dt_raw.reshape(bsz, seq, ssd_heads),
                      ssd_conv_w[o], ssd_conv_b[o], ssd_dt_bias[o], ssd_a_neg[o], ssd_d[o], ssd_norm[o],
                      ssd_inner, ssd_groups)
            x2 = _matmul_res(y.reshape(m, ssd_inner), ssd_out_b, o, x2, TM, TN // 2, "ssd_out_proj")
        q = _norm_matmul(x2, norm_xattn[layer], wq_b, layer, BF16, TM, TN, "xa_q_proj")
        x2 = _xattn(q.reshape(bsz, seq, d), kv, layer, wo_b, x2.reshape(bsz, seq, d), XA_TM)
        x2 = x2.reshape(m, d)
        g_out = norm_final if layer == depth - 1 else None
        x2 = _ffn(x2, norm_ffn[layer], wg_b, wu_b, wd_b, layer, TM, FFN_TH, g_out)
    return x2.reshape(bsz, seq, d)
```

```python
import functools
import math

import jax
import jax.numpy as jnp
from jax import lax
from jax.experimental import pallas as pl
from jax.experimental.pallas import tpu as pltpu

F32 = jnp.float32
BF16 = jnp.bfloat16
EPS = 1e-6
NEG_INF = float("-inf")

VMEM_LIMIT_BYTES = 56 * 1024 * 1024
CAST_BLOCK_BYTES = 4 * 1024 * 1024
SUBLANES = 8
LANES = 128

CONV_K = 4
LRU_BLOCK = 128
LRU_C = 8.0
HGRN_DK = 128
HGRN_CHUNK = 64
HGRN_SUB = SUBLANES
SSD_HEADDIM = 64
SSD_HPG = 8
SSD_STATE = 128
SSD_CHUNK = 128
XA_HEADS = 4
MEM_LEN = 256


def _cparams(*sem):
    return pltpu.CompilerParams(dimension_semantics=sem, vmem_limit_bytes=VMEM_LIMIT_BYTES)


def _rms_scale(x):
    return lax.rsqrt(jnp.mean(x * x, axis=-1, keepdims=True) + EPS)


def _sigmoid(x):
    return 0.5 * (1.0 + jnp.tanh(0.5 * x))


def _silu(x):
    return x * _sigmoid(x)


def _gelu_tanh(x):
    c = math.sqrt(2.0 / math.pi)
    return x * (0.5 * (1.0 + jnp.tanh(c * (x + 0.044715 * (x * x * x)))))


def _softplus(x):
    return jnp.maximum(x, 0.0) + jnp.log1p(jnp.exp(-jnp.abs(x)))


def _linear_scan_rows(a, b, h0):
    t, c = a.shape
    groups = t // SUBLANES
    a3 = a.reshape(groups, SUBLANES, c)
    b3 = b.reshape(groups, SUBLANES, c)
    row = lax.broadcasted_iota(jnp.int32, a3.shape, 1)
    s = 1
    while s < SUBLANES:
        keep = row >= s
        a_sh = pltpu.roll(a3, s, 1)
        b_sh = pltpu.roll(b3, s, 1)
        b3 = jnp.where(keep, a3 * b_sh + b3, b3)
        a3 = jnp.where(keep, a3 * a_sh, a3)
        s *= 2
    out = []
    carry = h0
    for g in range(groups):
        h = a3[g] * carry + b3[g]
        out.append(h)
        carry = h[SUBLANES - 1:SUBLANES, :]
    return jnp.concatenate(out, axis=0)


def _sqrt_nonneg(y):
    return jnp.where(y > 0.0, y * lax.rsqrt(y), 0.0)


def _cumsum(x, axis):
    n = x.shape[axis]
    idx = lax.broadcasted_iota(jnp.int32, x.shape, axis)
    s = 1
    while s < n:
        x = jnp.where(idx >= s, x + pltpu.roll(x, s, axis), x)
        s *= 2
    return x


def _causal_conv(u, buf_ref, w_ref, b_ref):
    t = u.shape[0]
    buf_ref[SUBLANES:SUBLANES + t, :] = u
    out = b_ref[...] + w_ref[3:4, :] * u
    for j in range(1, CONV_K):
        out = out + w_ref[CONV_K - 1 - j:CONV_K - j, :] * buf_ref[SUBLANES - j:SUBLANES - j + t, :]
    buf_ref[0:SUBLANES, :] = u[t - SUBLANES:t, :]
    return out


def _norm_matmul_kernel(x_ref, g_ref, w_ref, o_ref, xn_ref):
    @pl.when(pl.program_id(1) == 0)
    def _():
        x = x_ref[...]
        xn_ref[...] = (x * _rms_scale(x) * g_ref[...]).astype(BF16)

    o_ref[...] = jnp.dot(xn_ref[...], w_ref[...], preferred_element_type=F32).astype(o_ref.dtype)


def _norm_matmul(x, g, w, layer, out_dtype, tm, tn, name):
    m, k = x.shape
    n = w.shape[2]
    tm = min(tm, m)
    tn = min(tn, n)
    return pl.pallas_call(
        _norm_matmul_kernel,
        grid=(m // tm, n // tn),
        in_specs=[
            pl.BlockSpec((tm, k), lambda i, j: (i, 0)),
            pl.BlockSpec((1, k), lambda i, j: (0, 0)),
            pl.BlockSpec((None, k, tn), lambda i, j: (layer, 0, j)),
        ],
        out_specs=pl.BlockSpec((tm, tn), lambda i, j: (i, j)),
        out_shape=jax.ShapeDtypeStruct((m, n), out_dtype),
        scratch_shapes=[pltpu.VMEM((tm, k), BF16)],
        compiler_params=_cparams("parallel", "arbitrary"),
        name=name,
    )(x, g.reshape(1, k), w)


def _norm_matmul_side_kernel(x_ref, g_ref, w_ref, ws_ref, o_ref, os_ref, xn_ref):
    @pl.when(pl.program_id(1) == 0)
    def _():
        x = x_ref[...]
        xn = (x * _rms_scale(x) * g_ref[...]).astype(BF16)
        xn_ref[...] = xn
        os_ref[...] = jnp.dot(xn, ws_ref[...], preferred_element_type=F32)

    o_ref[...] = jnp.dot(xn_ref[...], w_ref[...], preferred_element_type=F32).astype(o_ref.dtype)


def _norm_matmul_side(x, g, w, w_side, layer, tm, tn, name):
    m, k = x.shape
    n = w.shape[2]
    ns = w_side.shape[2]
    tm = min(tm, m)
    tn = min(tn, n)
    return pl.pallas_call(
        _norm_matmul_side_kernel,
        grid=(m // tm, n // tn),
        in_specs=[
            pl.BlockSpec((tm, k), lambda i, j: (i, 0)),
            pl.BlockSpec((1, k), lambda i, j: (0, 0)),
            pl.BlockSpec((None, k, tn), lambda i, j: (layer, 0, j)),
            pl.BlockSpec((None, k, ns), lambda i, j: (layer, 0, 0)),
        ],
        out_specs=[pl.BlockSpec((tm, tn), lambda i, j: (i, j)), pl.BlockSpec((tm, ns), lambda i, j: (i, 0))],
        out_shape=[jax.ShapeDtypeStruct((m, n), F32), jax.ShapeDtypeStruct((m, ns), F32)],
        scratch_shapes=[pltpu.VMEM((tm, k), BF16)],
        compiler_params=_cparams("parallel", "arbitrary"),
        name=name,
    )(x, g.reshape(1, k), w, w_side)


def _kv_proj_kernel(x_ref, g_ref, w_ref, o_ref, xn_ref):
    @pl.when((pl.program_id(0) == 0) & (pl.program_id(1) == 0))
    def _():
        x = x_ref[...]
        xn_ref[...] = (x * _rms_scale(x) * g_ref[...]).astype(BF16)

    o_ref[...] = jnp.dot(xn_ref[...], w_ref[...], preferred_element_type=F32).astype(o_ref.dtype)


def _kv_proj(x, g, w, tn):
    m, k = x.shape
    nl, _, n = w.shape
    return pl.pallas_call(
        _kv_proj_kernel,
        grid=(nl, n // tn),
        in_specs=[
            pl.BlockSpec((m, k), lambda l, j: (0, 0)),
            pl.BlockSpec((1, k), lambda l, j: (0, 0)),
            pl.BlockSpec((None, k, tn), lambda l, j: (l, 0, j)),
        ],
        out_specs=pl.BlockSpec((None, m, tn), lambda l, j: (l, 0, j)),
        out_shape=jax.ShapeDtypeStruct((nl, m, n), BF16),
        scratch_shapes=[pltpu.VMEM((m, k), BF16)],
        compiler_params=_cparams("arbitrary", "arbitrary"),
        name="kv_proj",
    )(x, g.reshape(1, k), w)


def _cast_kernel(x_ref, *o_refs):
    off = 0
    for o_ref in o_refs:
        width = o_ref.shape[-1]
        o_ref[...] = x_ref[:, off:off + width].astype(BF16)
        off += width


def _cast_bf16(w, splits=None):
    nl, k, n = w.shape
    splits = splits or (n,)
    assert sum(splits) == n
    tk = SUBLANES
    while tk * 2 <= k and k % (tk * 2) == 0 and tk * 2 * n * 4 <= CAST_BLOCK_BYTES:
        tk *= 2
    outs = pl.pallas_call(
        _cast_kernel,
        grid=(nl, k // tk),
        in_specs=[pl.BlockSpec((None, tk, n), lambda l, i: (l, i, 0))],
        out_specs=[pl.BlockSpec((None, tk, s), lambda l, i: (l, i, 0)) for s in splits],
        out_shape=[jax.ShapeDtypeStruct((nl, k, s), BF16) for s in splits],
        compiler_params=_cparams("parallel", "parallel"),
        name="cast_bf16",
    )(w)
    return outs if len(splits) > 1 else outs[0]


def _matmul_res_kernel(a_ref, w_ref, r_ref, o_ref):
    o_ref[...] = r_ref[...] + jnp.dot(a_ref[...], w_ref[...], preferred_element_type=F32)


def _matmul_res(a, w, layer, res, tm, tn, name):
    m, k = a.shape
    n = w.shape[2]
    tm = min(tm, m)
    tn = min(tn, n)
    return pl.pallas_call(
        _matmul_res_kernel,
        grid=(m // tm, n // tn),
        in_specs=[
            pl.BlockSpec((tm, k), lambda i, j: (i, 0)),
            pl.BlockSpec((None, k, tn), lambda i, j: (layer, 0, j)),
            pl.BlockSpec((tm, tn), lambda i, j: (i, j)),
        ],
        out_specs=pl.BlockSpec((tm, tn), lambda i, j: (i, j)),
        out_shape=jax.ShapeDtypeStruct((m, n), F32),
        compiler_params=_cparams("parallel", "arbitrary"),
        name=name,
    )(a, w, res)


def _matmul2_res_kernel(a1_ref, a2_ref, w1_ref, w2_ref, r_ref, o_ref):
    acc = jnp.dot(a1_ref[...], w1_ref[...], preferred_element_type=F32)
    acc = acc + jnp.dot(a2_ref[...], w2_ref[...], preferred_element_type=F32)
    o_ref[...] = r_ref[...] + acc


def _matmul2_res(a1, a2, w, layer, res, tm, tn, name):
    m, k1 = a1.shape
    k2 = a2.shape[1]
    assert k1 == k2
    n = w.shape[2]
    tm = min(tm, m)
    tn = min(tn, n)
    return pl.pallas_call(
        _matmul2_res_kernel,
        grid=(m // tm, n // tn),
        in_specs=[
            pl.BlockSpec((tm, k1), lambda i, j: (i, 0)),
            pl.BlockSpec((tm, k2), lambda i, j: (i, 0)),
            pl.BlockSpec((None, k1, tn), lambda i, j: (layer, 0, j)),
            pl.BlockSpec((None, k2, tn), lambda i, j: (layer, 1, j)),
            pl.BlockSpec((tm, tn), lambda i, j: (i, j)),
        ],
        out_specs=pl.BlockSpec((tm, tn), lambda i, j: (i, j)),
        out_shape=jax.ShapeDtypeStruct((m, n), F32),
        compiler_params=_cparams("parallel", "arbitrary"),
        name=name,
    )(a1, a2, w, w, res)


def _ffn_kernel(x_ref, g_ref, go_ref, wg_ref, wu_ref, wd_ref, o_ref, xn_ref, *, out_norm):
    @pl.when(pl.program_id(1) == 0)
    def _():
        x = x_ref[...]
        xn_ref[...] = (x * _rms_scale(x) * g_ref[...]).astype(BF16)
        o_ref[...] = x

    xn = xn_ref[...]
    gate = jnp.dot(xn, wg_ref[...], preferred_element_type=F32)
    up = jnp.dot(xn, wu_ref[...], preferred_element_type=F32)
    hid = (_silu(gate) * up).astype(BF16)
    o_ref[...] += jnp.dot(hid, wd_ref[...], preferred_element_type=F32)

    if out_norm:
        @pl.when(pl.program_id(1) == pl.num_programs(1) - 1)
        def _():
            y = o_ref[...]
            o_ref[...] = y * _rms_scale(y) * go_ref[...]


def _ffn(x, g, wg, wu, wd, layer, tm, th, g_out=None):
    m, d = x.shape
    hdim = wg.shape[2]
    tm = min(tm, m)
    out_norm = g_out is not None
    g_out = g if g_out is None else g_out
    return pl.pallas_call(
        functools.partial(_ffn_kernel, out_norm=out_norm),
        grid=(m // tm, hdim // th),
        in_specs=[
            pl.BlockSpec((tm, d), lambda i, j: (i, 0)),
            pl.BlockSpec((1, d), lambda i, j: (0, 0)),
            pl.BlockSpec((1, d), lambda i, j: (0, 0)),
            pl.BlockSpec((None, d, th), lambda i, j: (layer, 0, j)),
            pl.BlockSpec((None, d, th), lambda i, j: (layer, 0, j)),
            pl.BlockSpec((None, th, d), lambda i, j: (layer, j, 0)),
        ],
        out_specs=pl.BlockSpec((tm, d), lambda i, j: (i, 0)),
        out_shape=jax.ShapeDtypeStruct((m, d), F32),
        scratch_shapes=[pltpu.VMEM((tm, d), BF16)],
        compiler_params=_cparams("parallel", "arbitrary"),
        name="ffn",
    )(x, g.reshape(1, d), g_out.reshape(1, d), wg, wu, wd)


def _xattn_kernel(q_ref, k_ref, v_ref, wo_ref, x_ref, o_ref, ob_ref, *, heads, scale):
    hd = q_ref.shape[2] // heads
    for h in range(heads):
        sl = slice(h * hd, (h + 1) * hd)
        s = lax.dot_general(q_ref[0, :, sl], k_ref[0, :, sl], (((1,), (1,)), ((), ())),
                            preferred_element_type=F32) * scale
        p = jnp.exp(s - jnp.max(s, axis=-1, keepdims=True))
        p = p / jnp.sum(p, axis=-1, keepdims=True)
        ob_ref[:, sl] = jnp.dot(p.astype(BF16), v_ref[0, :, sl], preferred_element_type=F32).astype(BF16)
    o_ref[0] = x_ref[0] + jnp.dot(ob_ref[...], wo_ref[...], preferred_element_type=F32)


def _xattn(q, kv, layer, wo, x, tm):
    b, s, d = x.shape
    mem = kv.shape[2]
    tm = min(tm, s)
    kern = functools.partial(_xattn_kernel, heads=XA_HEADS, scale=(d // XA_HEADS) ** -0.5)
    return pl.pallas_call(
        kern,
        grid=(b, s // tm),
        in_specs=[
            pl.BlockSpec((1, tm, d), lambda i, j: (i, j, 0)),
            pl.BlockSpec((None, 1, mem, d), lambda i, j: (layer, i, 0, 0)),
            pl.BlockSpec((None, 1, mem, d), lambda i, j: (layer, i, 0, 1)),
            pl.BlockSpec((None, d, d), lambda i, j: (layer, 0, 0)),
            pl.BlockSpec((1, tm, d), lambda i, j: (i, j, 0)),
        ],
        out_specs=pl.BlockSpec((1, tm, d), lambda i, j: (i, j, 0)),
        out_shape=jax.ShapeDtypeStruct((b, s, d), F32),
        scratch_shapes=[pltpu.VMEM((tm, d), BF16)],
        compiler_params=_cparams("parallel", "arbitrary"),
        name="xattn",
    )(q, kv, kv, wo, x)


def _lru_kernel(xa_ref, ga_ref, cw_ref, cb_ref, wri_ref, br_ref, bi_ref, sp_ref, o_ref, buf_ref, h_ref):
    @pl.when(pl.program_id(1) == 0)
    def _():
        buf_ref[0:SUBLANES, :] = jnp.zeros((SUBLANES, buf_ref.shape[1]), F32)
        h_ref[...] = jnp.zeros(h_ref.shape, F32)

    xc = _causal_conv(xa_ref[0], buf_ref, cw_ref, cb_ref)
    t = xc.shape[0]
    for blk in range(xc.shape[1] // LRU_BLOCK):
        sl = slice(blk * LRU_BLOCK, (blk + 1) * LRU_BLOCK)
        xb = xc[:, sl]
        pre = jnp.dot(xb.astype(BF16), wri_ref[blk], preferred_element_type=F32)
        r_gate = _sigmoid(pre[:, :LRU_BLOCK] + br_ref[:, sl])
        i_gate = _sigmoid(pre[:, LRU_BLOCK:] + bi_ref[:, sl])
        log_a = (-LRU_C) * r_gate * sp_ref[:, sl]
        a = jnp.exp(log_a)
        mult = _sqrt_nonneg(-jnp.tanh(log_a) * (a * a + 1.0))
        h = _linear_scan_rows(a, mult * i_gate * xb, h_ref[0:1, sl])
        h_ref[0:1, sl] = h[t - 1:t, :]
        o_ref[0, :, sl] = (_gelu_tanh(ga_ref[0, :, sl]) * h).astype(BF16)


def _lru(proj, conv_w, conv_b, w_ri, b_r, b_i, sp, tb):
    b, s, _ = proj.shape
    w = conv_w.shape[1]
    tb = min(tb, s)
    vec = lambda: pl.BlockSpec((1, w), lambda i, j: (0, 0))
    return pl.pallas_call(
        _lru_kernel,
        grid=(b, s // tb),
        in_specs=[
            pl.BlockSpec((1, tb, w), lambda i, j: (i, j, 0)),
            pl.BlockSpec((1, tb, w), lambda i, j: (i, j, 1)),
            pl.BlockSpec((CONV_K, w), lambda i, j: (0, 0)),
            vec(),
            pl.BlockSpec(w_ri.shape, lambda i, j: (0, 0, 0)),
            vec(), vec(), vec(),
        ],
        out_specs=pl.BlockSpec((1, tb, w), lambda i, j: (i, j, 0)),
        out_shape=jax.ShapeDtypeStruct((b, s, w), BF16),
        scratch_shapes=[pltpu.VMEM((tb + SUBLANES, w), F32), pltpu.VMEM((SUBLANES, w), F32)],
        compiler_params=_cparams("parallel", "arbitrary"),
        name="lru",
    )(proj, proj, conv_w, conv_b.reshape(1, w), w_ri, b_r.reshape(1, w), b_i.reshape(1, w), sp.reshape(1, w))


def _hgrn_chunk(q, f, v, lb, st):
    c, dk = q.shape
    nsub = c // HGRN_SUB
    qh = _silu(q)
    fg = lb + (1.0 - lb) * _sigmoid(f)
    kh = 1.0 - fg
    g = jnp.log(fg)
    cum = _cumsum(g, 0)
    ex = cum - g
    cum3 = cum.reshape(nsub, HGRN_SUB, dk)
    ex3 = ex.reshape(nsub, HGRN_SUB, dk)
    base3 = jnp.broadcast_to(ex3[:, 0:1, :], cum3.shape)
    q3 = qh.reshape(nsub, HGRN_SUB, dk)
    k3 = kh.reshape(nsub, HGRN_SUB, dk)
    v3 = v.reshape(nsub, HGRN_SUB, dk)
    vb = v.astype(BF16)

    row3 = lax.broadcasted_iota(jnp.int32, cum3.shape, 1)
    terms = []
    for s in range(HGRN_SUB):
        diff = cum3 - cum3[:, s:s + 1, :]
        dec = jnp.exp(jnp.where(row3 >= s, diff, NEG_INF))
        terms.append((dec * q3 * k3[:, s:s + 1, :]).reshape(c, dk))
    stacked = jnp.concatenate(terms, axis=0).astype(BF16)
    ones = jnp.ones((dk, dk), BF16)
    summed = jnp.dot(stacked, ones, preferred_element_type=F32)
    o3 = jnp.zeros(cum3.shape, F32)
    for s in range(HGRN_SUB):
        o3 = o3 + summed[s * c:(s + 1) * c, :].reshape(nsub, HGRN_SUB, dk) * v3[:, s:s + 1, :]
    o = o3.reshape(c, dk)

    q_loc = (qh * jnp.exp(cum - base3.reshape(c, dk))).astype(BF16)
    pad = jnp.zeros((LANES, dk), F32)
    k_parts = []
    for i in range(1, nsub):
        n = i * HGRN_SUB
        base_i = ex[n:n + 1, :]
        k_parts.append(kh[0:n, :] * jnp.exp(base_i - cum[0:n, :]))
        k_parts.append(pad[0:LANES - n, :])
    k_hat = jnp.concatenate(k_parts, axis=0).astype(BF16)
    a_all = lax.dot_general(q_loc, k_hat, (((1,), (1,)), ((), ())), preferred_element_type=F32)
    a_rows = [jnp.zeros((HGRN_SUB, c), F32)]
    for i in range(1, nsub):
        a_rows.append(a_all[i * HGRN_SUB:(i + 1) * HGRN_SUB, (i - 1) * LANES:(i - 1) * LANES + c])
    a_off = jnp.concatenate(a_rows, axis=0).astype(BF16)
    o = o + jnp.dot(a_off, vb, preferred_element_type=F32)

    q_in = (qh * jnp.exp(cum)).astype(BF16)
    o = o + lax.dot_general(q_in, st.astype(BF16), (((1,), (1,)), ((), ())), preferred_element_type=F32)
    last = cum[c - 1:c, :]
    k_out = (kh * jnp.exp(last - cum)).astype(BF16)
    st_new = st * jnp.exp(last) + lax.dot_general(vb, k_out, (((0,), (0,)), ((), ())),
                                                  preferred_element_type=F32)
    return o, st_new


def _hgrn_kernel(q_ref, f_ref, v_ref, gb_ref, lb_ref, gn_ref, o_ref, st_ref):
    @pl.when(pl.program_id(2) == 0)
    def _():
        st_ref[...] = jnp.zeros(st_ref.shape, F32)

    lb = lb_ref[...]
    st = st_ref[...]
    for ci in range(q_ref.shape[1] // HGRN_CHUNK):
        rows = slice(ci * HGRN_CHUNK, (ci + 1) * HGRN_CHUNK)
        o, st = _hgrn_chunk(q_ref[0, rows, :], f_ref[0, rows, :], v_ref[0, rows, :], lb, st)
        o = o * _rms_scale(o) * gn_ref[...]
        o_ref[0, rows, :] = (o * _silu(gb_ref[0, rows, :])).astype(BF16)
    st_ref[...] = st


def _hgrn(proj, lower_bound, head_norm, col0, heads, tb):
    b, s, _ = proj.shape
    tb = min(tb, s)
    c0 = col0 // HGRN_DK
    part = lambda p: pl.BlockSpec((1, tb, HGRN_DK), lambda i, h, j: (i, j, c0 + p * heads + h))
    vec = lambda: pl.BlockSpec((1, HGRN_DK), lambda i, h, j: (0, h))
    w = heads * HGRN_DK
    return pl.pallas_call(
        _hgrn_kernel,
        grid=(b, heads, s // tb),
        in_specs=[part(0), part(1), part(2), part(3), vec(), vec()],
        out_specs=pl.BlockSpec((1, tb, HGRN_DK), lambda i, h, j: (i, j, h)),
        out_shape=jax.ShapeDtypeStruct((b, s, w), BF16),
        scratch_shapes=[pltpu.VMEM((HGRN_DK, HGRN_DK), F32)],
        compiler_params=_cparams("parallel", "parallel", "arbitrary"),
        name="hgrn",
    )(proj, proj, proj, proj, lower_bound.reshape(1, w), head_norm.reshape(1, w))


def _split3(x):
    hi = x.astype(BF16).astype(F32)
    rem = x - hi
    mid = rem.astype(BF16).astype(F32)
    return hi, mid, rem - mid


def _ssd_scan_block(z_ref, x_ref, b_ref, c_ref, dtr_ref, conv_refs, dbr_ref, anr_ref, dsk_ref, nw_ref,
                    tri_ref, sel_ref, st_ref, y_ref, buf_refs, tail_refs, first):
    for buf_ref, tail_ref in zip(buf_refs, tail_refs):
        if first:
            buf_ref[0:SUBLANES, :] = jnp.zeros((SUBLANES, buf_ref.shape[1]), F32)
        else:
            buf_ref[0:SUBLANES, :] = tail_ref[...]
    if first:
        st_ref[...] = jnp.zeros(st_ref.shape, F32)

    tb = x_ref.shape[1]
    t = tri_ref.shape[0]
    n = st_ref.shape[0]
    hpg = dbr_ref.shape[0]
    (cwx_ref, cbx_ref, cwb_ref, cbb_ref, cwc_ref, cbc_ref) = conv_refs
    xs_all = _silu(_causal_conv(x_ref[0], buf_refs[0], cwx_ref, cbx_ref))
    bm_all = _silu(_causal_conv(b_ref[0], buf_refs[1], cwb_ref, cbb_ref))
    cm_all = _silu(_causal_conv(c_ref[0], buf_refs[2], cwc_ref, cbc_ref))
    for buf_ref, tail_ref in zip(buf_refs, tail_refs):
        tail_ref[...] = buf_ref[0:SUBLANES, :]
    dtr_all = _softplus(dtr_ref[0, 0] + dbr_ref[...])
    causal = (lax.broadcasted_iota(jnp.int32, (t, t), 0) >= lax.broadcasted_iota(jnp.int32, (t, t), 1))
    low_t = lax.broadcasted_iota(jnp.int32, (t, LANES), 1) < SSD_HEADDIM
    low_n = lax.broadcasted_iota(jnp.int32, (n, LANES), 1) < SSD_HEADDIM
    pad = jnp.zeros((sel_ref.shape[0] - 3 * hpg, t), F32)

    for ci in range(tb // t):
        rows = slice(ci * t, (ci + 1) * t)
        xs, bm, cm, dtr = xs_all[rows], bm_all[rows], cm_all[rows], dtr_all[:, rows]
        xsb = xs.astype(BF16)
        dta = dtr * anr_ref[...]
        parts = jnp.concatenate(list(_split3(dta)) + [jnp.zeros_like(dta)], axis=0).astype(BF16)
        c3 = jnp.dot(parts, tri_ref[...], preferred_element_type=F32)
        cum = c3[0:hpg] + c3[hpg:2 * hpg] + c3[2 * hpg:3 * hpg]
        cols = jnp.concatenate(list(_split3(cum)) + [pad], axis=0).astype(BF16)
        bc_all = lax.dot_general(cols, sel_ref[...], (((0,), (0,)), ((), ())), preferred_element_type=F32)
        row_term = cum - jnp.log(dtr)
        last = jnp.broadcast_to(cum[:, t - 1:t], cum.shape)
        row_scale = jnp.exp(last - cum) * dtr
        e_last = jnp.exp(bc_all[t - 1:t, :])
        cb = lax.dot_general(cm.astype(BF16), bm.astype(BF16), (((1,), (1,)), ((), ())),
                             preferred_element_type=F32)
        bt = bm.T

        for pair in range(hpg // 2):
            pc = slice(pair * LANES, (pair + 1) * LANES)
            rhs = jnp.concatenate([xsb[:, pc], st_ref[:, pc].astype(BF16)], axis=0)
            lhs, zl = [], []
            for h in (2 * pair, 2 * pair + 1):
                bc = bc_all[:, h * LANES:(h + 1) * LANES]
                dec = jnp.exp(jnp.where(causal, bc - row_term[h:h + 1, :], NEG_INF))
                lhs.append(jnp.concatenate([cb * dec, cm * jnp.exp(bc)], axis=1).astype(BF16))
                zl.append((bt * row_scale[h:h + 1, :]).astype(BF16))
            y2 = jnp.dot(jnp.concatenate(lhs, axis=0), rhs, preferred_element_type=F32)
            y_ref[rows, pc] = jnp.where(low_t, y2[:t], y2[t:]) + dsk_ref[:, pc] * xs[:, pc]
            z2 = jnp.dot(jnp.concatenate(zl, axis=0), xsb[:, pc], preferred_element_type=F32)
            e_pair = jnp.where(low_n[0:1], e_last[:, 2 * pair * LANES:(2 * pair + 1) * LANES],
                               e_last[:, (2 * pair + 1) * LANES:(2 * pair + 2) * LANES])
            st_ref[:, pc] = st_ref[:, pc] * e_pair + jnp.where(low_n, z2[:n], z2[n:])

    y = y_ref[...] * _silu(z_ref[0])
    return (y * _rms_scale(y) * nw_ref[...]).astype(BF16)


def _ssd_kernel(z_ref, x_ref, b_ref, c_ref, dtr_ref,
                cwx_ref, cbx_ref, cwb_ref, cbb_ref, cwc_ref, cbc_ref,
                dbr_ref, anr_ref, dsk_ref, nw_ref, tri_ref, sel_ref, wo_ref, res_ref,
                o_ref, st_ref, y_ref, yn_ref, bufx_ref, bufb_ref, bufc_ref, tailx_ref, tailb_ref, tailc_ref,
                *, row_blocks):
    k = pl.program_id(1)
    g = pl.program_id(2)
    slot = lax.rem(k, 2)
    groups = yn_ref.shape[1]

    def scan(first):
        yn_ref[slot, g] = _ssd_scan_block(
            z_ref, x_ref, b_ref, c_ref, dtr_ref, (cwx_ref, cbx_ref, cwb_ref, cbb_ref, cwc_ref, cbc_ref),
            dbr_ref, anr_ref, dsk_ref, nw_ref, tri_ref, sel_ref, st_ref.at[g], y_ref,
            (bufx_ref, bufb_ref, bufc_ref), (tailx_ref.at[g], tailb_ref.at[g], tailc_ref.at[g]), first)

    def project():
        lhs = jnp.concatenate([yn_ref[1 - slot, gg] for gg in range(groups)], axis=1)
        o_ref[0] = res_ref[0] + jnp.dot(lhs, wo_ref[...], preferred_element_type=F32)

    @pl.when(k == 0)
    def _():
        scan(True)

    @pl.when((k > 0) & (k < row_blocks))
    def _():
        project()
        scan(False)

    @pl.when(k == row_blocks)
    def _():
        project()


def _ssd(proj, dt_raw, conv_w, conv_b, dt_bias, a_neg, d_skip, norm_w, w_out, layer, res, inner, groups):
    b, s, _ = proj.shape
    d = res.shape[2]
    dg = d // groups
    t = min(SSD_CHUNK, s)
    tb = min(SSD_TB, s)
    gw = inner // groups
    n = SSD_STATE
    hpg = SSD_HPG
    assert gw == hpg * SSD_HEADDIM and 2 * SSD_HEADDIM == LANES and tb % t == 0
    dt_row = dt_raw.reshape(b, s, groups, hpg).transpose(0, 2, 3, 1)
    db_r = dt_bias.reshape(groups, hpg, 1)
    an_r = a_neg.reshape(groups, hpg, 1)
    dsk = jnp.repeat(d_skip, SSD_HEADDIM).reshape(1, inner)
    tri = (jnp.arange(t)[:, None] <= jnp.arange(t)[None, :]).astype(BF16)
    krow = jnp.arange(LANES)[:, None]
    sel = ((krow < 3 * hpg) & ((krow % hpg) == (jnp.arange(hpg * LANES)[None, :] // LANES))).astype(BF16)
    cbias = conv_b.reshape(1, -1)
    xoff, boff, coff = inner // gw, (2 * inner) // n, (2 * inner + groups * n) // n
    cboff, ccoff = inner // n, (inner + groups * n) // n
    nk = s // tb
    cur = lambda k: jnp.minimum(k, nk - 1)
    prev = lambda k: jnp.maximum(k - 1, 0)
    out_col = lambda k, g: jnp.where(k == 0, 0, g)
    col = lambda: pl.BlockSpec((None, hpg, 1), lambda i, k, g: (g, 0, 0))
    const = lambda a: pl.BlockSpec(a.shape, lambda i, k, g: (0, 0))
    return pl.pallas_call(
        functools.partial(_ssd_kernel, row_blocks=nk),
        grid=(b, nk + 1, groups),
        in_specs=[
            pl.BlockSpec((1, tb, gw), lambda i, k, g: (i, cur(k), g)),
            pl.BlockSpec((1, tb, gw), lambda i, k, g: (i, cur(k), xoff + g)),
            pl.BlockSpec((1, tb, n), lambda i, k, g: (i, cur(k), boff + g)),
            pl.BlockSpec((1, tb, n), lambda i, k, g: (i, cur(k), coff + g)),
            pl.BlockSpec((1, 1, hpg, tb), lambda i, k, g: (i, g, 0, cur(k))),
            pl.BlockSpec((CONV_K, gw), lambda i, k, g: (0, g)),
            pl.BlockSpec((1, gw), lambda i, k, g: (0, g)),
            pl.BlockSpec((CONV_K, n), lambda i, k, g: (0, cboff + g)),
            pl.BlockSpec((1, n), lambda i, k, g: (0, cboff + g)),
            pl.BlockSpec((CONV_K, n), lambda i, k, g: (0, ccoff + g)),
            pl.BlockSpec((1, n), lambda i, k, g: (0, ccoff + g)),
            col(), col(),
            pl.BlockSpec((1, gw), lambda i, k, g: (0, g)),
            pl.BlockSpec((1, gw), lambda i, k, g: (0, g)),
            const(tri), const(sel),
            pl.BlockSpec((None, inner, dg), lambda i, k, g: (layer, 0, g)),
            pl.BlockSpec((1, tb, dg), lambda i, k, g: (i, prev(k), out_col(k, g))),
        ],
        out_specs=pl.BlockSpec((1, tb, dg), lambda i, k, g: (i, prev(k), out_col(k, g))),
        out_shape=jax.ShapeDtypeStruct((b, s, d), F32),
        scratch_shapes=[
            pltpu.VMEM((groups, n, gw), F32),
            pltpu.VMEM((tb, gw), F32),
            pltpu.VMEM((2, groups, tb, gw), BF16),
            pltpu.VMEM((tb + SUBLANES, gw), F32),
            pltpu.VMEM((tb + SUBLANES, n), F32),
            pltpu.VMEM((tb + SUBLANES, n), F32),
            pltpu.VMEM((groups, SUBLANES, gw), F32),
            pltpu.VMEM((groups, SUBLANES, n), F32),
            pltpu.VMEM((groups, SUBLANES, n), F32),
        ],
        compiler_params=_cparams("arbitrary", "arbitrary", "arbitrary"),
        name="ssd",
    )(proj, proj, proj, proj, dt_row, conv_w, cbias, conv_w, cbias, conv_w, cbias,
      db_r, an_r, dsk, norm_w.reshape(1, inner), tri, sel, w_out, res)


TM = 1024
TN = 1024
SQ_TM = 512
SQ_TN = 2048
FFN_TH = 512
XA_TM = 512
LRU_TB = 256
HGRN_TB = 512
SSD_TB = 512


def kernel(x, mem, norm_mix, norm_xattn, norm_ffn, norm_mem, norm_final, ab_w_in, ab_w_out, lru_conv_w, lru_conv_b, lru_w_r, lru_b_r, lru_w_i, lru_b_i, lru_lambda, hgrn_lower_bounds, hgrn_norm, ssd_w_in, ssd_w_out, ssd_conv_w, ssd_conv_b, ssd_dt_bias, ssd_a_log, ssd_d, ssd_norm, xa_w_q, xa_w_kv, xa_w_o, ffn_w_gate, ffn_w_up, ffn_w_down):
    bsz, seq, d = x.shape
    depth = norm_mix.shape[0]
    m = bsz * seq
    lru_w = lru_conv_w.shape[2]
    hgrn_w = hgrn_norm.shape[1]
    ssd_heads = ssd_a_log.shape[1]
    ssd_inner = ssd_norm.shape[1]
    ssd_groups = ssd_heads // SSD_HPG
    ssd_main = ssd_w_in.shape[2] - ssd_heads

    sm = jax.nn.softmax(hgrn_lower_bounds.astype(F32), axis=0)
    lower_bounds = jnp.cumsum(sm, axis=0) - sm[0]
    lru_sp = jax.nn.softplus(-lru_lambda.astype(F32))
    ssd_a_neg = -jnp.exp(ssd_a_log.astype(F32))
    lru_w_ri = jnp.concatenate([lru_w_r, lru_w_i], axis=-1).astype(BF16)
    ab_in_b = _cast_bf16(ab_w_in)
    ab_out_b = _cast_bf16(ab_w_out)
    ssd_in_b, ssd_dt_b = _cast_bf16(ssd_w_in, (ssd_main, ssd_heads))
    ssd_out_b = _cast_bf16(ssd_w_out)
    wq_b = _cast_bf16(xa_w_q)
    wkv_b = _cast_bf16(xa_w_kv)
    wo_b = _cast_bf16(xa_w_o)
    wg_b = _cast_bf16(ffn_w_gate)
    wu_b = _cast_bf16(ffn_w_up)
    wd_b = _cast_bf16(ffn_w_down)

    x2 = x.reshape(m, d)
    mem_len = mem.shape[1]
    kv = _kv_proj(mem.reshape(bsz * mem_len, d), norm_mem, wkv_b, TN).reshape(depth, bsz, mem_len, 2 * d)

    for layer in range(depth):
        if layer % 2 == 0:
            e = layer // 2
            proj = _norm_matmul(x2, norm_mix[layer], ab_in_b, e, F32, TM, TN, "ab_in_proj")
            proj = proj.reshape(bsz, seq, -1)
            ya = _lru(proj, lru_conv_w[e], lru_conv_b[e], lru_w_ri[e], lru_b_r[e], lru_b_i[e], lru_sp[e], LRU_TB)
            yb = _hgrn(proj, lower_bounds[e], hgrn_norm[e], 2 * lru_w, hgrn_w // HGRN_DK, HGRN_TB)
            x2 = _matmul2_res(ya.reshape(m, lru_w), yb.reshape(m, hgrn_w), ab_out_b, e, x2, SQ_TM, SQ_TN, "ab_out_proj")
        else:
            o = layer // 2
            proj, dt_raw = _norm_matmul_side(x2, norm_mix[layer], ssd_in_b, ssd_dt_b, o, TM, TN, "ssd_in_proj")
            x2 = _ssd(proj.reshape(bsz, seq, ssd_main), dt_raw.reshape(bsz, seq, ssd_heads),
                      ssd_conv_w[o], ssd_conv_b[o], ssd_dt_bias[o], ssd_a_neg[o], ssd_d[o], ssd_norm[o],
                      ssd_out_b, o, x2.reshape(bsz, seq, d), ssd_inner, ssd_groups).reshape(m, d)
        q = _norm_matmul(x2, norm_xattn[layer], wq_b, layer, BF16, SQ_TM, SQ_TN, "xa_q_proj")
        x2 = _xattn(q.reshape(bsz, seq, d), kv, layer, wo_b, x2.reshape(bsz, seq, d), XA_TM)
        x2 = x2.reshape(m, d)
        g_out = norm_final if layer == depth - 1 else None
        x2 = _ffn(x2, norm_ffn[layer], wg_b, wu_b, wd_b, layer, TM, FFN_TH, g_out)
    return x2.reshape(bsz, seq, d)
```

```python
import functools
import math

import jax
import jax.numpy as jnp
from jax import lax
from jax.experimental import pallas as pl
from jax.experimental.pallas import tpu as pltpu

F32 = jnp.float32
BF16 = jnp.bfloat16
EPS = 1e-6
NEG_INF = float("-inf")

VMEM_LIMIT_BYTES = 56 * 1024 * 1024
CAST_BLOCK_BYTES = 4 * 1024 * 1024
SUBLANES = 8
LANES = 128

CONV_K = 4
LRU_BLOCK = 128
LRU_C = 8.0
HGRN_DK = 128
HGRN_CHUNK = 64
HGRN_SUB = SUBLANES
SSD_HEADDIM = 64
SSD_HPG = 8
SSD_STATE = 128
SSD_CHUNK = 128
XA_HEADS = 4
MEM_LEN = 256


def _cparams(*sem):
    return pltpu.CompilerParams(dimension_semantics=sem, vmem_limit_bytes=VMEM_LIMIT_BYTES)


def _rms_scale(x):
    return lax.rsqrt(jnp.mean(x * x, axis=-1, keepdims=True) + EPS)


def _sigmoid(x):
    return 0.5 * (1.0 + jnp.tanh(0.5 * x))


def _silu(x):
    return x * _sigmoid(x)


def _gelu_tanh(x):
    c = math.sqrt(2.0 / math.pi)
    return x * (0.5 * (1.0 + jnp.tanh(c * (x + 0.044715 * (x * x * x)))))


def _softplus(x):
    return jnp.maximum(x, 0.0) + jnp.log1p(jnp.exp(-jnp.abs(x)))


def _linear_scan_rows(a, b, h0):
    t, c = a.shape
    groups = t // SUBLANES
    a3 = a.reshape(groups, SUBLANES, c)
    b3 = b.reshape(groups, SUBLANES, c)
    row = lax.broadcasted_iota(jnp.int32, a3.shape, 1)
    s = 1
    while s < SUBLANES:
        keep = row >= s
        a_sh = pltpu.roll(a3, s, 1)
        b_sh = pltpu.roll(b3, s, 1)
        b3 = jnp.where(keep, a3 * b_sh + b3, b3)
        a3 = jnp.where(keep, a3 * a_sh, a3)
        s *= 2
    out = []
    carry = h0
    for g in range(groups):
        h = a3[g] * carry + b3[g]
        out.append(h)
        carry = h[SUBLANES - 1:SUBLANES, :]
    return jnp.concatenate(out, axis=0)


def _sqrt_nonneg(y):
    return jnp.where(y > 0.0, y * lax.rsqrt(y), 0.0)


def _cumsum(x, axis):
    n = x.shape[axis]
    idx = lax.broadcasted_iota(jnp.int32, x.shape, axis)
    s = 1
    while s < n:
        x = jnp.where(idx >= s, x + pltpu.roll(x, s, axis), x)
        s *= 2
    return x


def _causal_conv(u, buf_ref, w_ref, b_ref):
    t = u.shape[0]
    buf_ref[SUBLANES:SUBLANES + t, :] = u
    out = b_ref[...] + w_ref[3:4, :] * u
    for j in range(1, CONV_K):
        out = out + w_ref[CONV_K - 1 - j:CONV_K - j, :] * buf_ref[SUBLANES - j:SUBLANES - j + t, :]
    buf_ref[0:SUBLANES, :] = u[t - SUBLANES:t, :]
    return out


def _norm_matmul_kernel(x_ref, g_ref, w_ref, o_ref, xn_ref):
    @pl.when(pl.program_id(1) == 0)
    def _():
        x = x_ref[...]
        xn_ref[...] = (x * _rms_scale(x) * g_ref[...]).astype(BF16)

    o_ref[...] = jnp.dot(xn_ref[...], w_ref[...], preferred_element_type=F32).astype(o_ref.dtype)


def _norm_matmul(x, g, w, layer, out_dtype, tm, tn, name):
    m, k = x.shape
    n = w.shape[2]
    tm = min(tm, m)
    tn = min(tn, n)
    return pl.pallas_call(
        _norm_matmul_kernel,
        grid=(m // tm, n // tn),
        in_specs=[
            pl.BlockSpec((tm, k), lambda i, j: (i, 0)),
            pl.BlockSpec((1, k), lambda i, j: (0, 0)),
            pl.BlockSpec((None, k, tn), lambda i, j: (layer, 0, j)),
        ],
        out_specs=pl.BlockSpec((tm, tn), lambda i, j: (i, j)),
        out_shape=jax.ShapeDtypeStruct((m, n), out_dtype),
        scratch_shapes=[pltpu.VMEM((tm, k), BF16)],
        compiler_params=_cparams("parallel", "arbitrary"),
        name=name,
    )(x, g.reshape(1, k), w)


def _norm_matmul_side_kernel(x_ref, g_ref, w_ref, ws_ref, o_ref, os_ref, xn_ref):
    @pl.when(pl.program_id(1) == 0)
    def _():
        x = x_ref[...]
        xn = (x * _rms_scale(x) * g_ref[...]).astype(BF16)
        xn_ref[...] = xn
        os_ref[...] = jnp.dot(xn, ws_ref[...], preferred_element_type=F32)

    o_ref[...] = jnp.dot(xn_ref[...], w_ref[...], preferred_element_type=F32).astype(o_ref.dtype)


def _norm_matmul_side(x, g, w, w_side, layer, tm, tn, name):
    m, k = x.shape
    n = w.shape[2]
    ns = w_side.shape[2]
    tm = min(tm, m)
    tn = min(tn, n)
    return pl.pallas_call(
        _norm_matmul_side_kernel,
        grid=(m // tm, n // tn),
        in_specs=[
            pl.BlockSpec((tm, k), lambda i, j: (i, 0)),
            pl.BlockSpec((1, k), lambda i, j: (0, 0)),
            pl.BlockSpec((None, k, tn), lambda i, j: (layer, 0, j)),
            pl.BlockSpec((None, k, ns), lambda i, j: (layer, 0, 0)),
        ],
        out_specs=[pl.BlockSpec((tm, tn), lambda i, j: (i, j)), pl.BlockSpec((tm, ns), lambda i, j: (i, 0))],
        out_shape=[jax.ShapeDtypeStruct((m, n), F32), jax.ShapeDtypeStruct((m, ns), F32)],
        scratch_shapes=[pltpu.VMEM((tm, k), BF16)],
        compiler_params=_cparams("parallel", "arbitrary"),
        name=name,
    )(x, g.reshape(1, k), w, w_side)


def _kv_proj_kernel(x_ref, g_ref, w_ref, o_ref, xn_ref):
    @pl.when((pl.program_id(0) == 0) & (pl.program_id(1) == 0))
    def _():
        x = x_ref[...]
        xn_ref[...] = (x * _rms_scale(x) * g_ref[...]).astype(BF16)

    o_ref[...] = jnp.dot(xn_ref[...], w_ref[...], preferred_element_type=F32).astype(o_ref.dtype)


def _kv_proj(x, g, w, tn):
    m, k = x.shape
    nl, _, n = w.shape
    return pl.pallas_call(
        _kv_proj_kernel,
        grid=(nl, n // tn),
        in_specs=[
            pl.BlockSpec((m, k), lambda l, j: (0, 0)),
            pl.BlockSpec((1, k), lambda l, j: (0, 0)),
            pl.BlockSpec((None, k, tn), lambda l, j: (l, 0, j)),
        ],
        out_specs=pl.BlockSpec((None, m, tn), lambda l, j: (l, 0, j)),
        out_shape=jax.ShapeDtypeStruct((nl, m, n), BF16),
        scratch_shapes=[pltpu.VMEM((m, k), BF16)],
        compiler_params=_cparams("arbitrary", "arbitrary"),
        name="kv_proj",
    )(x, g.reshape(1, k), w)


def _cast_kernel(x_ref, *o_refs):
    off = 0
    for o_ref in o_refs:
        width = o_ref.shape[-1]
        o_ref[...] = x_ref[:, off:off + width].astype(BF16)
        off += width


def _cast_bf16(w, splits=None):
    nl, k, n = w.shape
    splits = splits or (n,)
    assert sum(splits) == n
    tk = SUBLANES
    while tk * 2 <= k and k % (tk * 2) == 0 and tk * 2 * n * 4 <= CAST_BLOCK_BYTES:
        tk *= 2
    outs = pl.pallas_call(
        _cast_kernel,
        grid=(nl, k // tk),
        in_specs=[pl.BlockSpec((None, tk, n), lambda l, i: (l, i, 0))],
        out_specs=[pl.BlockSpec((None, tk, s), lambda l, i: (l, i, 0)) for s in splits],
        out_shape=[jax.ShapeDtypeStruct((nl, k, s), BF16) for s in splits],
        compiler_params=_cparams("parallel", "parallel"),
        name="cast_bf16",
    )(w)
    return outs if len(splits) > 1 else outs[0]


def _ffn_kernel(x_ref, g_ref, go_ref, wg_ref, wu_ref, wd_ref, o_ref, xn_ref, *, out_norm):
    @pl.when(pl.program_id(1) == 0)
    def _():
        x = x_ref[...]
        xn_ref[...] = (x * _rms_scale(x) * g_ref[...]).astype(BF16)
        o_ref[...] = x

    xn = xn_ref[...]
    gate = jnp.dot(xn, wg_ref[...], preferred_element_type=F32)
    up = jnp.dot(xn, wu_ref[...], preferred_element_type=F32)
    hid = (_silu(gate) * up).astype(BF16)
    o_ref[...] += jnp.dot(hid, wd_ref[...], preferred_element_type=F32)

    if out_norm:
        @pl.when(pl.program_id(1) == pl.num_programs(1) - 1)
        def _():
            y = o_ref[...]
            o_ref[...] = y * _rms_scale(y) * go_ref[...]


def _ffn(x, g, wg, wu, wd, layer, tm, th, g_out=None):
    m, d = x.shape
    hdim = wg.shape[2]
    tm = min(tm, m)
    out_norm = g_out is not None
    g_out = g if g_out is None else g_out
    return pl.pallas_call(
        functools.partial(_ffn_kernel, out_norm=out_norm),
        grid=(m // tm, hdim // th),
        in_specs=[
            pl.BlockSpec((tm, d), lambda i, j: (i, 0)),
            pl.BlockSpec((1, d), lambda i, j: (0, 0)),
            pl.BlockSpec((1, d), lambda i, j: (0, 0)),
            pl.BlockSpec((None, d, th), lambda i, j: (layer, 0, j)),
            pl.BlockSpec((None, d, th), lambda i, j: (layer, 0, j)),
            pl.BlockSpec((None, th, d), lambda i, j: (layer, j, 0)),
        ],
        out_specs=pl.BlockSpec((tm, d), lambda i, j: (i, 0)),
        out_shape=jax.ShapeDtypeStruct((m, d), F32),
        scratch_shapes=[pltpu.VMEM((tm, d), BF16)],
        compiler_params=_cparams("parallel", "arbitrary"),
        name="ffn",
    )(x, g.reshape(1, d), g_out.reshape(1, d), wg, wu, wd)


def _xattn_kernel(q_ref, k_ref, v_ref, wo_ref, x_ref, o_ref, ob_ref, *, heads, scale):
    hd = q_ref.shape[2] // heads
    for h in range(heads):
        sl = slice(h * hd, (h + 1) * hd)
        s = lax.dot_general(q_ref[0, :, sl], k_ref[0, :, sl], (((1,), (1,)), ((), ())),
                            preferred_element_type=F32) * scale
        p = jnp.exp(s - jnp.max(s, axis=-1, keepdims=True))
        p = p / jnp.sum(p, axis=-1, keepdims=True)
        ob_ref[:, sl] = jnp.dot(p.astype(BF16), v_ref[0, :, sl], preferred_element_type=F32).astype(BF16)
    o_ref[0] = x_ref[0] + jnp.dot(ob_ref[...], wo_ref[...], preferred_element_type=F32)


def _xattn(q, kv, layer, wo, x, tm):
    b, s, d = x.shape
    mem = kv.shape[2]
    tm = min(tm, s)
    kern = functools.partial(_xattn_kernel, heads=XA_HEADS, scale=(d // XA_HEADS) ** -0.5)
    return pl.pallas_call(
        kern,
        grid=(b, s // tm),
        in_specs=[
            pl.BlockSpec((1, tm, d), lambda i, j: (i, j, 0)),
            pl.BlockSpec((None, 1, mem, d), lambda i, j: (layer, i, 0, 0)),
            pl.BlockSpec((None, 1, mem, d), lambda i, j: (layer, i, 0, 1)),
            pl.BlockSpec((None, d, d), lambda i, j: (layer, 0, 0)),
            pl.BlockSpec((1, tm, d), lambda i, j: (i, j, 0)),
        ],
        out_specs=pl.BlockSpec((1, tm, d), lambda i, j: (i, j, 0)),
        out_shape=jax.ShapeDtypeStruct((b, s, d), F32),
        scratch_shapes=[pltpu.VMEM((tm, d), BF16)],
        compiler_params=_cparams("parallel", "arbitrary"),
        name="xattn",
    )(q, kv, kv, wo, x)


def _lru_kernel(xa_ref, ga_ref, cw_ref, cb_ref, wri_ref, br_ref, bi_ref, sp_ref, o_ref, buf_ref, h_ref):
    @pl.when(pl.program_id(1) == 0)
    def _():
        buf_ref[0:SUBLANES, :] = jnp.zeros((SUBLANES, buf_ref.shape[1]), F32)
        h_ref[...] = jnp.zeros(h_ref.shape, F32)

    xc = _causal_conv(xa_ref[0], buf_ref, cw_ref, cb_ref)
    t = xc.shape[0]
    for blk in range(xc.shape[1] // LRU_BLOCK):
        sl = slice(blk * LRU_BLOCK, (blk + 1) * LRU_BLOCK)
        xb = xc[:, sl]
        pre = jnp.dot(xb.astype(BF16), wri_ref[blk], preferred_element_type=F32)
        r_gate = _sigmoid(pre[:, :LRU_BLOCK] + br_ref[:, sl])
        i_gate = _sigmoid(pre[:, LRU_BLOCK:] + bi_ref[:, sl])
        log_a = (-LRU_C) * r_gate * sp_ref[:, sl]
        a = jnp.exp(log_a)
        mult = _sqrt_nonneg(-jnp.tanh(log_a) * (a * a + 1.0))
        h = _linear_scan_rows(a, mult * i_gate * xb, h_ref[0:1, sl])
        h_ref[0:1, sl] = h[t - 1:t, :]
        o_ref[0, :, sl] = (_gelu_tanh(ga_ref[0, :, sl]) * h).astype(BF16)


def _lru(proj, conv_w, conv_b, w_ri, b_r, b_i, sp, tb):
    b, s, _ = proj.shape
    w = conv_w.shape[1]
    tb = min(tb, s)
    vec = lambda: pl.BlockSpec((1, w), lambda i, j: (0, 0))
    return pl.pallas_call(
        _lru_kernel,
        grid=(b, s // tb),
        in_specs=[
            pl.BlockSpec((1, tb, w), lambda i, j: (i, j, 0)),
            pl.BlockSpec((1, tb, w), lambda i, j: (i, j, 1)),
            pl.BlockSpec((CONV_K, w), lambda i, j: (0, 0)),
            vec(),
            pl.BlockSpec(w_ri.shape, lambda i, j: (0, 0, 0)),
            vec(), vec(), vec(),
        ],
        out_specs=pl.BlockSpec((1, tb, w), lambda i, j: (i, j, 0)),
        out_shape=jax.ShapeDtypeStruct((b, s, w), BF16),
        scratch_shapes=[pltpu.VMEM((tb + SUBLANES, w), F32), pltpu.VMEM((SUBLANES, w), F32)],
        compiler_params=_cparams("parallel", "arbitrary"),
        name="lru",
    )(proj, proj, conv_w, conv_b.reshape(1, w), w_ri, b_r.reshape(1, w), b_i.reshape(1, w), sp.reshape(1, w))


def _hgrn_chunk(q, f, v, lb, st):
    c, dk = q.shape
    nsub = c // HGRN_SUB
    qh = _silu(q)
    fg = lb + (1.0 - lb) * _sigmoid(f)
    kh = 1.0 - fg
    g = jnp.log(fg)
    cum = _cumsum(g, 0)
    ex = cum - g
    cum3 = cum.reshape(nsub, HGRN_SUB, dk)
    ex3 = ex.reshape(nsub, HGRN_SUB, dk)
    base3 = jnp.broadcast_to(ex3[:, 0:1, :], cum3.shape)
    q3 = qh.reshape(nsub, HGRN_SUB, dk)
    k3 = kh.reshape(nsub, HGRN_SUB, dk)
    v3 = v.reshape(nsub, HGRN_SUB, dk)
    vb = v.astype(BF16)

    row3 = lax.broadcasted_iota(jnp.int32, cum3.shape, 1)
    terms = []
    for s in range(HGRN_SUB):
        diff = cum3 - cum3[:, s:s + 1, :]
        dec = jnp.exp(jnp.where(row3 >= s, diff, NEG_INF))
        terms.append((dec * q3 * k3[:, s:s + 1, :]).reshape(c, dk))
    stacked = jnp.concatenate(terms, axis=0).astype(BF16)
    ones = jnp.ones((dk, dk), BF16)
    summed = jnp.dot(stacked, ones, preferred_element_type=F32)
    o3 = jnp.zeros(cum3.shape, F32)
    for s in range(HGRN_SUB):
        o3 = o3 + summed[s * c:(s + 1) * c, :].reshape(nsub, HGRN_SUB, dk) * v3[:, s:s + 1, :]
    o = o3.reshape(c, dk)

    q_loc = (qh * jnp.exp(cum - base3.reshape(c, dk))).astype(BF16)
    pad = jnp.zeros((LANES, dk), F32)
    k_parts = []
    for i in range(1, nsub):
        n = i * HGRN_SUB
        base_i = ex[n:n + 1, :]
        k_parts.append(kh[0:n, :] * jnp.exp(base_i - cum[0:n, :]))
        k_parts.append(pad[0:LANES - n, :])
    k_hat = jnp.concatenate(k_parts, axis=0).astype(BF16)
    a_all = lax.dot_general(q_loc, k_hat, (((1,), (1,)), ((), ())), preferred_element_type=F32)
    a_rows = [jnp.zeros((HGRN_SUB, c), F32)]
    for i in range(1, nsub):
        a_rows.append(a_all[i * HGRN_SUB:(i + 1) * HGRN_SUB, (i - 1) * LANES:(i - 1) * LANES + c])
    a_off = jnp.concatenate(a_rows, axis=0).astype(BF16)
    o = o + jnp.dot(a_off, vb, preferred_element_type=F32)

    q_in = (qh * jnp.exp(cum)).astype(BF16)
    o = o + lax.dot_general(q_in, st.astype(BF16), (((1,), (1,)), ((), ())), preferred_element_type=F32)
    last = cum[c - 1:c, :]
    k_out = (kh * jnp.exp(last - cum)).astype(BF16)
    st_new = st * jnp.exp(last) + lax.dot_general(vb, k_out, (((0,), (0,)), ((), ())),
                                                  preferred_element_type=F32)
    return o, st_new


def _hgrn_kernel(q_ref, f_ref, v_ref, gb_ref, lb_ref, gn_ref, ya_ref, wa_ref, wb_ref, res_ref,
                 o_ref, st_ref, yb_ref, *, row_blocks):
    k = pl.program_id(1)
    h = pl.program_id(2)
    slot = lax.rem(k, 2)
    heads = yb_ref.shape[1]

    def scan(first):
        lb = lb_ref[h]
        gn = gn_ref[h]
        st = jnp.zeros(st_ref.shape[1:], F32) if first else st_ref[h]
        for ci in range(q_ref.shape[1] // HGRN_CHUNK):
            rows = slice(ci * HGRN_CHUNK, (ci + 1) * HGRN_CHUNK)
            o, st = _hgrn_chunk(q_ref[0, rows, :], f_ref[0, rows, :], v_ref[0, rows, :], lb, st)
            o = o * _rms_scale(o) * gn
            yb_ref[slot, h, rows, :] = (o * _silu(gb_ref[0, rows, :])).astype(BF16)
        st_ref[h] = st

    def project():
        yb = jnp.concatenate([yb_ref[1 - slot, hh] for hh in range(heads)], axis=1)
        acc = jnp.dot(ya_ref[0], wa_ref[...], preferred_element_type=F32)
        o_ref[0] = res_ref[0] + acc + jnp.dot(yb, wb_ref[...], preferred_element_type=F32)

    @pl.when(k == 0)
    def _():
        scan(True)

    @pl.when((k > 0) & (k < row_blocks))
    def _():
        project()
        scan(False)

    @pl.when(k == row_blocks)
    def _():
        project()


def _hgrn(proj, lower_bound, head_norm, ya, w_out, layer, res, col0, heads, tb):
    b, s, _ = proj.shape
    d = res.shape[2]
    tb = min(tb, s)
    nk = s // tb
    w = heads * HGRN_DK
    dg = d // heads
    c0 = col0 // HGRN_DK
    cur = lambda k: jnp.minimum(k, nk - 1)
    prev = lambda k: jnp.maximum(k - 1, 0)
    out_col = lambda k, h: jnp.where(k == 0, 0, h)
    part = lambda p: pl.BlockSpec((1, tb, HGRN_DK), lambda i, k, h: (i, cur(k), c0 + p * heads + h))
    vec = lambda: pl.BlockSpec((heads, 1, HGRN_DK), lambda i, k, h: (0, 0, 0))
    return pl.pallas_call(
        functools.partial(_hgrn_kernel, row_blocks=nk),
        grid=(b, nk + 1, heads),
        in_specs=[
            part(0), part(1), part(2), part(3), vec(), vec(),
            pl.BlockSpec((1, tb, w), lambda i, k, h: (i, prev(k), 0)),
            pl.BlockSpec((None, w, dg), lambda i, k, h: (layer, 0, h)),
            pl.BlockSpec((None, w, dg), lambda i, k, h: (layer, 1, h)),
            pl.BlockSpec((1, tb, dg), lambda i, k, h: (i, prev(k), out_col(k, h))),
        ],
        out_specs=pl.BlockSpec((1, tb, dg), lambda i, k, h: (i, prev(k), out_col(k, h))),
        out_shape=jax.ShapeDtypeStruct((b, s, d), F32),
        scratch_shapes=[
            pltpu.VMEM((heads, HGRN_DK, HGRN_DK), F32),
            pltpu.VMEM((2, heads, tb, HGRN_DK), BF16),
        ],
        compiler_params=_cparams("arbitrary", "arbitrary", "arbitrary"),
        name="hgrn",
    )(proj, proj, proj, proj, lower_bound.reshape(heads, 1, HGRN_DK), head_norm.reshape(heads, 1, HGRN_DK),
      ya, w_out, w_out, res)


def _split3(x):
    hi = x.astype(BF16).astype(F32)
    rem = x - hi
    mid = rem.astype(BF16).astype(F32)
    return hi, mid, rem - mid


def _ssd_scan_block(z_ref, x_ref, b_ref, c_ref, dtr_ref, conv_refs, dbr_ref, anr_ref, dsk_ref, nw_ref,
                    tri_ref, sel_ref, st_ref, y_ref, buf_refs, tail_refs, first):
    for buf_ref, tail_ref in zip(buf_refs, tail_refs):
        if first:
            buf_ref[0:SUBLANES, :] = jnp.zeros((SUBLANES, buf_ref.shape[1]), F32)
        else:
            buf_ref[0:SUBLANES, :] = tail_ref[...]
    if first:
        st_ref[...] = jnp.zeros(st_ref.shape, F32)

    tb = x_ref.shape[1]
    t = tri_ref.shape[0]
    n = st_ref.shape[0]
    hpg = dbr_ref.shape[0]
    (cwx_ref, cbx_ref, cwb_ref, cbb_ref, cwc_ref, cbc_ref) = conv_refs
    xs_all = _silu(_causal_conv(x_ref[0], buf_refs[0], cwx_ref, cbx_ref))
    bm_all = _silu(_causal_conv(b_ref[0], buf_refs[1], cwb_ref, cbb_ref))
    cm_all = _silu(_causal_conv(c_ref[0], buf_refs[2], cwc_ref, cbc_ref))
    for buf_ref, tail_ref in zip(buf_refs, tail_refs):
        tail_ref[...] = buf_ref[0:SUBLANES, :]
    dtr_all = _softplus(dtr_ref[0, 0] + dbr_ref[...])
    causal = (lax.broadcasted_iota(jnp.int32, (t, t), 0) >= lax.broadcasted_iota(jnp.int32, (t, t), 1))
    low_t = lax.broadcasted_iota(jnp.int32, (t, LANES), 1) < SSD_HEADDIM
    low_n = lax.broadcasted_iota(jnp.int32, (n, LANES), 1) < SSD_HEADDIM
    pad = jnp.zeros((sel_ref.shape[0] - 3 * hpg, t), F32)

    for ci in range(tb // t):
        rows = slice(ci * t, (ci + 1) * t)
        xs, bm, cm, dtr = xs_all[rows], bm_all[rows], cm_all[rows], dtr_all[:, rows]
        xsb = xs.astype(BF16)
        dta = dtr * anr_ref[...]
        parts = jnp.concatenate(list(_split3(dta)) + [jnp.zeros_like(dta)], axis=0).astype(BF16)
        c3 = jnp.dot(parts, tri_ref[...], preferred_element_type=F32)
        cum = c3[0:hpg] + c3[hpg:2 * hpg] + c3[2 * hpg:3 * hpg]
        cols = jnp.concatenate(list(_split3(cum)) + [pad], axis=0).astype(BF16)
        bc_all = lax.dot_general(cols, sel_ref[...], (((0,), (0,)), ((), ())), preferred_element_type=F32)
        row_term = cum - jnp.log(dtr)
        last = jnp.broadcast_to(cum[:, t - 1:t], cum.shape)
        row_scale = jnp.exp(last - cum) * dtr
        e_last = jnp.exp(bc_all[t - 1:t, :])
        cb = lax.dot_general(cm.astype(BF16), bm.astype(BF16), (((1,), (1,)), ((), ())),
                             preferred_element_type=F32)
        bt = bm.T

        for pair in range(hpg // 2):
            pc = slice(pair * LANES, (pair + 1) * LANES)
            rhs = jnp.concatenate([xsb[:, pc], st_ref[:, pc].astype(BF16)], axis=0)
            lhs, zl = [], []
            for h in (2 * pair, 2 * pair + 1):
                bc = bc_all[:, h * LANES:(h + 1) * LANES]
                dec = jnp.exp(jnp.where(causal, bc - row_term[h:h + 1, :], NEG_INF))
                lhs.append(jnp.concatenate([cb * dec, cm * jnp.exp(bc)], axis=1).astype(BF16))
                zl.append((bt * row_scale[h:h + 1, :]).astype(BF16))
            y2 = jnp.dot(jnp.concatenate(lhs, axis=0), rhs, preferred_element_type=F32)
            y_ref[rows, pc] = jnp.where(low_t, y2[:t], y2[t:]) + dsk_ref[:, pc] * xs[:, pc]
            z2 = jnp.dot(jnp.concatenate(zl, axis=0), xsb[:, pc], preferred_element_type=F32)
            e_pair = jnp.where(low_n[0:1], e_last[:, 2 * pair * LANES:(2 * pair + 1) * LANES],
                               e_last[:, (2 * pair + 1) * LANES:(2 * pair + 2) * LANES])
            st_ref[:, pc] = st_ref[:, pc] * e_pair + jnp.where(low_n, z2[:n], z2[n:])

    y = y_ref[...] * _silu(z_ref[0])
    return (y * _rms_scale(y) * nw_ref[...]).astype(BF16)


def _ssd_kernel(z_ref, x_ref, b_ref, c_ref, dtr_ref,
                cwx_ref, cbx_ref, cwb_ref, cbb_ref, cwc_ref, cbc_ref,
                dbr_ref, anr_ref, dsk_ref, nw_ref, tri_ref, sel_ref, wo_ref, res_ref,
                o_ref, st_ref, y_ref, yn_ref, bufx_ref, bufb_ref, bufc_ref, tailx_ref, tailb_ref, tailc_ref,
                *, row_blocks):
    k = pl.program_id(1)
    g = pl.program_id(2)
    slot = lax.rem(k, 2)
    groups = yn_ref.shape[1]

    def scan(first):
        yn_ref[slot, g] = _ssd_scan_block(
            z_ref, x_ref, b_ref, c_ref, dtr_ref,
            tuple(r.at[g] for r in (cwx_ref, cbx_ref, cwb_ref, cbb_ref, cwc_ref, cbc_ref)),
            dbr_ref.at[g], anr_ref.at[g], dsk_ref.at[g], nw_ref.at[g], tri_ref, sel_ref, st_ref.at[g], y_ref,
            (bufx_ref, bufb_ref, bufc_ref), (tailx_ref.at[g], tailb_ref.at[g], tailc_ref.at[g]), first)

    def project():
        lhs = jnp.concatenate([yn_ref[1 - slot, gg] for gg in range(groups)], axis=1)
        o_ref[0] = res_ref[0] + jnp.dot(lhs, wo_ref[...], preferred_element_type=F32)

    @pl.when(k == 0)
    def _():
        scan(True)

    @pl.when((k > 0) & (k < row_blocks))
    def _():
        project()
        scan(False)

    @pl.when(k == row_blocks)
    def _():
        project()


def _ssd(proj, dt_raw, conv_w, conv_b, dt_bias, a_neg, d_skip, norm_w, w_out, layer, res, inner, groups):
    b, s, _ = proj.shape
    d = res.shape[2]
    dg = d // groups
    t = min(SSD_CHUNK, s)
    tb = min(SSD_TB, s)
    gw = inner // groups
    n = SSD_STATE
    hpg = SSD_HPG
    assert gw == hpg * SSD_HEADDIM and 2 * SSD_HEADDIM == LANES and tb % t == 0
    dt_row = dt_raw.reshape(b, s, groups, hpg).transpose(0, 2, 3, 1)
    db_r = dt_bias.reshape(groups, hpg, 1)
    an_r = a_neg.reshape(groups, hpg, 1)
    dsk = jnp.repeat(d_skip, SSD_HEADDIM).reshape(groups, 1, gw)
    nw = norm_w.reshape(groups, 1, gw)
    by_group = lambda a, lo, w: (a[:, lo:lo + groups * w].reshape(a.shape[0], groups, w).transpose(1, 0, 2))
    conv_b2 = conv_b.reshape(1, -1)
    conv_parts = []
    for lo, w in ((0, gw), (inner, n), (inner + groups * n, n)):
        conv_parts += [by_group(conv_w, lo, w), by_group(conv_b2, lo, w)]
    tri = (jnp.arange(t)[:, None] <= jnp.arange(t)[None, :]).astype(BF16)
    krow = jnp.arange(LANES)[:, None]
    sel = ((krow < 3 * hpg) & ((krow % hpg) == (jnp.arange(hpg * LANES)[None, :] // LANES))).astype(BF16)
    xoff, boff, coff = inner // gw, (2 * inner) // n, (2 * inner + groups * n) // n
    nk = s // tb
    cur = lambda k: jnp.minimum(k, nk - 1)
    prev = lambda k: jnp.maximum(k - 1, 0)
    out_col = lambda k, g: jnp.where(k == 0, 0, g)
    const = lambda a: pl.BlockSpec(a.shape, lambda i, k, g: (0,) * a.ndim)
    small = conv_parts + [db_r, an_r, dsk, nw, tri, sel]
    return pl.pallas_call(
        functools.partial(_ssd_kernel, row_blocks=nk),
        grid=(b, nk + 1, groups),
        in_specs=[
            pl.BlockSpec((1, tb, gw), lambda i, k, g: (i, cur(k), g)),
            pl.BlockSpec((1, tb, gw), lambda i, k, g: (i, cur(k), xoff + g)),
            pl.BlockSpec((1, tb, n), lambda i, k, g: (i, cur(k), boff + g)),
            pl.BlockSpec((1, tb, n), lambda i, k, g: (i, cur(k), coff + g)),
            pl.BlockSpec((1, 1, hpg, tb), lambda i, k, g: (i, g, 0, cur(k))),
            *[const(a) for a in small],
            pl.BlockSpec((None, inner, dg), lambda i, k, g: (layer, 0, g)),
            pl.BlockSpec((1, tb, dg), lambda i, k, g: (i, prev(k), out_col(k, g))),
        ],
        out_specs=pl.BlockSpec((1, tb, dg), lambda i, k, g: (i, prev(k), out_col(k, g))),
        out_shape=jax.ShapeDtypeStruct((b, s, d), F32),
        scratch_shapes=[
            pltpu.VMEM((groups, n, gw), F32),
            pltpu.VMEM((tb, gw), F32),
            pltpu.VMEM((2, groups, tb, gw), BF16),
            pltpu.VMEM((tb + SUBLANES, gw), F32),
            pltpu.VMEM((tb + SUBLANES, n), F32),
            pltpu.VMEM((tb + SUBLANES, n), F32),
            pltpu.VMEM((groups, SUBLANES, gw), F32),
            pltpu.VMEM((groups, SUBLANES, n), F32),
            pltpu.VMEM((groups, SUBLANES, n), F32),
        ],
        compiler_params=_cparams("arbitrary", "arbitrary", "arbitrary"),
        name="ssd",
    )(proj, proj, proj, proj, dt_row, *small, w_out, res)


TM = 1024
TN = 1024
SQ_TM = 512
SQ_TN = 2048
FFN_TH = 512
XA_TM = 512
LRU_TB = 256
HGRN_TB = 512
SSD_TB = 512


def kernel(x, mem, norm_mix, norm_xattn, norm_ffn, norm_mem, norm_final, ab_w_in, ab_w_out, lru_conv_w, lru_conv_b, lru_w_r, lru_b_r, lru_w_i, lru_b_i, lru_lambda, hgrn_lower_bounds, hgrn_norm, ssd_w_in, ssd_w_out, ssd_conv_w, ssd_conv_b, ssd_dt_bias, ssd_a_log, ssd_d, ssd_norm, xa_w_q, xa_w_kv, xa_w_o, ffn_w_gate, ffn_w_up, ffn_w_down):
    bsz, seq, d = x.shape
    depth = norm_mix.shape[0]
    m = bsz * seq
    lru_w = lru_conv_w.shape[2]
    hgrn_w = hgrn_norm.shape[1]
    ssd_heads = ssd_a_log.shape[1]
    ssd_inner = ssd_norm.shape[1]
    ssd_groups = ssd_heads // SSD_HPG
    ssd_main = ssd_w_in.shape[2] - ssd_heads

    sm = jax.nn.softmax(hgrn_lower_bounds.astype(F32), axis=0)
    lower_bounds = jnp.cumsum(sm, axis=0) - sm[0]
    lru_sp = jax.nn.softplus(-lru_lambda.astype(F32))
    ssd_a_neg = -jnp.exp(ssd_a_log.astype(F32))
    lru_w_ri = jnp.concatenate([lru_w_r, lru_w_i], axis=-1).astype(BF16)
    ab_in_b = _cast_bf16(ab_w_in)
    ab_out_b = _cast_bf16(ab_w_out)
    ssd_in_b, ssd_dt_b = _cast_bf16(ssd_w_in, (ssd_main, ssd_heads))
    ssd_out_b = _cast_bf16(ssd_w_out)
    wq_b = _cast_bf16(xa_w_q)
    wkv_b = _cast_bf16(xa_w_kv)
    wo_b = _cast_bf16(xa_w_o)
    wg_b = _cast_bf16(ffn_w_gate)
    wu_b = _cast_bf16(ffn_w_up)
    wd_b = _cast_bf16(ffn_w_down)

    x2 = x.reshape(m, d)
    mem_len = mem.shape[1]
    kv = _kv_proj(mem.reshape(bsz * mem_len, d), norm_mem, wkv_b, TN).reshape(depth, bsz, mem_len, 2 * d)

    for layer in range(depth):
        if layer % 2 == 0:
            e = layer // 2
            proj = _norm_matmul(x2, norm_mix[layer], ab_in_b, e, F32, TM, TN, "ab_in_proj")
            proj = proj.reshape(bsz, seq, -1)
            ya = _lru(proj, lru_conv_w[e], lru_conv_b[e], lru_w_ri[e], lru_b_r[e], lru_b_i[e], lru_sp[e], LRU_TB)
            x2 = _hgrn(proj, lower_bounds[e], hgrn_norm[e], ya, ab_out_b, e, x2.reshape(bsz, seq, d),
                       2 * lru_w, hgrn_w // HGRN_DK, HGRN_TB).reshape(m, d)
        else:
            o = layer // 2
            proj, dt_raw = _norm_matmul_side(x2, norm_mix[layer], ssd_in_b, ssd_dt_b, o, TM, TN, "ssd_in_proj")
            x2 = _ssd(proj.reshape(bsz, seq, ssd_main), dt_raw.reshape(bsz, seq, ssd_heads),
                      ssd_conv_w[o], ssd_conv_b[o], ssd_dt_bias[o], ssd_a_neg[o], ssd_d[o], ssd_norm[o],
                      ssd_out_b, o, x2.reshape(bsz, seq, d), ssd_inner, ssd_groups).reshape(m, d)
        q = _norm_matmul(x2, norm_xattn[layer], wq_b, layer, BF16, SQ_TM, SQ_TN, "xa_q_proj")
        x2 = _xattn(q.reshape(bsz, seq, d), kv, layer, wo_b, x2.reshape(bsz, seq, d), XA_TM)
        x2 = x2.reshape(m, d)
        g_out = norm_final if layer == depth - 1 else None
        x2 = _ffn(x2, norm_ffn[layer], wg_b, wu_b, wd_b, layer, TM, FFN_TH, g_out)
    return x2.reshape(bsz, seq, d)
```

```python
import functools
import math

import jax
import jax.numpy as jnp
from jax import lax
from jax.experimental import pallas as pl
from jax.experimental.pallas import tpu as pltpu

F32 = jnp.float32
BF16 = jnp.bfloat16
EPS = 1e-6
NEG_INF = float("-inf")

VMEM_LIMIT_BYTES = 56 * 1024 * 1024
CAST_BLOCK_BYTES = 4 * 1024 * 1024
SUBLANES = 8
LANES = 128

CONV_K = 4
LRU_BLOCK = 128
LRU_C = 8.0
HGRN_DK = 128
HGRN_CHUNK = 64
HGRN_SUB = SUBLANES
SSD_HEADDIM = 64
SSD_HPG = 8
SSD_STATE = 128
SSD_CHUNK = 128
XA_HEADS = 4
MEM_LEN = 256


def _cparams(*sem):
    return pltpu.CompilerParams(dimension_semantics=sem, vmem_limit_bytes=VMEM_LIMIT_BYTES)


def _rms_scale(x):
    return lax.rsqrt(jnp.mean(x * x, axis=-1, keepdims=True) + EPS)


def _sigmoid(x):
    return 0.5 * (1.0 + jnp.tanh(0.5 * x))


def _silu(x):
    return x * _sigmoid(x)


def _gelu_tanh(x):
    c = math.sqrt(2.0 / math.pi)
    return x * (0.5 * (1.0 + jnp.tanh(c * (x + 0.044715 * (x * x * x)))))


def _softplus(x):
    return jnp.maximum(x, 0.0) + jnp.log1p(jnp.exp(-jnp.abs(x)))


def _linear_scan_rows(a, b, h0):
    t, c = a.shape
    groups = t // SUBLANES
    a3 = a.reshape(groups, SUBLANES, c)
    b3 = b.reshape(groups, SUBLANES, c)
    row = lax.broadcasted_iota(jnp.int32, a3.shape, 1)
    s = 1
    while s < SUBLANES:
        keep = row >= s
        a_sh = pltpu.roll(a3, s, 1)
        b_sh = pltpu.roll(b3, s, 1)
        b3 = jnp.where(keep, a3 * b_sh + b3, b3)
        a3 = jnp.where(keep, a3 * a_sh, a3)
        s *= 2
    out = []
    carry = h0
    for g in range(groups):
        h = a3[g] * carry + b3[g]
        out.append(h)
        carry = h[SUBLANES - 1:SUBLANES, :]
    return jnp.concatenate(out, axis=0)


def _sqrt_nonneg(y):
    return jnp.where(y > 0.0, y * lax.rsqrt(y), 0.0)


def _cumsum(x, axis):
    n = x.shape[axis]
    idx = lax.broadcasted_iota(jnp.int32, x.shape, axis)
    s = 1
    while s < n:
        x = jnp.where(idx >= s, x + pltpu.roll(x, s, axis), x)
        s *= 2
    return x


def _causal_conv(u, buf_ref, w_ref, b_ref):
    t = u.shape[0]
    buf_ref[SUBLANES:SUBLANES + t, :] = u
    out = b_ref[...] + w_ref[3:4, :] * u
    for j in range(1, CONV_K):
        out = out + w_ref[CONV_K - 1 - j:CONV_K - j, :] * buf_ref[SUBLANES - j:SUBLANES - j + t, :]
    buf_ref[0:SUBLANES, :] = u[t - SUBLANES:t, :]
    return out


def _norm_matmul_kernel(x_ref, g_ref, w_ref, o_ref, xn_ref):
    @pl.when(pl.program_id(1) == 0)
    def _():
        x = x_ref[...]
        xn_ref[...] = (x * _rms_scale(x) * g_ref[...]).astype(BF16)

    o_ref[...] = jnp.dot(xn_ref[...], w_ref[...], preferred_element_type=F32).astype(o_ref.dtype)


def _norm_matmul(x, g, w, layer, out_dtype, tm, tn, name):
    m, k = x.shape
    n = w.shape[2]
    tm = min(tm, m)
    tn = min(tn, n)
    return pl.pallas_call(
        _norm_matmul_kernel,
        grid=(m // tm, n // tn),
        in_specs=[
            pl.BlockSpec((tm, k), lambda i, j: (i, 0)),
            pl.BlockSpec((1, k), lambda i, j: (0, 0)),
            pl.BlockSpec((None, k, tn), lambda i, j: (layer, 0, j)),
        ],
        out_specs=pl.BlockSpec((tm, tn), lambda i, j: (i, j)),
        out_shape=jax.ShapeDtypeStruct((m, n), out_dtype),
        scratch_shapes=[pltpu.VMEM((tm, k), BF16)],
        compiler_params=_cparams("parallel", "arbitrary"),
        name=name,
    )(x, g.reshape(1, k), w)


def _norm_matmul_side_kernel(x_ref, g_ref, w_ref, ws_ref, o_ref, os_ref, xn_ref):
    @pl.when(pl.program_id(1) == 0)
    def _():
        x = x_ref[...]
        xn = (x * _rms_scale(x) * g_ref[...]).astype(BF16)
        xn_ref[...] = xn
        os_ref[...] = jnp.dot(xn, ws_ref[...], preferred_element_type=F32)

    o_ref[...] = jnp.dot(xn_ref[...], w_ref[...], preferred_element_type=F32).astype(o_ref.dtype)


def _norm_matmul_side(x, g, w, w_side, layer, tm, tn, name):
    m, k = x.shape
    n = w.shape[2]
    ns = w_side.shape[2]
    tm = min(tm, m)
    tn = min(tn, n)
    return pl.pallas_call(
        _norm_matmul_side_kernel,
        grid=(m // tm, n // tn),
        in_specs=[
            pl.BlockSpec((tm, k), lambda i, j: (i, 0)),
            pl.BlockSpec((1, k), lambda i, j: (0, 0)),
            pl.BlockSpec((None, k, tn), lambda i, j: (layer, 0, j)),
            pl.BlockSpec((None, k, ns), lambda i, j: (layer, 0, 0)),
        ],
        out_specs=[pl.BlockSpec((tm, tn), lambda i, j: (i, j)), pl.BlockSpec((tm, ns), lambda i, j: (i, 0))],
        out_shape=[jax.ShapeDtypeStruct((m, n), F32), jax.ShapeDtypeStruct((m, ns), F32)],
        scratch_shapes=[pltpu.VMEM((tm, k), BF16)],
        compiler_params=_cparams("parallel", "arbitrary"),
        name=name,
    )(x, g.reshape(1, k), w, w_side)


def _kv_proj_kernel(x_ref, g_ref, w_ref, o_ref, xn_ref):
    @pl.when((pl.program_id(0) == 0) & (pl.program_id(1) == 0))
    def _():
        x = x_ref[...]
        xn_ref[...] = (x * _rms_scale(x) * g_ref[...]).astype(BF16)

    o_ref[...] = jnp.dot(xn_ref[...], w_ref[...], preferred_element_type=F32).astype(o_ref.dtype)


def _kv_proj(x, g, w, tn):
    m, k = x.shape
    nl, _, n = w.shape
    return pl.pallas_call(
        _kv_proj_kernel,
        grid=(nl, n // tn),
        in_specs=[
            pl.BlockSpec((m, k), lambda l, j: (0, 0)),
            pl.BlockSpec((1, k), lambda l, j: (0, 0)),
            pl.BlockSpec((None, k, tn), lambda l, j: (l, 0, j)),
        ],
        out_specs=pl.BlockSpec((None, m, tn), lambda l, j: (l, 0, j)),
        out_shape=jax.ShapeDtypeStruct((nl, m, n), BF16),
        scratch_shapes=[pltpu.VMEM((m, k), BF16)],
        compiler_params=_cparams("arbitrary", "arbitrary"),
        name="kv_proj",
    )(x, g.reshape(1, k), w)


def _cast_kernel(x_ref, *o_refs):
    off = 0
    for o_ref in o_refs:
        width = o_ref.shape[-1]
        o_ref[...] = x_ref[:, off:off + width].astype(BF16)
        off += width


def _cast_bf16(w, splits=None):
    nl, k, n = w.shape
    splits = splits or (n,)
    assert sum(splits) == n
    tk = SUBLANES
    while tk * 2 <= k and k % (tk * 2) == 0 and tk * 2 * n * 4 <= CAST_BLOCK_BYTES:
        tk *= 2
    outs = pl.pallas_call(
        _cast_kernel,
        grid=(nl, k // tk),
        in_specs=[pl.BlockSpec((None, tk, n), lambda l, i: (l, i, 0))],
        out_specs=[pl.BlockSpec((None, tk, s), lambda l, i: (l, i, 0)) for s in splits],
        out_shape=[jax.ShapeDtypeStruct((nl, k, s), BF16) for s in splits],
        compiler_params=_cparams("parallel", "parallel"),
        name="cast_bf16",
    )(w)
    return outs if len(splits) > 1 else outs[0]


def _ffn_kernel(x_ref, g_ref, go_ref, wg_ref, wu_ref, wd_ref, o_ref, xn_ref, *, out_norm):
    @pl.when(pl.program_id(1) == 0)
    def _():
        x = x_ref[...]
        xn_ref[...] = (x * _rms_scale(x) * g_ref[...]).astype(BF16)
        o_ref[...] = x

    xn = xn_ref[...]
    gate = jnp.dot(xn, wg_ref[...], preferred_element_type=F32)
    up = jnp.dot(xn, wu_ref[...], preferred_element_type=F32)
    hid = (_silu(gate) * up).astype(BF16)
    o_ref[...] += jnp.dot(hid, wd_ref[...], preferred_element_type=F32)

    if out_norm:
        @pl.when(pl.program_id(1) == pl.num_programs(1) - 1)
        def _():
            y = o_ref[...]
            o_ref[...] = y * _rms_scale(y) * go_ref[...]


def _ffn(x, g, wg, wu, wd, layer, tm, th, g_out=None):
    m, d = x.shape
    hdim = wg.shape[2]
    tm = min(tm, m)
    out_norm = g_out is not None
    g_out = g if g_out is None else g_out
    return pl.pallas_call(
        functools.partial(_ffn_kernel, out_norm=out_norm),
        grid=(m // tm, hdim // th),
        in_specs=[
            pl.BlockSpec((tm, d), lambda i, j: (i, 0)),
            pl.BlockSpec((1, d), lambda i, j: (0, 0)),
            pl.BlockSpec((1, d), lambda i, j: (0, 0)),
            pl.BlockSpec((None, d, th), lambda i, j: (layer, 0, j)),
            pl.BlockSpec((None, d, th), lambda i, j: (layer, 0, j)),
            pl.BlockSpec((None, th, d), lambda i, j: (layer, j, 0)),
        ],
        out_specs=pl.BlockSpec((tm, d), lambda i, j: (i, 0)),
        out_shape=jax.ShapeDtypeStruct((m, d), F32),
        scratch_shapes=[pltpu.VMEM((tm, d), BF16)],
        compiler_params=_cparams("parallel", "arbitrary"),
        name="ffn",
    )(x, g.reshape(1, d), g_out.reshape(1, d), wg, wu, wd)


def _xattn_kernel(q_ref, k_ref, v_ref, wo_ref, x_ref, o_ref, ob_ref, *, heads, scale):
    hd = q_ref.shape[2] // heads
    for h in range(heads):
        sl = slice(h * hd, (h + 1) * hd)
        s = lax.dot_general(q_ref[0, :, sl], k_ref[0, :, sl], (((1,), (1,)), ((), ())),
                            preferred_element_type=F32) * scale
        p = jnp.exp(s - jnp.max(s, axis=-1, keepdims=True))
        p = p / jnp.sum(p, axis=-1, keepdims=True)
        ob_ref[:, sl] = jnp.dot(p.astype(BF16), v_ref[0, :, sl], preferred_element_type=F32).astype(BF16)
    o_ref[0] = x_ref[0] + jnp.dot(ob_ref[...], wo_ref[...], preferred_element_type=F32)


def _xattn(q, kv, layer, wo, x, tm):
    b, s, d = x.shape
    mem = kv.shape[2]
    tm = min(tm, s)
    kern = functools.partial(_xattn_kernel, heads=XA_HEADS, scale=(d // XA_HEADS) ** -0.5)
    return pl.pallas_call(
        kern,
        grid=(b, s // tm),
        in_specs=[
            pl.BlockSpec((1, tm, d), lambda i, j: (i, j, 0)),
            pl.BlockSpec((None, 1, mem, d), lambda i, j: (layer, i, 0, 0)),
            pl.BlockSpec((None, 1, mem, d), lambda i, j: (layer, i, 0, 1)),
            pl.BlockSpec((None, d, d), lambda i, j: (layer, 0, 0)),
            pl.BlockSpec((1, tm, d), lambda i, j: (i, j, 0)),
        ],
        out_specs=pl.BlockSpec((1, tm, d), lambda i, j: (i, j, 0)),
        out_shape=jax.ShapeDtypeStruct((b, s, d), F32),
        scratch_shapes=[pltpu.VMEM((tm, d), BF16)],
        compiler_params=_cparams("parallel", "arbitrary"),
        name="xattn",
    )(q, kv, kv, wo, x)


def _lru_kernel(xa_ref, ga_ref, cw_ref, cb_ref, wri_ref, br_ref, bi_ref, sp_ref, o_ref, buf_ref, h_ref):
    @pl.when(pl.program_id(1) == 0)
    def _():
        buf_ref[0:SUBLANES, :] = jnp.zeros((SUBLANES, buf_ref.shape[1]), F32)
        h_ref[...] = jnp.zeros(h_ref.shape, F32)

    xc = _causal_conv(xa_ref[0], buf_ref, cw_ref, cb_ref)
    t = xc.shape[0]
    for blk in range(xc.shape[1] // LRU_BLOCK):
        sl = slice(blk * LRU_BLOCK, (blk + 1) * LRU_BLOCK)
        xb = xc[:, sl]
        pre = jnp.dot(xb.astype(BF16), wri_ref[blk], preferred_element_type=F32)
        r_gate = _sigmoid(pre[:, :LRU_BLOCK] + br_ref[:, sl])
        i_gate = _sigmoid(pre[:, LRU_BLOCK:] + bi_ref[:, sl])
        log_a = (-LRU_C) * r_gate * sp_ref[:, sl]
        a = jnp.exp(log_a)
        mult = _sqrt_nonneg(-jnp.tanh(log_a) * (a * a + 1.0))
        h = _linear_scan_rows(a, mult * i_gate * xb, h_ref[0:1, sl])
        h_ref[0:1, sl] = h[t - 1:t, :]
        o_ref[0, :, sl] = (_gelu_tanh(ga_ref[0, :, sl]) * h).astype(BF16)


def _lru(proj, conv_w, conv_b, w_ri, b_r, b_i, sp, tb):
    b, s, _ = proj.shape
    w = conv_w.shape[1]
    tb = min(tb, s)
    vec = lambda: pl.BlockSpec((1, w), lambda i, j: (0, 0))
    return pl.pallas_call(
        _lru_kernel,
        grid=(b, s // tb),
        in_specs=[
            pl.BlockSpec((1, tb, w), lambda i, j: (i, j, 0)),
            pl.BlockSpec((1, tb, w), lambda i, j: (i, j, 1)),
            pl.BlockSpec((CONV_K, w), lambda i, j: (0, 0)),
            vec(),
            pl.BlockSpec(w_ri.shape, lambda i, j: (0, 0, 0)),
            vec(), vec(), vec(),
        ],
        out_specs=pl.BlockSpec((1, tb, w), lambda i, j: (i, j, 0)),
        out_shape=jax.ShapeDtypeStruct((b, s, w), BF16),
        scratch_shapes=[pltpu.VMEM((tb + SUBLANES, w), F32), pltpu.VMEM((SUBLANES, w), F32)],
        compiler_params=_cparams("parallel", "arbitrary"),
        name="lru",
    )(proj, proj, conv_w, conv_b.reshape(1, w), w_ri, b_r.reshape(1, w), b_i.reshape(1, w), sp.reshape(1, w))


def _hgrn_chunk(q, f, v, lb, st):
    c, dk = q.shape
    nsub = c // HGRN_SUB
    qh = _silu(q)
    fg = lb + (1.0 - lb) * _sigmoid(f)
    kh = 1.0 - fg
    g = jnp.log(fg)
    cum = _cumsum(g, 0)
    ex = cum - g
    cum3 = cum.reshape(nsub, HGRN_SUB, dk)
    ex3 = ex.reshape(nsub, HGRN_SUB, dk)
    base3 = jnp.broadcast_to(ex3[:, 0:1, :], cum3.shape)
    q3 = qh.reshape(nsub, HGRN_SUB, dk)
    k3 = kh.reshape(nsub, HGRN_SUB, dk)
    v3 = v.reshape(nsub, HGRN_SUB, dk)
    vb = v.astype(BF16)

    row3 = lax.broadcasted_iota(jnp.int32, cum3.shape, 1)
    terms = []
    for s in range(HGRN_SUB):
        diff = cum3 - cum3[:, s:s + 1, :]
        dec = jnp.exp(jnp.where(row3 >= s, diff, NEG_INF))
        terms.append((dec * q3 * k3[:, s:s + 1, :]).reshape(c, dk))
    stacked = jnp.concatenate(terms, axis=0).astype(BF16)
    ones = jnp.ones((dk, dk), BF16)
    summed = jnp.dot(stacked, ones, preferred_element_type=F32)
    o3 = jnp.zeros(cum3.shape, F32)
    for s in range(HGRN_SUB):
        o3 = o3 + summed[s * c:(s + 1) * c, :].reshape(nsub, HGRN_SUB, dk) * v3[:, s:s + 1, :]
    o = o3.reshape(c, dk)

    q_loc = (qh * jnp.exp(cum - base3.reshape(c, dk))).astype(BF16)
    pad = jnp.zeros((LANES, dk), F32)
    k_parts = []
    for i in range(1, nsub):
        n = i * HGRN_SUB
        base_i = ex[n:n + 1, :]
        k_parts.append(kh[0:n, :] * jnp.exp(base_i - cum[0:n, :]))
        k_parts.append(pad[0:LANES - n, :])
    k_hat = jnp.concatenate(k_parts, axis=0).astype(BF16)
    a_all = lax.dot_general(q_loc, k_hat, (((1,), (1,)), ((), ())), preferred_element_type=F32)
    a_rows = [jnp.zeros((HGRN_SUB, c), F32)]
    for i in range(1, nsub):
        a_rows.append(a_all[i * HGRN_SUB:(i + 1) * HGRN_SUB, (i - 1) * LANES:(i - 1) * LANES + c])
    a_off = jnp.concatenate(a_rows, axis=0).astype(BF16)
    o = o + jnp.dot(a_off, vb, preferred_element_type=F32)

    q_in = (qh * jnp.exp(cum)).astype(BF16)
    o = o + lax.dot_general(q_in, st.astype(BF16), (((1,), (1,)), ((), ())), preferred_element_type=F32)
    last = cum[c - 1:c, :]
    k_out = (kh * jnp.exp(last - cum)).astype(BF16)
    st_new = st * jnp.exp(last) + lax.dot_general(vb, k_out, (((0,), (0,)), ((), ())),
                                                  preferred_element_type=F32)
    return o, st_new


def _hgrn_kernel(q_ref, f_ref, v_ref, gb_ref, lb_ref, gn_ref, ya_ref, wa_ref, wb_ref, res_ref,
                 o_ref, st_ref, yb_ref, *, row_blocks):
    k = pl.program_id(1)
    h = pl.program_id(2)
    slot = lax.rem(k, 2)
    heads = yb_ref.shape[1]

    def scan(first):
        lb = lb_ref[h]
        gn = gn_ref[h]
        st = jnp.zeros(st_ref.shape[1:], F32) if first else st_ref[h]
        for ci in range(q_ref.shape[1] // HGRN_CHUNK):
            rows = slice(ci * HGRN_CHUNK, (ci + 1) * HGRN_CHUNK)
            o, st = _hgrn_chunk(q_ref[0, rows, :], f_ref[0, rows, :], v_ref[0, rows, :], lb, st)
            o = o * _rms_scale(o) * gn
            yb_ref[slot, h, rows, :] = (o * _silu(gb_ref[0, rows, :])).astype(BF16)
        st_ref[h] = st

    def project():
        yb = jnp.concatenate([yb_ref[1 - slot, hh] for hh in range(heads)], axis=1)
        acc = jnp.dot(ya_ref[0], wa_ref[...], preferred_element_type=F32)
        o_ref[0] = res_ref[0] + acc + jnp.dot(yb, wb_ref[...], preferred_element_type=F32)

    @pl.when(k == 0)
    def _():
        scan(True)

    @pl.when((k > 0) & (k < row_blocks))
    def _():
        project()
        scan(False)

    @pl.when(k == row_blocks)
    def _():
        project()


def _hgrn(proj, lower_bound, head_norm, ya, w_out, layer, res, col0, heads, tb):
    b, s, _ = proj.shape
    d = res.shape[2]
    tb = min(tb, s)
    nk = s // tb
    w = heads * HGRN_DK
    dg = d // heads
    c0 = col0 // HGRN_DK
    cur = lambda k: jnp.minimum(k, nk - 1)
    prev = lambda k: jnp.maximum(k - 1, 0)
    out_col = lambda k, h: jnp.where(k == 0, 0, h)
    part = lambda p: pl.BlockSpec((1, tb, HGRN_DK), lambda i, k, h: (i, cur(k), c0 + p * heads + h))
    vec = lambda: pl.BlockSpec((heads, 1, HGRN_DK), lambda i, k, h: (0, 0, 0))
    return pl.pallas_call(
        functools.partial(_hgrn_kernel, row_blocks=nk),
        grid=(b, nk + 1, heads),
        in_specs=[
            part(0), part(1), part(2), part(3), vec(), vec(),
            pl.BlockSpec((1, tb, w), lambda i, k, h: (i, prev(k), 0)),
            pl.BlockSpec((None, None, w, dg), lambda i, k, h: (layer, h, 0, 0)),
            pl.BlockSpec((None, None, w, dg), lambda i, k, h: (layer, h, 1, 0)),
            pl.BlockSpec((1, tb, dg), lambda i, k, h: (i, prev(k), out_col(k, h))),
        ],
        out_specs=pl.BlockSpec((1, tb, dg), lambda i, k, h: (i, prev(k), out_col(k, h))),
        out_shape=jax.ShapeDtypeStruct((b, s, d), F32),
        scratch_shapes=[
            pltpu.VMEM((heads, HGRN_DK, HGRN_DK), F32),
            pltpu.VMEM((2, heads, tb, HGRN_DK), BF16),
        ],
        compiler_params=_cparams("arbitrary", "arbitrary", "arbitrary"),
        name="hgrn",
    )(proj, proj, proj, proj, lower_bound.reshape(heads, 1, HGRN_DK), head_norm.reshape(heads, 1, HGRN_DK),
      ya, w_out, w_out, res)


def _split3(x):
    hi = x.astype(BF16).astype(F32)
    rem = x - hi
    mid = rem.astype(BF16).astype(F32)
    return hi, mid, rem - mid


def _ssd_scan_block(z_ref, x_ref, b_ref, c_ref, dtr_ref, conv_refs, dbr_ref, anr_ref, dsk_ref, nw_ref,
                    tri_ref, sel_ref, st_ref, y_ref, buf_refs, tail_refs, first):
    for buf_ref, tail_ref in zip(buf_refs, tail_refs):
        if first:
            buf_ref[0:SUBLANES, :] = jnp.zeros((SUBLANES, buf_ref.shape[1]), F32)
        else:
            buf_ref[0:SUBLANES, :] = tail_ref[...]
    if first:
        st_ref[...] = jnp.zeros(st_ref.shape, F32)

    tb = x_ref.shape[1]
    t = tri_ref.shape[0]
    n = st_ref.shape[0]
    hpg = dbr_ref.shape[0]
    (cwx_ref, cbx_ref, cwb_ref, cbb_ref, cwc_ref, cbc_ref) = conv_refs
    xs_all = _silu(_causal_conv(x_ref[0], buf_refs[0], cwx_ref, cbx_ref))
    bm_all = _silu(_causal_conv(b_ref[0], buf_refs[1], cwb_ref, cbb_ref))
    cm_all = _silu(_causal_conv(c_ref[0], buf_refs[2], cwc_ref, cbc_ref))
    for buf_ref, tail_ref in zip(buf_refs, tail_refs):
        tail_ref[...] = buf_ref[0:SUBLANES, :]
    dtr_all = _softplus(dtr_ref[0, 0] + dbr_ref[...])
    causal = (lax.broadcasted_iota(jnp.int32, (t, t), 0) >= lax.broadcasted_iota(jnp.int32, (t, t), 1))
    low_t = lax.broadcasted_iota(jnp.int32, (t, LANES), 1) < SSD_HEADDIM
    low_n = lax.broadcasted_iota(jnp.int32, (n, LANES), 1) < SSD_HEADDIM
    pad = jnp.zeros((sel_ref.shape[0] - 3 * hpg, t), F32)

    for ci in range(tb // t):
        rows = slice(ci * t, (ci + 1) * t)
        xs, bm, cm, dtr = xs_all[rows], bm_all[rows], cm_all[rows], dtr_all[:, rows]
        xsb = xs.astype(BF16)
        dta = dtr * anr_ref[...]
        parts = jnp.concatenate(list(_split3(dta)) + [jnp.zeros_like(dta)], axis=0).astype(BF16)
        c3 = jnp.dot(parts, tri_ref[...], preferred_element_type=F32)
        cum = c3[0:hpg] + c3[hpg:2 * hpg] + c3[2 * hpg:3 * hpg]
        cols = jnp.concatenate(list(_split3(cum)) + [pad], axis=0).astype(BF16)
        bc_all = lax.dot_general(cols, sel_ref[...], (((0,), (0,)), ((), ())), preferred_element_type=F32)
        row_term = cum - jnp.log(dtr)
        last = jnp.broadcast_to(cum[:, t - 1:t], cum.shape)
        row_scale = jnp.exp(last - cum) * dtr
        e_last = jnp.exp(bc_all[t - 1:t, :])
        cb = lax.dot_general(cm.astype(BF16), bm.astype(BF16), (((1,), (1,)), ((), ())),
                             preferred_element_type=F32)
        bt = bm.T

        for pair in range(hpg // 2):
            pc = slice(pair * LANES, (pair + 1) * LANES)
            rhs = jnp.concatenate([xsb[:, pc], st_ref[:, pc].astype(BF16)], axis=0)
            lhs, zl = [], []
            for h in (2 * pair, 2 * pair + 1):
                bc = bc_all[:, h * LANES:(h + 1) * LANES]
                dec = jnp.exp(jnp.where(causal, bc - row_term[h:h + 1, :], NEG_INF))
                lhs.append(jnp.concatenate([cb * dec, cm * jnp.exp(bc)], axis=1).astype(BF16))
                zl.append((bt * row_scale[h:h + 1, :]).astype(BF16))
            y2 = jnp.dot(jnp.concatenate(lhs, axis=0), rhs, preferred_element_type=F32)
            y_ref[rows, pc] = jnp.where(low_t, y2[:t], y2[t:]) + dsk_ref[:, pc] * xs[:, pc]
            z2 = jnp.dot(jnp.concatenate(zl, axis=0), xsb[:, pc], preferred_element_type=F32)
            e_pair = jnp.where(low_n[0:1], e_last[:, 2 * pair * LANES:(2 * pair + 1) * LANES],
                               e_last[:, (2 * pair + 1) * LANES:(2 * pair + 2) * LANES])
            st_ref[:, pc] = st_ref[:, pc] * e_pair + jnp.where(low_n, z2[:n], z2[n:])

    y = y_ref[...] * _silu(z_ref[0])
    return (y * _rms_scale(y) * nw_ref[...]).astype(BF16)


def _ssd_kernel(z_ref, x_ref, b_ref, c_ref, dtr_ref,
                cwx_ref, cbx_ref, cwb_ref, cbb_ref, cwc_ref, cbc_ref,
                dbr_ref, anr_ref, dsk_ref, nw_ref, tri_ref, sel_ref, wo_ref, res_ref,
                o_ref, st_ref, y_ref, yn_ref, bufx_ref, bufb_ref, bufc_ref, tailx_ref, tailb_ref, tailc_ref,
                *, row_blocks):
    k = pl.program_id(1)
    g = pl.program_id(2)
    slot = lax.rem(k, 2)
    groups = yn_ref.shape[1]

    def scan(first):
        yn_ref[slot, g] = _ssd_scan_block(
            z_ref, x_ref, b_ref, c_ref, dtr_ref,
            tuple(r.at[g] for r in (cwx_ref, cbx_ref, cwb_ref, cbb_ref, cwc_ref, cbc_ref)),
            dbr_ref.at[g], anr_ref.at[g], dsk_ref.at[g], nw_ref.at[g], tri_ref, sel_ref, st_ref.at[g], y_ref,
            (bufx_ref, bufb_ref, bufc_ref), (tailx_ref.at[g], tailb_ref.at[g], tailc_ref.at[g]), first)

    def project():
        lhs = jnp.concatenate([yn_ref[1 - slot, gg] for gg in range(groups)], axis=1)
        o_ref[0] = res_ref[0] + jnp.dot(lhs, wo_ref[...], preferred_element_type=F32)

    @pl.when(k == 0)
    def _():
        scan(True)

    @pl.when((k > 0) & (k < row_blocks))
    def _():
        project()
        scan(False)

    @pl.when(k == row_blocks)
    def _():
        project()


def _ssd(proj, dt_raw, conv_w, conv_b, dt_bias, a_neg, d_skip, norm_w, w_out, layer, res, inner, groups):
    b, s, _ = proj.shape
    d = res.shape[2]
    dg = d // groups
    t = min(SSD_CHUNK, s)
    tb = min(SSD_TB, s)
    gw = inner // groups
    n = SSD_STATE
    hpg = SSD_HPG
    assert gw == hpg * SSD_HEADDIM and 2 * SSD_HEADDIM == LANES and tb % t == 0
    dt_row = dt_raw.reshape(b, s, groups, hpg).transpose(0, 2, 3, 1)
    db_r = dt_bias.reshape(groups, hpg, 1)
    an_r = a_neg.reshape(groups, hpg, 1)
    dsk = jnp.repeat(d_skip, SSD_HEADDIM).reshape(groups, 1, gw)
    nw = norm_w.reshape(groups, 1, gw)
    by_group = lambda a, lo, w: (a[:, lo:lo + groups * w].reshape(a.shape[0], groups, w).transpose(1, 0, 2))
    conv_b2 = conv_b.reshape(1, -1)
    conv_parts = []
    for lo, w in ((0, gw), (inner, n), (inner + groups * n, n)):
        conv_parts += [by_group(conv_w, lo, w), by_group(conv_b2, lo, w)]
    tri = (jnp.arange(t)[:, None] <= jnp.arange(t)[None, :]).astype(BF16)
    krow = jnp.arange(LANES)[:, None]
    sel = ((krow < 3 * hpg) & ((krow % hpg) == (jnp.arange(hpg * LANES)[None, :] // LANES))).astype(BF16)
    xoff, boff, coff = inner // gw, (2 * inner) // n, (2 * inner + groups * n) // n
    nk = s // tb
    cur = lambda k: jnp.minimum(k, nk - 1)
    prev = lambda k: jnp.maximum(k - 1, 0)
    out_col = lambda k, g: jnp.where(k == 0, 0, g)
    const = lambda a: pl.BlockSpec(a.shape, lambda i, k, g: (0,) * a.ndim)
    small = conv_parts + [db_r, an_r, dsk, nw, tri, sel]
    return pl.pallas_call(
        functools.partial(_ssd_kernel, row_blocks=nk),
        grid=(b, nk + 1, groups),
        in_specs=[
            pl.BlockSpec((1, tb, gw), lambda i, k, g: (i, cur(k), g)),
            pl.BlockSpec((1, tb, gw), lambda i, k, g: (i, cur(k), xoff + g)),
            pl.BlockSpec((1, tb, n), lambda i, k, g: (i, cur(k), boff + g)),
            pl.BlockSpec((1, tb, n), lambda i, k, g: (i, cur(k), coff + g)),
            pl.BlockSpec((1, 1, hpg, tb), lambda i, k, g: (i, g, 0, cur(k))),
            *[const(a) for a in small],
            pl.BlockSpec((None, None, inner, dg), lambda i, k, g: (layer, g, 0, 0)),
            pl.BlockSpec((1, tb, dg), lambda i, k, g: (i, prev(k), out_col(k, g))),
        ],
        out_specs=pl.BlockSpec((1, tb, dg), lambda i, k, g: (i, prev(k), out_col(k, g))),
        out_shape=jax.ShapeDtypeStruct((b, s, d), F32),
        scratch_shapes=[
            pltpu.VMEM((groups, n, gw), F32),
            pltpu.VMEM((tb, gw), F32),
            pltpu.VMEM((2, groups, tb, gw), BF16),
            pltpu.VMEM((tb + SUBLANES, gw), F32),
            pltpu.VMEM((tb + SUBLANES, n), F32),
            pltpu.VMEM((tb + SUBLANES, n), F32),
            pltpu.VMEM((groups, SUBLANES, gw), F32),
            pltpu.VMEM((groups, SUBLANES, n), F32),
            pltpu.VMEM((groups, SUBLANES, n), F32),
        ],
        compiler_params=_cparams("arbitrary", "arbitrary", "arbitrary"),
        name="ssd",
    )(proj, proj, proj, proj, dt_row, *small, w_out, res)


TM = 1024
TN = 1024
SQ_TM = 512
SQ_TN = 2048
FFN_TH = 512
XA_TM = 512
LRU_TB = 256
HGRN_TB = 512
SSD_TB = 512


def kernel(x, mem, norm_mix, norm_xattn, norm_ffn, norm_mem, norm_final, ab_w_in, ab_w_out, lru_conv_w, lru_conv_b, lru_w_r, lru_b_r, lru_w_i, lru_b_i, lru_lambda, hgrn_lower_bounds, hgrn_norm, ssd_w_in, ssd_w_out, ssd_conv_w, ssd_conv_b, ssd_dt_bias, ssd_a_log, ssd_d, ssd_norm, xa_w_q, xa_w_kv, xa_w_o, ffn_w_gate, ffn_w_up, ffn_w_down):
    bsz, seq, d = x.shape
    depth = norm_mix.shape[0]
    m = bsz * seq
    lru_w = lru_conv_w.shape[2]
    hgrn_w = hgrn_norm.shape[1]
    ssd_heads = ssd_a_log.shape[1]
    ssd_inner = ssd_norm.shape[1]
    ssd_groups = ssd_heads // SSD_HPG
    ssd_main = ssd_w_in.shape[2] - ssd_heads

    sm = jax.nn.softmax(hgrn_lower_bounds.astype(F32), axis=0)
    lower_bounds = jnp.cumsum(sm, axis=0) - sm[0]
    lru_sp = jax.nn.softplus(-lru_lambda.astype(F32))
    ssd_a_neg = -jnp.exp(ssd_a_log.astype(F32))
    lru_w_ri = jnp.concatenate([lru_w_r, lru_w_i], axis=-1).astype(BF16)
    ab_in_b = _cast_bf16(ab_w_in)
    by_cols = lambda w, g: w.reshape(w.shape[0], w.shape[1], g, w.shape[2] // g).transpose(0, 2, 1, 3)
    ab_out_b = by_cols(_cast_bf16(ab_w_out), hgrn_w // HGRN_DK)
    ssd_in_b, ssd_dt_b = _cast_bf16(ssd_w_in, (ssd_main, ssd_heads))
    ssd_out_b = by_cols(_cast_bf16(ssd_w_out), ssd_groups)
    wq_b = _cast_bf16(xa_w_q)
    wkv_b = _cast_bf16(xa_w_kv)
    wo_b = _cast_bf16(xa_w_o)
    wg_b = _cast_bf16(ffn_w_gate)
    wu_b = _cast_bf16(ffn_w_up)
    wd_b = _cast_bf16(ffn_w_down)

    x2 = x.reshape(m, d)
    mem_len = mem.shape[1]
    kv = _kv_proj(mem.reshape(bsz * mem_len, d), norm_mem, wkv_b, TN).reshape(depth, bsz, mem_len, 2 * d)

    for layer in range(depth):
        if layer % 2 == 0:
            e = layer // 2
            proj = _norm_matmul(x2, norm_mix[layer], ab_in_b, e, F32, TM, TN, "ab_in_proj")
            proj = proj.reshape(bsz, seq, -1)
            ya = _lru(proj, lru_conv_w[e], lru_conv_b[e], lru_w_ri[e], lru_b_r[e], lru_b_i[e], lru_sp[e], LRU_TB)
            x2 = _hgrn(proj, lower_bounds[e], hgrn_norm[e], ya, ab_out_b, e, x2.reshape(bsz, seq, d),
                       2 * lru_w, hgrn_w // HGRN_DK, HGRN_TB).reshape(m, d)
        else:
            o = layer // 2
            proj, dt_raw = _norm_matmul_side(x2, norm_mix[layer], ssd_in_b, ssd_dt_b, o, TM, TN, "ssd_in_proj")
            x2 = _ssd(proj.reshape(bsz, seq, ssd_main), dt_raw.reshape(bsz, seq, ssd_heads),
                      ssd_conv_w[o], ssd_conv_b[o], ssd_dt_bias[o], ssd_a_neg[o], ssd_d[o], ssd_norm[o],
                      ssd_out_b, o, x2.reshape(bsz, seq, d), ssd_inner, ssd_groups).reshape(m, d)
        q = _norm_matmul(x2, norm_xattn[layer], wq_b, layer, BF16, SQ_TM, SQ_TN, "xa_q_proj")
        x2 = _xattn(q.reshape(bsz, seq, d), kv, layer, wo_b, x2.reshape(bsz, seq, d), XA_TM)
        x2 = x2.reshape(m, d)
        g_out = norm_final if layer == depth - 1 else None
        x2 = _ffn(x2, norm_ffn[layer], wg_b, wu_b, wd_b, layer, TM, FFN_TH, g_out)
    return x2.reshape(bsz, seq, d)
```

```python
import functools
import math

import jax
import jax.numpy as jnp
from jax import lax
from jax.experimental import pallas as pl
from jax.experimental.pallas import tpu as pltpu

F32 = jnp.float32
BF16 = jnp.bfloat16
EPS = 1e-6
NEG_INF = float("-inf")

VMEM_LIMIT_BYTES = 56 * 1024 * 1024
CAST_BLOCK_BYTES = 4 * 1024 * 1024
SUBLANES = 8
LANES = 128

CONV_K = 4
LRU_BLOCK = 128
LRU_C = 8.0
HGRN_DK = 128
HGRN_CHUNK = 64
HGRN_SUB = SUBLANES
SSD_HEADDIM = 64
SSD_HPG = 8
SSD_STATE = 128
SSD_CHUNK = 128
XA_HEADS = 4
MEM_LEN = 256


def _cparams(*sem):
    return pltpu.CompilerParams(dimension_semantics=sem, vmem_limit_bytes=VMEM_LIMIT_BYTES)


def _rms_scale(x):
    return lax.rsqrt(jnp.mean(x * x, axis=-1, keepdims=True) + EPS)


def _sigmoid(x):
    return 0.5 * (1.0 + jnp.tanh(0.5 * x))


def _silu(x):
    return x * _sigmoid(x)


def _gelu_tanh(x):
    c = math.sqrt(2.0 / math.pi)
    return x * (0.5 * (1.0 + jnp.tanh(c * (x + 0.044715 * (x * x * x)))))


def _softplus(x):
    return jnp.maximum(x, 0.0) + jnp.log1p(jnp.exp(-jnp.abs(x)))


def _linear_scan_rows(a, b, h0):
    t, c = a.shape
    groups = t // SUBLANES
    a3 = a.reshape(groups, SUBLANES, c)
    b3 = b.reshape(groups, SUBLANES, c)
    row = lax.broadcasted_iota(jnp.int32, a3.shape, 1)
    s = 1
    while s < SUBLANES:
        keep = row >= s
        a_sh = pltpu.roll(a3, s, 1)
        b_sh = pltpu.roll(b3, s, 1)
        b3 = jnp.where(keep, a3 * b_sh + b3, b3)
        a3 = jnp.where(keep, a3 * a_sh, a3)
        s *= 2
    out = []
    carry = h0
    for g in range(groups):
        h = a3[g] * carry + b3[g]
        out.append(h)
        carry = h[SUBLANES - 1:SUBLANES, :]
    return jnp.concatenate(out, axis=0)


def _sqrt_nonneg(y):
    return jnp.where(y > 0.0, y * lax.rsqrt(y), 0.0)


def _cumsum(x, axis):
    n = x.shape[axis]
    idx = lax.broadcasted_iota(jnp.int32, x.shape, axis)
    s = 1
    while s < n:
        x = jnp.where(idx >= s, x + pltpu.roll(x, s, axis), x)
        s *= 2
    return x


def _causal_conv(u, buf_ref, w_ref, b_ref):
    t = u.shape[0]
    buf_ref[SUBLANES:SUBLANES + t, :] = u
    out = b_ref[...] + w_ref[3:4, :] * u
    for j in range(1, CONV_K):
        out = out + w_ref[CONV_K - 1 - j:CONV_K - j, :] * buf_ref[SUBLANES - j:SUBLANES - j + t, :]
    buf_ref[0:SUBLANES, :] = u[t - SUBLANES:t, :]
    return out


def _norm_matmul_kernel(x_ref, g_ref, w_ref, o_ref, xn_ref):
    @pl.when(pl.program_id(1) == 0)
    def _():
        x = x_ref[...]
        xn_ref[...] = (x * _rms_scale(x) * g_ref[...]).astype(BF16)

    o_ref[...] = jnp.dot(xn_ref[...], w_ref[...], preferred_element_type=F32).astype(o_ref.dtype)


def _norm_matmul(x, g, w, layer, out_dtype, tm, tn, name):
    m, k = x.shape
    n = w.shape[2]
    tm = min(tm, m)
    tn = min(tn, n)
    return pl.pallas_call(
        _norm_matmul_kernel,
        grid=(m // tm, n // tn),
        in_specs=[
            pl.BlockSpec((tm, k), lambda i, j: (i, 0)),
            pl.BlockSpec((1, k), lambda i, j: (0, 0)),
            pl.BlockSpec((None, k, tn), lambda i, j: (layer, 0, j)),
        ],
        out_specs=pl.BlockSpec((tm, tn), lambda i, j: (i, j)),
        out_shape=jax.ShapeDtypeStruct((m, n), out_dtype),
        scratch_shapes=[pltpu.VMEM((tm, k), BF16)],
        compiler_params=_cparams("parallel", "arbitrary"),
        name=name,
    )(x, g.reshape(1, k), w)


def _norm_matmul_side_kernel(x_ref, g_ref, w_ref, ws_ref, o_ref, os_ref, xn_ref):
    @pl.when(pl.program_id(1) == 0)
    def _():
        x = x_ref[...]
        xn = (x * _rms_scale(x) * g_ref[...]).astype(BF16)
        xn_ref[...] = xn
        os_ref[...] = jnp.dot(xn, ws_ref[...], preferred_element_type=F32)

    o_ref[...] = jnp.dot(xn_ref[...], w_ref[...], preferred_element_type=F32).astype(o_ref.dtype)


def _norm_matmul_side(x, g, w, w_side, layer, tm, tn, name):
    m, k = x.shape
    n = w.shape[2]
    ns = w_side.shape[2]
    tm = min(tm, m)
    tn = min(tn, n)
    return pl.pallas_call(
        _norm_matmul_side_kernel,
        grid=(m // tm, n // tn),
        in_specs=[
            pl.BlockSpec((tm, k), lambda i, j: (i, 0)),
            pl.BlockSpec((1, k), lambda i, j: (0, 0)),
            pl.BlockSpec((None, k, tn), lambda i, j: (layer, 0, j)),
            pl.BlockSpec((None, k, ns), lambda i, j: (layer, 0, 0)),
        ],
        out_specs=[pl.BlockSpec((tm, tn), lambda i, j: (i, j)), pl.BlockSpec((tm, ns), lambda i, j: (i, 0))],
        out_shape=[jax.ShapeDtypeStruct((m, n), F32), jax.ShapeDtypeStruct((m, ns), F32)],
        scratch_shapes=[pltpu.VMEM((tm, k), BF16)],
        compiler_params=_cparams("parallel", "arbitrary"),
        name=name,
    )(x, g.reshape(1, k), w, w_side)


def _kv_proj_kernel(x_ref, g_ref, w_ref, o_ref, xn_ref):
    @pl.when((pl.program_id(0) == 0) & (pl.program_id(1) == 0))
    def _():
        x = x_ref[...]
        xn_ref[...] = (x * _rms_scale(x) * g_ref[...]).astype(BF16)

    o_ref[...] = jnp.dot(xn_ref[...], w_ref[...].astype(BF16), preferred_element_type=F32).astype(o_ref.dtype)


def _kv_proj(x, g, w, tn):
    m, k = x.shape
    nl, _, n = w.shape
    return pl.pallas_call(
        _kv_proj_kernel,
        grid=(nl, n // tn),
        in_specs=[
            pl.BlockSpec((m, k), lambda l, j: (0, 0)),
            pl.BlockSpec((1, k), lambda l, j: (0, 0)),
            pl.BlockSpec((None, k, tn), lambda l, j: (l, 0, j)),
        ],
        out_specs=pl.BlockSpec((None, m, tn), lambda l, j: (l, 0, j)),
        out_shape=jax.ShapeDtypeStruct((nl, m, n), BF16),
        scratch_shapes=[pltpu.VMEM((m, k), BF16)],
        compiler_params=_cparams("arbitrary", "arbitrary"),
        name="kv_proj",
    )(x, g.reshape(1, k), w)


def _cast_kernel(x_ref, *o_refs):
    off = 0
    for o_ref in o_refs:
        width = o_ref.shape[-1]
        o_ref[...] = x_ref[:, off:off + width].astype(BF16)
        off += width


def _cast_bf16(w, splits=None):
    nl, k, n = w.shape
    splits = splits or (n,)
    assert sum(splits) == n
    tk = SUBLANES
    while tk * 2 <= k and k % (tk * 2) == 0 and tk * 2 * n * 4 <= CAST_BLOCK_BYTES:
        tk *= 2
    outs = pl.pallas_call(
        _cast_kernel,
        grid=(nl, k // tk),
        in_specs=[pl.BlockSpec((None, tk, n), lambda l, i: (l, i, 0))],
        out_specs=[pl.BlockSpec((None, tk, s), lambda l, i: (l, i, 0)) for s in splits],
        out_shape=[jax.ShapeDtypeStruct((nl, k, s), BF16) for s in splits],
        compiler_params=_cparams("parallel", "parallel"),
        name="cast_bf16",
    )(w)
    return outs if len(splits) > 1 else outs[0]


def _ffn_kernel(x_ref, g_ref, go_ref, wg_ref, wu_ref, wd_ref, o_ref, xn_ref, *, out_norm):
    @pl.when(pl.program_id(1) == 0)
    def _():
        x = x_ref[...]
        xn_ref[...] = (x * _rms_scale(x) * g_ref[...]).astype(BF16)
        o_ref[...] = x

    xn = xn_ref[...]
    gate = jnp.dot(xn, wg_ref[...], preferred_element_type=F32)
    up = jnp.dot(xn, wu_ref[...], preferred_element_type=F32)
    hid = (_silu(gate) * up).astype(BF16)
    o_ref[...] += jnp.dot(hid, wd_ref[...], preferred_element_type=F32)

    if out_norm:
        @pl.when(pl.program_id(1) == pl.num_programs(1) - 1)
        def _():
            y = o_ref[...]
            o_ref[...] = y * _rms_scale(y) * go_ref[...]


def _ffn(x, g, wg, wu, wd, layer, tm, th, g_out=None):
    m, d = x.shape
    hdim = wg.shape[2]
    tm = min(tm, m)
    out_norm = g_out is not None
    g_out = g if g_out is None else g_out
    return pl.pallas_call(
        functools.partial(_ffn_kernel, out_norm=out_norm),
        grid=(m // tm, hdim // th),
        in_specs=[
            pl.BlockSpec((tm, d), lambda i, j: (i, 0)),
            pl.BlockSpec((1, d), lambda i, j: (0, 0)),
            pl.BlockSpec((1, d), lambda i, j: (0, 0)),
            pl.BlockSpec((None, d, th), lambda i, j: (layer, 0, j)),
            pl.BlockSpec((None, d, th), lambda i, j: (layer, 0, j)),
            pl.BlockSpec((None, th, d), lambda i, j: (layer, j, 0)),
        ],
        out_specs=pl.BlockSpec((tm, d), lambda i, j: (i, 0)),
        out_shape=jax.ShapeDtypeStruct((m, d), F32),
        scratch_shapes=[pltpu.VMEM((tm, d), BF16)],
        compiler_params=_cparams("parallel", "arbitrary"),
        name="ffn",
    )(x, g.reshape(1, d), g_out.reshape(1, d), wg, wu, wd)


def _xattn_kernel(q_ref, k_ref, v_ref, wo_ref, x_ref, o_ref, ob_ref, *, heads, scale):
    hd = q_ref.shape[2] // heads
    for h in range(heads):
        sl = slice(h * hd, (h + 1) * hd)
        s = lax.dot_general(q_ref[0, :, sl], k_ref[0, :, sl], (((1,), (1,)), ((), ())),
                            preferred_element_type=F32) * scale
        p = jnp.exp(s - jnp.max(s, axis=-1, keepdims=True))
        p = p / jnp.sum(p, axis=-1, keepdims=True)
        ob_ref[:, sl] = jnp.dot(p.astype(BF16), v_ref[0, :, sl], preferred_element_type=F32).astype(BF16)
    o_ref[0] = x_ref[0] + jnp.dot(ob_ref[...], wo_ref[...], preferred_element_type=F32)


def _xattn(q, kv, layer, wo, x, tm):
    b, s, d = x.shape
    mem = kv.shape[2]
    tm = min(tm, s)
    kern = functools.partial(_xattn_kernel, heads=XA_HEADS, scale=(d // XA_HEADS) ** -0.5)
    return pl.pallas_call(
        kern,
        grid=(b, s // tm),
        in_specs=[
            pl.BlockSpec((1, tm, d), lambda i, j: (i, j, 0)),
            pl.BlockSpec((None, 1, mem, d), lambda i, j: (layer, i, 0, 0)),
            pl.BlockSpec((None, 1, mem, d), lambda i, j: (layer, i, 0, 1)),
            pl.BlockSpec((None, d, d), lambda i, j: (layer, 0, 0)),
            pl.BlockSpec((1, tm, d), lambda i, j: (i, j, 0)),
        ],
        out_specs=pl.BlockSpec((1, tm, d), lambda i, j: (i, j, 0)),
        out_shape=jax.ShapeDtypeStruct((b, s, d), F32),
        scratch_shapes=[pltpu.VMEM((tm, d), BF16)],
        compiler_params=_cparams("parallel", "arbitrary"),
        name="xattn",
    )(q, kv, kv, wo, x)


def _lru_kernel(xa_ref, ga_ref, cw_ref, cb_ref, wri_ref, br_ref, bi_ref, sp_ref, o_ref, buf_ref, h_ref):
    @pl.when(pl.program_id(1) == 0)
    def _():
        buf_ref[0:SUBLANES, :] = jnp.zeros((SUBLANES, buf_ref.shape[1]), F32)
        h_ref[...] = jnp.zeros(h_ref.shape, F32)

    xc = _causal_conv(xa_ref[0], buf_ref, cw_ref, cb_ref)
    t = xc.shape[0]
    for blk in range(xc.shape[1] // LRU_BLOCK):
        sl = slice(blk * LRU_BLOCK, (blk + 1) * LRU_BLOCK)
        xb = xc[:, sl]
        pre = jnp.dot(xb.astype(BF16), wri_ref[blk], preferred_element_type=F32)
        r_gate = _sigmoid(pre[:, :LRU_BLOCK] + br_ref[:, sl])
        i_gate = _sigmoid(pre[:, LRU_BLOCK:] + bi_ref[:, sl])
        log_a = (-LRU_C) * r_gate * sp_ref[:, sl]
        a = jnp.exp(log_a)
        mult = _sqrt_nonneg(-jnp.tanh(log_a) * (a * a + 1.0))
        h = _linear_scan_rows(a, mult * i_gate * xb, h_ref[0:1, sl])
        h_ref[0:1, sl] = h[t - 1:t, :]
        o_ref[0, :, sl] = (_gelu_tanh(ga_ref[0, :, sl]) * h).astype(BF16)


def _lru(proj, conv_w, conv_b, w_ri, b_r, b_i, sp, tb):
    b, s, _ = proj.shape
    w = conv_w.shape[1]
    tb = min(tb, s)
    vec = lambda: pl.BlockSpec((1, w), lambda i, j: (0, 0))
    return pl.pallas_call(
        _lru_kernel,
        grid=(b, s // tb),
        in_specs=[
            pl.BlockSpec((1, tb, w), lambda i, j: (i, j, 0)),
            pl.BlockSpec((1, tb, w), lambda i, j: (i, j, 1)),
            pl.BlockSpec((CONV_K, w), lambda i, j: (0, 0)),
            vec(),
            pl.BlockSpec(w_ri.shape, lambda i, j: (0, 0, 0)),
            vec(), vec(), vec(),
        ],
        out_specs=pl.BlockSpec((1, tb, w), lambda i, j: (i, j, 0)),
        out_shape=jax.ShapeDtypeStruct((b, s, w), BF16),
        scratch_shapes=[pltpu.VMEM((tb + SUBLANES, w), F32), pltpu.VMEM((SUBLANES, w), F32)],
        compiler_params=_cparams("parallel", "arbitrary"),
        name="lru",
    )(proj, proj, conv_w, conv_b.reshape(1, w), w_ri, b_r.reshape(1, w), b_i.reshape(1, w), sp.reshape(1, w))


def _hgrn_chunk(q, f, v, lb, st):
    c, dk = q.shape
    nsub = c // HGRN_SUB
    qh = _silu(q)
    fg = lb + (1.0 - lb) * _sigmoid(f)
    kh = 1.0 - fg
    g = jnp.log(fg)
    cum = _cumsum(g, 0)
    ex = cum - g
    cum3 = cum.reshape(nsub, HGRN_SUB, dk)
    ex3 = ex.reshape(nsub, HGRN_SUB, dk)
    base3 = jnp.broadcast_to(ex3[:, 0:1, :], cum3.shape)
    q3 = qh.reshape(nsub, HGRN_SUB, dk)
    k3 = kh.reshape(nsub, HGRN_SUB, dk)
    v3 = v.reshape(nsub, HGRN_SUB, dk)
    vb = v.astype(BF16)

    row3 = lax.broadcasted_iota(jnp.int32, cum3.shape, 1)
    terms = []
    for s in range(HGRN_SUB):
        diff = cum3 - cum3[:, s:s + 1, :]
        dec = jnp.exp(jnp.where(row3 >= s, diff, NEG_INF))
        terms.append((dec * q3 * k3[:, s:s + 1, :]).reshape(c, dk))
    stacked = jnp.concatenate(terms, axis=0).astype(BF16)
    ones = jnp.ones((dk, dk), BF16)
    summed = jnp.dot(stacked, ones, preferred_element_type=F32)
    o3 = jnp.zeros(cum3.shape, F32)
    for s in range(HGRN_SUB):
        o3 = o3 + summed[s * c:(s + 1) * c, :].reshape(nsub, HGRN_SUB, dk) * v3[:, s:s + 1, :]
    o = o3.reshape(c, dk)

    q_loc = (qh * jnp.exp(cum - base3.reshape(c, dk))).astype(BF16)
    pad = jnp.zeros((LANES, dk), F32)
    k_parts = []
    for i in range(1, nsub):
        n = i * HGRN_SUB
        base_i = ex[n:n + 1, :]
        k_parts.append(kh[0:n, :] * jnp.exp(base_i - cum[0:n, :]))
        k_parts.append(pad[0:LANES - n, :])
    k_hat = jnp.concatenate(k_parts, axis=0).astype(BF16)
    a_all = lax.dot_general(q_loc, k_hat, (((1,), (1,)), ((), ())), preferred_element_type=F32)
    a_rows = [jnp.zeros((HGRN_SUB, c), F32)]
    for i in range(1, nsub):
        a_rows.append(a_all[i * HGRN_SUB:(i + 1) * HGRN_SUB, (i - 1) * LANES:(i - 1) * LANES + c])
    a_off = jnp.concatenate(a_rows, axis=0).astype(BF16)
    o = o + jnp.dot(a_off, vb, preferred_element_type=F32)

    q_in = (qh * jnp.exp(cum)).astype(BF16)
    o = o + lax.dot_general(q_in, st.astype(BF16), (((1,), (1,)), ((), ())), preferred_element_type=F32)
    last = cum[c - 1:c, :]
    k_out = (kh * jnp.exp(last - cum)).astype(BF16)
    st_new = st * jnp.exp(last) + lax.dot_general(vb, k_out, (((0,), (0,)), ((), ())),
                                                  preferred_element_type=F32)
    return o, st_new


def _hgrn_kernel(q_ref, f_ref, v_ref, gb_ref, lb_ref, gn_ref, o_ref, st_ref):
    @pl.when(pl.program_id(2) == 0)
    def _():
        st_ref[...] = jnp.zeros(st_ref.shape, F32)

    lb = lb_ref[...]
    st = st_ref[...]
    for ci in range(q_ref.shape[1] // HGRN_CHUNK):
        rows = slice(ci * HGRN_CHUNK, (ci + 1) * HGRN_CHUNK)
        o, st = _hgrn_chunk(q_ref[0, rows, :], f_ref[0, rows, :], v_ref[0, rows, :], lb, st)
        o = o * _rms_scale(o) * gn_ref[...]
        o_ref[0, rows, :] = (o * _silu(gb_ref[0, rows, :])).astype(BF16)
    st_ref[...] = st


def _hgrn(proj, lower_bound, head_norm, col0, heads, tb):
    b, s, _ = proj.shape
    tb = min(tb, s)
    c0 = col0 // HGRN_DK
    part = lambda p: pl.BlockSpec((1, tb, HGRN_DK), lambda i, h, j: (i, j, c0 + p * heads + h))
    vec = lambda: pl.BlockSpec((1, HGRN_DK), lambda i, h, j: (0, h))
    w = heads * HGRN_DK
    return pl.pallas_call(
        _hgrn_kernel,
        grid=(b, heads, s // tb),
        in_specs=[part(0), part(1), part(2), part(3), vec(), vec()],
        out_specs=pl.BlockSpec((1, tb, HGRN_DK), lambda i, h, j: (i, j, h)),
        out_shape=jax.ShapeDtypeStruct((b, s, w), BF16),
        scratch_shapes=[pltpu.VMEM((HGRN_DK, HGRN_DK), F32)],
        compiler_params=_cparams("parallel", "parallel", "arbitrary"),
        name="hgrn",
    )(proj, proj, proj, proj, lower_bound.reshape(1, w), head_norm.reshape(1, w))


def _out_proj2_kernel(a1_ref, a2_ref, w1_ref, w2_ref, r_ref, o_ref):
    acc = jnp.dot(a1_ref[...], w1_ref[...], preferred_element_type=F32)
    acc = acc + jnp.dot(a2_ref[...], w2_ref[...], preferred_element_type=F32)
    o_ref[...] = r_ref[...] + acc


def _out_proj2(a1, a2, w, layer, res, tm, tn):
    m, k1 = a1.shape
    assert a2.shape[1] == k1
    n = w.shape[2]
    tm = min(tm, m)
    tn = min(tn, n)
    return pl.pallas_call(
        _out_proj2_kernel,
        grid=(m // tm, n // tn),
        in_specs=[
            pl.BlockSpec((tm, k1), lambda i, j: (i, 0)),
            pl.BlockSpec((tm, k1), lambda i, j: (i, 0)),
            pl.BlockSpec((None, k1, tn), lambda i, j: (layer, 0, j)),
            pl.BlockSpec((None, k1, tn), lambda i, j: (layer, 1, j)),
            pl.BlockSpec((tm, tn), lambda i, j: (i, j)),
        ],
        out_specs=pl.BlockSpec((tm, tn), lambda i, j: (i, j)),
        out_shape=jax.ShapeDtypeStruct((m, n), F32),
        compiler_params=_cparams("parallel", "arbitrary"),
        name="ab_out_proj",
    )(a1, a2, w, w, res)


def _split3(x):
    hi = x.astype(BF16).astype(F32)
    rem = x - hi
    mid = rem.astype(BF16).astype(F32)
    return hi, mid, rem - mid


def _ssd_scan_block(z_ref, x_ref, b_ref, c_ref, dtr_ref, conv_refs, dbr_ref, anr_ref, dsk_ref, nw_ref,
                    tri_ref, sel_ref, st_ref, y_ref, buf_refs, tail_refs, first):
    for buf_ref, tail_ref in zip(buf_refs, tail_refs):
        if first:
            buf_ref[0:SUBLANES, :] = jnp.zeros((SUBLANES, buf_ref.shape[1]), F32)
        else:
            buf_ref[0:SUBLANES, :] = tail_ref[...]
    if first:
        st_ref[...] = jnp.zeros(st_ref.shape, F32)

    tb = x_ref.shape[1]
    t = tri_ref.shape[0]
    n = st_ref.shape[0]
    hpg = dbr_ref.shape[0]
    (cwx_ref, cbx_ref, cwb_ref, cbb_ref, cwc_ref, cbc_ref) = conv_refs
    xs_all = _silu(_causal_conv(x_ref[0], buf_refs[0], cwx_ref, cbx_ref))
    bm_all = _silu(_causal_conv(b_ref[0], buf_refs[1], cwb_ref, cbb_ref))
    cm_all = _silu(_causal_conv(c_ref[0], buf_refs[2], cwc_ref, cbc_ref))
    for buf_ref, tail_ref in zip(buf_refs, tail_refs):
        tail_ref[...] = buf_ref[0:SUBLANES, :]
    dtr_all = _softplus(dtr_ref[0, 0] + dbr_ref[...])
    causal = (lax.broadcasted_iota(jnp.int32, (t, t), 0) >= lax.broadcasted_iota(jnp.int32, (t, t), 1))
    low_t = lax.broadcasted_iota(jnp.int32, (t, LANES), 1) < SSD_HEADDIM
    low_n = lax.broadcasted_iota(jnp.int32, (n, LANES), 1) < SSD_HEADDIM
    pad = jnp.zeros((sel_ref.shape[0] - 3 * hpg, t), F32)

    for ci in range(tb // t):
        rows = slice(ci * t, (ci + 1) * t)
        xs, bm, cm, dtr = xs_all[rows], bm_all[rows], cm_all[rows], dtr_all[:, rows]
        xsb = xs.astype(BF16)
        dta = dtr * anr_ref[...]
        parts = jnp.concatenate(list(_split3(dta)) + [jnp.zeros_like(dta)], axis=0).astype(BF16)
        c3 = jnp.dot(parts, tri_ref[...], preferred_element_type=F32)
        cum = c3[0:hpg] + c3[hpg:2 * hpg] + c3[2 * hpg:3 * hpg]
        cols = jnp.concatenate(list(_split3(cum)) + [pad], axis=0).astype(BF16)
        bc_all = lax.dot_general(cols, sel_ref[...], (((0,), (0,)), ((), ())), preferred_element_type=F32)
        row_term = cum - jnp.log(dtr)
        last = jnp.broadcast_to(cum[:, t - 1:t], cum.shape)
        row_scale = jnp.exp(last - cum) * dtr
        e_last = jnp.exp(bc_all[t - 1:t, :])
        cb = lax.dot_general(cm.astype(BF16), bm.astype(BF16), (((1,), (1,)), ((), ())),
                             preferred_element_type=F32)
        bt = bm.T

        for pair in range(hpg // 2):
            pc = slice(pair * LANES, (pair + 1) * LANES)
            rhs = jnp.concatenate([xsb[:, pc], st_ref[:, pc].astype(BF16)], axis=0)
            lhs, zl = [], []
            for h in (2 * pair, 2 * pair + 1):
                bc = bc_all[:, h * LANES:(h + 1) * LANES]
                dec = jnp.exp(jnp.where(causal, bc - row_term[h:h + 1, :], NEG_INF))
                lhs.append(jnp.concatenate([cb * dec, cm * jnp.exp(bc)], axis=1).astype(BF16))
                zl.append((bt * row_scale[h:h + 1, :]).astype(BF16))
            y2 = jnp.dot(jnp.concatenate(lhs, axis=0), rhs, preferred_element_type=F32)
            y_ref[rows, pc] = jnp.where(low_t, y2[:t], y2[t:]) + dsk_ref[:, pc] * xs[:, pc]
            z2 = jnp.dot(jnp.concatenate(zl, axis=0), xsb[:, pc], preferred_element_type=F32)
            e_pair = jnp.where(low_n[0:1], e_last[:, 2 * pair * LANES:(2 * pair + 1) * LANES],
                               e_last[:, (2 * pair + 1) * LANES:(2 * pair + 2) * LANES])
            st_ref[:, pc] = st_ref[:, pc] * e_pair + jnp.where(low_n, z2[:n], z2[n:])

    y = y_ref[...] * _silu(z_ref[0])
    return (y * _rms_scale(y) * nw_ref[...]).astype(BF16)


def _ssd_kernel(z_ref, x_ref, b_ref, c_ref, dtr_ref,
                cwx_ref, cbx_ref, cwb_ref, cbb_ref, cwc_ref, cbc_ref,
                dbr_ref, anr_ref, dsk_ref, nw_ref, tri_ref, sel_ref, wo_ref, res_ref,
                o_ref, st_ref, y_ref, yn_ref, bufx_ref, bufb_ref, bufc_ref, tailx_ref, tailb_ref, tailc_ref,
                *, row_blocks):
    k = pl.program_id(1)
    g = pl.program_id(2)
    slot = lax.rem(k, 2)
    groups = yn_ref.shape[1]

    def scan(first):
        yn_ref[slot, g] = _ssd_scan_block(
            z_ref, x_ref, b_ref, c_ref, dtr_ref,
            tuple(r.at[g] for r in (cwx_ref, cbx_ref, cwb_ref, cbb_ref, cwc_ref, cbc_ref)),
            dbr_ref.at[g], anr_ref.at[g], dsk_ref.at[g], nw_ref.at[g], tri_ref, sel_ref, st_ref.at[g], y_ref,
            (bufx_ref, bufb_ref, bufc_ref), (tailx_ref.at[g], tailb_ref.at[g], tailc_ref.at[g]), first)

    def project():
        lhs = jnp.concatenate([yn_ref[1 - slot, gg] for gg in range(groups)], axis=1)
        o_ref[0] = res_ref[0] + jnp.dot(lhs, wo_ref[...], preferred_element_type=F32)

    @pl.when(k == 0)
    def _():
        scan(True)

    @pl.when((k > 0) & (k < row_blocks))
    def _():
        project()
        scan(False)

    @pl.when(k == row_blocks)
    def _():
        project()


def _ssd(proj, dt_raw, conv_w, conv_b, dt_bias, a_neg, d_skip, norm_w, w_out, layer, res, inner, groups):
    b, s, _ = proj.shape
    d = res.shape[2]
    dg = d // groups
    t = min(SSD_CHUNK, s)
    tb = min(SSD_TB, s)
    gw = inner // groups
    n = SSD_STATE
    hpg = SSD_HPG
    assert gw == hpg * SSD_HEADDIM and 2 * SSD_HEADDIM == LANES and tb % t == 0
    dt_row = dt_raw.reshape(b, s, groups, hpg).transpose(0, 2, 3, 1)
    db_r = dt_bias.reshape(groups, hpg, 1)
    an_r = a_neg.reshape(groups, hpg, 1)
    dsk = jnp.repeat(d_skip, SSD_HEADDIM).reshape(groups, 1, gw)
    nw = norm_w.reshape(groups, 1, gw)
    by_group = lambda a, lo, w: (a[:, lo:lo + groups * w].reshape(a.shape[0], groups, w).transpose(1, 0, 2))
    conv_b2 = conv_b.reshape(1, -1)
    conv_parts = []
    for lo, w in ((0, gw), (inner, n), (inner + groups * n, n)):
        conv_parts += [by_group(conv_w, lo, w), by_group(conv_b2, lo, w)]
    tri = (jnp.arange(t)[:, None] <= jnp.arange(t)[None, :]).astype(BF16)
    krow = jnp.arange(LANES)[:, None]
    sel = ((krow < 3 * hpg) & ((krow % hpg) == (jnp.arange(hpg * LANES)[None, :] // LANES))).astype(BF16)
    xoff, boff, coff = inner // gw, (2 * inner) // n, (2 * inner + groups * n) // n
    nk = s // tb
    cur = lambda k: jnp.minimum(k, nk - 1)
    prev = lambda k: jnp.maximum(k - 1, 0)
    out_col = lambda k, g: jnp.where(k == 0, 0, g)
    const = lambda a: pl.BlockSpec(a.shape, lambda i, k, g: (0,) * a.ndim)
    small = conv_parts + [db_r, an_r, dsk, nw, tri, sel]
    return pl.pallas_call(
        functools.partial(_ssd_kernel, row_blocks=nk),
        grid=(b, nk + 1, groups),
        in_specs=[
            pl.BlockSpec((1, tb, gw), lambda i, k, g: (i, cur(k), g)),
            pl.BlockSpec((1, tb, gw), lambda i, k, g: (i, cur(k), xoff + g)),
            pl.BlockSpec((1, tb, n), lambda i, k, g: (i, cur(k), boff + g)),
            pl.BlockSpec((1, tb, n), lambda i, k, g: (i, cur(k), coff + g)),
            pl.BlockSpec((1, 1, hpg, tb), lambda i, k, g: (i, g, 0, cur(k))),
            *[const(a) for a in small],
            pl.BlockSpec((None, inner, dg), lambda i, k, g: (layer, 0, g)),
            pl.BlockSpec((1, tb, dg), lambda i, k, g: (i, prev(k), out_col(k, g))),
        ],
        out_specs=pl.BlockSpec((1, tb, dg), lambda i, k, g: (i, prev(k), out_col(k, g))),
        out_shape=jax.ShapeDtypeStruct((b, s, d), F32),
        scratch_shapes=[
            pltpu.VMEM((groups, n, gw), F32),
            pltpu.VMEM((tb, gw), F32),
            pltpu.VMEM((2, groups, tb, gw), BF16),
            pltpu.VMEM((tb + SUBLANES, gw), F32),
            pltpu.VMEM((tb + SUBLANES, n), F32),
            pltpu.VMEM((tb + SUBLANES, n), F32),
            pltpu.VMEM((groups, SUBLANES, gw), F32),
            pltpu.VMEM((groups, SUBLANES, n), F32),
            pltpu.VMEM((groups, SUBLANES, n), F32),
        ],
        compiler_params=_cparams("arbitrary", "arbitrary", "arbitrary"),
        name="ssd",
    )(proj, proj, proj, proj, dt_row, *small, w_out, res)


TM = 1024
TN = 1024
SQ_TM = 512
SQ_TN = 2048
FFN_TH = 512
XA_TM = 512
LRU_TB = 256
HGRN_TB = 512
SSD_TB = 512


def kernel(x, mem, norm_mix, norm_xattn, norm_ffn, norm_mem, norm_final, ab_w_in, ab_w_out, lru_conv_w, lru_conv_b, lru_w_r, lru_b_r, lru_w_i, lru_b_i, lru_lambda, hgrn_lower_bounds, hgrn_norm, ssd_w_in, ssd_w_out, ssd_conv_w, ssd_conv_b, ssd_dt_bias, ssd_a_log, ssd_d, ssd_norm, xa_w_q, xa_w_kv, xa_w_o, ffn_w_gate, ffn_w_up, ffn_w_down):
    bsz, seq, d = x.shape
    depth = norm_mix.shape[0]
    m = bsz * seq
    lru_w = lru_conv_w.shape[2]
    hgrn_w = hgrn_norm.shape[1]
    ssd_heads = ssd_a_log.shape[1]
    ssd_inner = ssd_norm.shape[1]
    ssd_groups = ssd_heads // SSD_HPG
    ssd_main = ssd_w_in.shape[2] - ssd_heads

    sm = jax.nn.softmax(hgrn_lower_bounds.astype(F32), axis=0)
    lower_bounds = jnp.cumsum(sm, axis=0) - sm[0]
    lru_sp = jax.nn.softplus(-lru_lambda.astype(F32))
    ssd_a_neg = -jnp.exp(ssd_a_log.astype(F32))
    lru_w_ri = jnp.concatenate([lru_w_r, lru_w_i], axis=-1).astype(BF16)
    ab_in_b = _cast_bf16(ab_w_in)
    ab_out_b = _cast_bf16(ab_w_out)
    ssd_in_b = ssd_w_in[:, :, :ssd_main].astype(BF16)
    ssd_dt_b = ssd_w_in[:, :, ssd_main:].astype(BF16)
    ssd_out_b = _cast_bf16(ssd_w_out)
    wq_b = _cast_bf16(xa_w_q)
    wo_b = _cast_bf16(xa_w_o)
    wg_b = _cast_bf16(ffn_w_gate)
    wu_b = _cast_bf16(ffn_w_up)
    wd_b = _cast_bf16(ffn_w_down)

    x2 = x.reshape(m, d)
    mem_len = mem.shape[1]
    kv = _kv_proj(mem.reshape(bsz * mem_len, d), norm_mem, xa_w_kv, TN).reshape(depth, bsz, mem_len, 2 * d)

    for layer in range(depth):
        if layer % 2 == 0:
            e = layer // 2
            proj = _norm_matmul(x2, norm_mix[layer], ab_in_b, e, F32, TM, TN, "ab_in_proj")
            proj = proj.reshape(bsz, seq, -1)
            ya = _lru(proj, lru_conv_w[e], lru_conv_b[e], lru_w_ri[e], lru_b_r[e], lru_b_i[e], lru_sp[e], LRU_TB)
            yb = _hgrn(proj, lower_bounds[e], hgrn_norm[e], 2 * lru_w, hgrn_w // HGRN_DK, HGRN_TB)
            x2 = _out_proj2(ya.reshape(m, lru_w), yb.reshape(m, hgrn_w), ab_out_b, e, x2, SQ_TM, SQ_TN)
        else:
            o = layer // 2
            proj, dt_raw = _norm_matmul_side(x2, norm_mix[layer], ssd_in_b, ssd_dt_b, o, TM, TN, "ssd_in_proj")
            x2 = _ssd(proj.reshape(bsz, seq, ssd_main), dt_raw.reshape(bsz, seq, ssd_heads),
                      ssd_conv_w[o], ssd_conv_b[o], ssd_dt_bias[o], ssd_a_neg[o], ssd_d[o], ssd_norm[o],
                      ssd_out_b, o, x2.reshape(bsz, seq, d), ssd_inner, ssd_groups).reshape(m, d)
        q = _norm_matmul(x2, norm_xattn[layer], wq_b, layer, BF16, SQ_TM, SQ_TN, "xa_q_proj")
        x2 = _xattn(q.reshape(bsz, seq, d), kv, layer, wo_b, x2.reshape(bsz, seq, d), XA_TM)
        x2 = x2.reshape(m, d)
        g_out = norm_final if layer == depth - 1 else None
        x2 = _ffn(x2, norm_ffn[layer], wg_b, wu_b, wd_b, layer, TM, FFN_TH, g_out)
    return x2.reshape(bsz, seq, d)
```

```python
import functools
import math

import jax
import jax.numpy as jnp
from jax import lax
from jax.experimental import pallas as pl
from jax.experimental.pallas import tpu as pltpu

F32 = jnp.float32
BF16 = jnp.bfloat16
EPS = 1e-6
NEG_INF = float("-inf")

VMEM_LIMIT_BYTES = 56 * 1024 * 1024
CAST_BLOCK_BYTES = 4 * 1024 * 1024
SUBLANES = 8
LANES = 128

CONV_K = 4
LRU_BLOCK = 128
LRU_C = 8.0
HGRN_DK = 128
HGRN_CHUNK = 64
HGRN_SUB = SUBLANES
SSD_HEADDIM = 64
SSD_HPG = 8
SSD_STATE = 128
SSD_CHUNK = 128
XA_HEADS = 4


def _cparams(*sem):
    return pltpu.CompilerParams(dimension_semantics=sem, vmem_limit_bytes=VMEM_LIMIT_BYTES)


def _rms_scale(x):
    return lax.rsqrt(jnp.mean(x * x, axis=-1, keepdims=True) + EPS)


def _sigmoid(x):
    return 0.5 * (1.0 + jnp.tanh(0.5 * x))


def _silu(x):
    return x * _sigmoid(x)


def _gelu_tanh(x):
    c = math.sqrt(2.0 / math.pi)
    return x * (0.5 * (1.0 + jnp.tanh(c * (x + 0.044715 * (x * x * x)))))


def _softplus(x):
    return jnp.maximum(x, 0.0) + jnp.log1p(jnp.exp(-jnp.abs(x)))


def _linear_scan_rows(a, b, h0):
    t, c = a.shape
    groups = t // SUBLANES
    a3 = a.reshape(groups, SUBLANES, c)
    b3 = b.reshape(groups, SUBLANES, c)
    row = lax.broadcasted_iota(jnp.int32, a3.shape, 1)
    s = 1
    while s < SUBLANES:
        keep = row >= s
        a_sh = pltpu.roll(a3, s, 1)
        b_sh = pltpu.roll(b3, s, 1)
        b3 = jnp.where(keep, a3 * b_sh + b3, b3)
        a3 = jnp.where(keep, a3 * a_sh, a3)
        s *= 2
    out = []
    carry = h0
    for g in range(groups):
        h = a3[g] * carry + b3[g]
        out.append(h)
        carry = h[SUBLANES - 1:SUBLANES, :]
    return jnp.concatenate(out, axis=0)


def _sqrt_nonneg(y):
    return jnp.where(y > 0.0, y * lax.rsqrt(y), 0.0)


def _cumsum(x, axis):
    n = x.shape[axis]
    idx = lax.broadcasted_iota(jnp.int32, x.shape, axis)
    s = 1
    while s < n:
        x = jnp.where(idx >= s, x + pltpu.roll(x, s, axis), x)
        s *= 2
    return x


def _causal_conv(u, buf_ref, w_ref, b_ref):
    t = u.shape[0]
    buf_ref[SUBLANES:SUBLANES + t, :] = u
    out = b_ref[...] + w_ref[3:4, :] * u
    for j in range(1, CONV_K):
        out = out + w_ref[CONV_K - 1 - j:CONV_K - j, :] * buf_ref[SUBLANES - j:SUBLANES - j + t, :]
    buf_ref[0:SUBLANES, :] = u[t - SUBLANES:t, :]
    return out


def _norm_matmul_kernel(x_ref, g_ref, w_ref, o_ref, xn_ref):
    @pl.when(pl.program_id(1) == 0)
    def _():
        x = x_ref[...]
        xn_ref[...] = (x * _rms_scale(x) * g_ref[...]).astype(BF16)

    o_ref[...] = jnp.dot(xn_ref[...], w_ref[...], preferred_element_type=F32).astype(o_ref.dtype)


def _norm_matmul(x, g, w, layer, out_dtype, tm, tn, name):
    m, k = x.shape
    n = w.shape[2]
    tm = min(tm, m)
    tn = min(tn, n)
    return pl.pallas_call(
        _norm_matmul_kernel,
        grid=(m // tm, n // tn),
        in_specs=[
            pl.BlockSpec((tm, k), lambda i, j: (i, 0)),
            pl.BlockSpec((1, k), lambda i, j: (0, 0)),
            pl.BlockSpec((None, k, tn), lambda i, j: (layer, 0, j)),
        ],
        out_specs=pl.BlockSpec((tm, tn), lambda i, j: (i, j)),
        out_shape=jax.ShapeDtypeStruct((m, n), out_dtype),
        scratch_shapes=[pltpu.VMEM((tm, k), BF16)],
        compiler_params=_cparams("parallel", "arbitrary"),
        name=name,
    )(x, g.reshape(1, k), w)


def _norm_matmul_side_kernel(x_ref, g_ref, w_ref, ws_ref, o_ref, os_ref, xn_ref):
    @pl.when(pl.program_id(1) == 0)
    def _():
        x = x_ref[...]
        xn = (x * _rms_scale(x) * g_ref[...]).astype(BF16)
        xn_ref[...] = xn
        os_ref[...] = jnp.dot(xn, ws_ref[...], preferred_element_type=F32)

    o_ref[...] = jnp.dot(xn_ref[...], w_ref[...], preferred_element_type=F32).astype(o_ref.dtype)


def _norm_matmul_side(x, g, w, w_side, layer, tm, tn, name):
    m, k = x.shape
    n = w.shape[2]
    ns = w_side.shape[2]
    tm = min(tm, m)
    tn = min(tn, n)
    return pl.pallas_call(
        _norm_matmul_side_kernel,
        grid=(m // tm, n // tn),
        in_specs=[
            pl.BlockSpec((tm, k), lambda i, j: (i, 0)),
            pl.BlockSpec((1, k), lambda i, j: (0, 0)),
            pl.BlockSpec((None, k, tn), lambda i, j: (layer, 0, j)),
            pl.BlockSpec((None, k, ns), lambda i, j: (layer, 0, 0)),
        ],
        out_specs=[pl.BlockSpec((tm, tn), lambda i, j: (i, j)), pl.BlockSpec((tm, ns), lambda i, j: (i, 0))],
        out_shape=[jax.ShapeDtypeStruct((m, n), F32), jax.ShapeDtypeStruct((m, ns), F32)],
        scratch_shapes=[pltpu.VMEM((tm, k), BF16)],
        compiler_params=_cparams("parallel", "arbitrary"),
        name=name,
    )(x, g.reshape(1, k), w, w_side)


def _kv_proj_kernel(x_ref, g_ref, w_ref, o_ref, xn_ref):
    @pl.when((pl.program_id(0) == 0) & (pl.program_id(1) == 0))
    def _():
        x = x_ref[...]
        xn_ref[...] = (x * _rms_scale(x) * g_ref[...]).astype(BF16)

    o_ref[...] = jnp.dot(xn_ref[...], w_ref[...].astype(BF16), preferred_element_type=F32).astype(o_ref.dtype)


def _kv_proj(x, g, w, tn):
    m, k = x.shape
    nl, _, n = w.shape
    return pl.pallas_call(
        _kv_proj_kernel,
        grid=(nl, n // tn),
        in_specs=[
            pl.BlockSpec((m, k), lambda l, j: (0, 0)),
            pl.BlockSpec((1, k), lambda l, j: (0, 0)),
            pl.BlockSpec((None, k, tn), lambda l, j: (l, 0, j)),
        ],
        out_specs=pl.BlockSpec((None, m, tn), lambda l, j: (l, 0, j)),
        out_shape=jax.ShapeDtypeStruct((nl, m, n), BF16),
        scratch_shapes=[pltpu.VMEM((m, k), BF16)],
        compiler_params=_cparams("arbitrary", "arbitrary"),
        name="kv_proj",
    )(x, g.reshape(1, k), w)


def _cast_kernel(x_ref, o_ref):
    o_ref[...] = x_ref[...].astype(BF16)


def _cast_bf16(w):
    nl, k, n = w.shape
    tk = SUBLANES
    while tk * 2 <= k and k % (tk * 2) == 0 and tk * 2 * n * 4 <= CAST_BLOCK_BYTES:
        tk *= 2
    return pl.pallas_call(
        _cast_kernel,
        grid=(nl, k // tk),
        in_specs=[pl.BlockSpec((None, tk, n), lambda l, i: (l, i, 0))],
        out_specs=pl.BlockSpec((None, tk, n), lambda l, i: (l, i, 0)),
        out_shape=jax.ShapeDtypeStruct((nl, k, n), BF16),
        compiler_params=_cparams("parallel", "parallel"),
        name="cast_bf16",
    )(w)


def _ffn_kernel(x_ref, g_ref, go_ref, wg_ref, wu_ref, wd_ref, o_ref, xn_ref, *, out_norm):
    @pl.when(pl.program_id(1) == 0)
    def _():
        x = x_ref[...]
        xn_ref[...] = (x * _rms_scale(x) * g_ref[...]).astype(BF16)
        o_ref[...] = x

    xn = xn_ref[...]
    gate = jnp.dot(xn, wg_ref[...], preferred_element_type=F32)
    up = jnp.dot(xn, wu_ref[...], preferred_element_type=F32)
    hid = (_silu(gate) * up).astype(BF16)
    o_ref[...] += jnp.dot(hid, wd_ref[...], preferred_element_type=F32)

    if out_norm:
        @pl.when(pl.program_id(1) == pl.num_programs(1) - 1)
        def _():
            y = o_ref[...]
            o_ref[...] = y * _rms_scale(y) * go_ref[...]


def _ffn(x, g, wg, wu, wd, layer, tm, th, g_out=None):
    m, d = x.shape
    hdim = wg.shape[2]
    tm = min(tm, m)
    out_norm = g_out is not None
    g_out = g if g_out is None else g_out
    return pl.pallas_call(
        functools.partial(_ffn_kernel, out_norm=out_norm),
        grid=(m // tm, hdim // th),
        in_specs=[
            pl.BlockSpec((tm, d), lambda i, j: (i, 0)),
            pl.BlockSpec((1, d), lambda i, j: (0, 0)),
            pl.BlockSpec((1, d), lambda i, j: (0, 0)),
            pl.BlockSpec((None, d, th), lambda i, j: (layer, 0, j)),
            pl.BlockSpec((None, d, th), lambda i, j: (layer, 0, j)),
            pl.BlockSpec((None, th, d), lambda i, j: (layer, j, 0)),
        ],
        out_specs=pl.BlockSpec((tm, d), lambda i, j: (i, 0)),
        out_shape=jax.ShapeDtypeStruct((m, d), F32),
        scratch_shapes=[pltpu.VMEM((tm, d), BF16)],
        compiler_params=_cparams("parallel", "arbitrary"),
        name="ffn",
    )(x, g.reshape(1, d), g_out.reshape(1, d), wg, wu, wd)


def _xattn_kernel(q_ref, k_ref, v_ref, wo_ref, x_ref, o_ref, ob_ref, *, heads, scale):
    hd = q_ref.shape[2] // heads
    for h in range(heads):
        sl = slice(h * hd, (h + 1) * hd)
        s = lax.dot_general(q_ref[0, :, sl], k_ref[0, :, sl], (((1,), (1,)), ((), ())),
                            preferred_element_type=F32) * scale
        p = jnp.exp(s - jnp.max(s, axis=-1, keepdims=True))
        p = p / jnp.sum(p, axis=-1, keepdims=True)
        ob_ref[:, sl] = jnp.dot(p.astype(BF16), v_ref[0, :, sl], preferred_element_type=F32).astype(BF16)
    o_ref[0] = x_ref[0] + jnp.dot(ob_ref[...], wo_ref[...], preferred_element_type=F32)


def _xattn(q, kv, layer, wo, x, tm):
    b, s, d = x.shape
    mem = kv.shape[2]
    tm = min(tm, s)
    kern = functools.partial(_xattn_kernel, heads=XA_HEADS, scale=(d // XA_HEADS) ** -0.5)
    return pl.pallas_call(
        kern,
        grid=(b, s // tm),
        in_specs=[
            pl.BlockSpec((1, tm, d), lambda i, j: (i, j, 0)),
            pl.BlockSpec((None, 1, mem, d), lambda i, j: (layer, i, 0, 0)),
            pl.BlockSpec((None, 1, mem, d), lambda i, j: (layer, i, 0, 1)),
            pl.BlockSpec((None, d, d), lambda i, j: (layer, 0, 0)),
            pl.BlockSpec((1, tm, d), lambda i, j: (i, j, 0)),
        ],
        out_specs=pl.BlockSpec((1, tm, d), lambda i, j: (i, j, 0)),
        out_shape=jax.ShapeDtypeStruct((b, s, d), F32),
        scratch_shapes=[pltpu.VMEM((tm, d), BF16)],
        compiler_params=_cparams("parallel", "arbitrary"),
        name="xattn",
    )(q, kv, kv, wo, x)


def _lru_kernel(xa_ref, ga_ref, cw_ref, cb_ref, wri_ref, br_ref, bi_ref, sp_ref, o_ref, buf_ref, h_ref):
    @pl.when(pl.program_id(1) == 0)
    def _():
        buf_ref[0:SUBLANES, :] = jnp.zeros((SUBLANES, buf_ref.shape[1]), F32)
        h_ref[...] = jnp.zeros(h_ref.shape, F32)

    xc = _causal_conv(xa_ref[0], buf_ref, cw_ref, cb_ref)
    t = xc.shape[0]
    for blk in range(xc.shape[1] // LRU_BLOCK):
        sl = slice(blk * LRU_BLOCK, (blk + 1) * LRU_BLOCK)
        xb = xc[:, sl]
        pre = jnp.dot(xb.astype(BF16), wri_ref[blk], preferred_element_type=F32)
        r_gate = _sigmoid(pre[:, :LRU_BLOCK] + br_ref[:, sl])
        i_gate = _sigmoid(pre[:, LRU_BLOCK:] + bi_ref[:, sl])
        log_a = (-LRU_C) * r_gate * sp_ref[:, sl]
        a = jnp.exp(log_a)
        mult = _sqrt_nonneg(-jnp.tanh(log_a) * (a * a + 1.0))
        h = _linear_scan_rows(a, mult * i_gate * xb, h_ref[0:1, sl])
        h_ref[0:1, sl] = h[t - 1:t, :]
        o_ref[0, :, sl] = (_gelu_tanh(ga_ref[0, :, sl]) * h).astype(BF16)


def _lru(proj, conv_w, conv_b, w_ri, b_r, b_i, sp, tb):
    b, s, _ = proj.shape
    w = conv_w.shape[1]
    tb = min(tb, s)
    vec = lambda: pl.BlockSpec((1, w), lambda i, j: (0, 0))
    return pl.pallas_call(
        _lru_kernel,
        grid=(b, s // tb),
        in_specs=[
            pl.BlockSpec((1, tb, w), lambda i, j: (i, j, 0)),
            pl.BlockSpec((1, tb, w), lambda i, j: (i, j, 1)),
            pl.BlockSpec((CONV_K, w), lambda i, j: (0, 0)),
            vec(),
            pl.BlockSpec(w_ri.shape, lambda i, j: (0, 0, 0)),
            vec(), vec(), vec(),
        ],
        out_specs=pl.BlockSpec((1, tb, w), lambda i, j: (i, j, 0)),
        out_shape=jax.ShapeDtypeStruct((b, s, w), BF16),
        scratch_shapes=[pltpu.VMEM((tb + SUBLANES, w), F32), pltpu.VMEM((SUBLANES, w), F32)],
        compiler_params=_cparams("parallel", "arbitrary"),
        name="lru",
    )(proj, proj, conv_w, conv_b.reshape(1, w), w_ri, b_r.reshape(1, w), b_i.reshape(1, w), sp.reshape(1, w))


def _hgrn_chunk(q, f, v, lb, st):
    c, dk = q.shape
    nsub = c // HGRN_SUB
    qh = _silu(q)
    fg = lb + (1.0 - lb) * _sigmoid(f)
    kh = 1.0 - fg
    g = jnp.log(fg)
    cum = _cumsum(g, 0)
    ex = cum - g
    cum3 = cum.reshape(nsub, HGRN_SUB, dk)
    ex3 = ex.reshape(nsub, HGRN_SUB, dk)
    base3 = jnp.broadcast_to(ex3[:, 0:1, :], cum3.shape)
    q3 = qh.reshape(nsub, HGRN_SUB, dk)
    k3 = kh.reshape(nsub, HGRN_SUB, dk)
    v3 = v.reshape(nsub, HGRN_SUB, dk)
    vb = v.astype(BF16)

    row3 = lax.broadcasted_iota(jnp.int32, cum3.shape, 1)
    terms = []
    for s in range(HGRN_SUB):
        diff = cum3 - cum3[:, s:s + 1, :]
        dec = jnp.exp(jnp.where(row3 >= s, diff, NEG_INF))
        terms.append((dec * q3 * k3[:, s:s + 1, :]).reshape(c, dk))
    stacked = jnp.concatenate(terms, axis=0).astype(BF16)
    ones = jnp.ones((dk, dk), BF16)
    summed = jnp.dot(stacked, ones, preferred_element_type=F32)
    o3 = jnp.zeros(cum3.shape, F32)
    for s in range(HGRN_SUB):
        o3 = o3 + summed[s * c:(s + 1) * c, :].reshape(nsub, HGRN_SUB, dk) * v3[:, s:s + 1, :]
    o = o3.reshape(c, dk)

    q_loc = (qh * jnp.exp(cum - base3.reshape(c, dk))).astype(BF16)
    pad = jnp.zeros((LANES, dk), F32)
    k_parts = []
    for i in range(1, nsub):
        n = i * HGRN_SUB
        base_i = ex[n:n + 1, :]
        k_parts.append(kh[0:n, :] * jnp.exp(base_i - cum[0:n, :]))
        k_parts.append(pad[0:LANES - n, :])
    k_hat = jnp.concatenate(k_parts, axis=0).astype(BF16)
    a_all = lax.dot_general(q_loc, k_hat, (((1,), (1,)), ((), ())), preferred_element_type=F32)
    a_rows = [jnp.zeros((HGRN_SUB, c), F32)]
    for i in range(1, nsub):
        a_rows.append(a_all[i * HGRN_SUB:(i + 1) * HGRN_SUB, (i - 1) * LANES:(i - 1) * LANES + c])
    a_off = jnp.concatenate(a_rows, axis=0).astype(BF16)
    o = o + jnp.dot(a_off, vb, preferred_element_type=F32)

    q_in = (qh * jnp.exp(cum)).astype(BF16)
    o = o + lax.dot_general(q_in, st.astype(BF16), (((1,), (1,)), ((), ())), preferred_element_type=F32)
    last = cum[c - 1:c, :]
    k_out = (kh * jnp.exp(last - cum)).astype(BF16)
    st_new = st * jnp.exp(last) + lax.dot_general(vb, k_out, (((0,), (0,)), ((), ())),
                                                  preferred_element_type=F32)
    return o, st_new


def _hgrn_kernel(q_ref, f_ref, v_ref, gb_ref, lb_ref, gn_ref, o_ref, st_ref):
    @pl.when(pl.program_id(2) == 0)
    def _():
        st_ref[...] = jnp.zeros(st_ref.shape, F32)

    lb = lb_ref[...]
    st = st_ref[...]
    for ci in range(q_ref.shape[1] // HGRN_CHUNK):
        rows = slice(ci * HGRN_CHUNK, (ci + 1) * HGRN_CHUNK)
        o, st = _hgrn_chunk(q_ref[0, rows, :], f_ref[0, rows, :], v_ref[0, rows, :], lb, st)
        o = o * _rms_scale(o) * gn_ref[...]
        o_ref[0, rows, :] = (o * _silu(gb_ref[0, rows, :])).astype(BF16)
    st_ref[...] = st


def _hgrn(proj, lower_bound, head_norm, col0, heads, tb):
    b, s, _ = proj.shape
    tb = min(tb, s)
    c0 = col0 // HGRN_DK
    part = lambda p: pl.BlockSpec((1, tb, HGRN_DK), lambda i, h, j: (i, j, c0 + p * heads + h))
    vec = lambda: pl.BlockSpec((1, HGRN_DK), lambda i, h, j: (0, h))
    w = heads * HGRN_DK
    return pl.pallas_call(
        _hgrn_kernel,
        grid=(b, heads, s // tb),
        in_specs=[part(0), part(1), part(2), part(3), vec(), vec()],
        out_specs=pl.BlockSpec((1, tb, HGRN_DK), lambda i, h, j: (i, j, h)),
        out_shape=jax.ShapeDtypeStruct((b, s, w), BF16),
        scratch_shapes=[pltpu.VMEM((HGRN_DK, HGRN_DK), F32)],
        compiler_params=_cparams("parallel", "parallel", "arbitrary"),
        name="hgrn",
    )(proj, proj, proj, proj, lower_bound.reshape(1, w), head_norm.reshape(1, w))


def _out_proj2_kernel(a1_ref, a2_ref, w1_ref, w2_ref, r_ref, o_ref):
    acc = jnp.dot(a1_ref[...], w1_ref[...], preferred_element_type=F32)
    acc = acc + jnp.dot(a2_ref[...], w2_ref[...], preferred_element_type=F32)
    o_ref[...] = r_ref[...] + acc


def _out_proj2(a1, a2, w, layer, res, tm, tn):
    m, k1 = a1.shape
    assert a2.shape[1] == k1
    n = w.shape[2]
    tm = min(tm, m)
    tn = min(tn, n)
    return pl.pallas_call(
        _out_proj2_kernel,
        grid=(m // tm, n // tn),
        in_specs=[
            pl.BlockSpec((tm, k1), lambda i, j: (i, 0)),
            pl.BlockSpec((tm, k1), lambda i, j: (i, 0)),
            pl.BlockSpec((None, k1, tn), lambda i, j: (layer, 0, j)),
            pl.BlockSpec((None, k1, tn), lambda i, j: (layer, 1, j)),
            pl.BlockSpec((tm, tn), lambda i, j: (i, j)),
        ],
        out_specs=pl.BlockSpec((tm, tn), lambda i, j: (i, j)),
        out_shape=jax.ShapeDtypeStruct((m, n), F32),
        compiler_params=_cparams("parallel", "arbitrary"),
        name="ab_out_proj",
    )(a1, a2, w, w, res)


def _split3(x):
    hi = x.astype(BF16).astype(F32)
    rem = x - hi
    mid = rem.astype(BF16).astype(F32)
    return hi, mid, rem - mid


def _ssd_scan_block(z_ref, x_ref, b_ref, c_ref, dtr_ref, conv_refs, dbr_ref, anr_ref, dsk_ref, nw_ref,
                    tri_ref, sel_ref, st_ref, y_ref, buf_refs, tail_refs):
    tb = x_ref.shape[1]
    t = tri_ref.shape[0]
    n = st_ref.shape[0]
    hpg = dbr_ref.shape[0]
    (cwx_ref, cbx_ref, cwb_ref, cbb_ref, cwc_ref, cbc_ref) = conv_refs
    xs_all = _silu(_causal_conv(x_ref[0], buf_refs[0], cwx_ref, cbx_ref))
    bm_all = _silu(_causal_conv(b_ref[0], buf_refs[1], cwb_ref, cbb_ref))
    cm_all = _silu(_causal_conv(c_ref[0], buf_refs[2], cwc_ref, cbc_ref))
    for buf_ref, tail_ref in zip(buf_refs, tail_refs):
        tail_ref[...] = buf_ref[0:SUBLANES, :]
    dtr_all = _softplus(dtr_ref[0, 0] + dbr_ref[...])
    causal = (lax.broadcasted_iota(jnp.int32, (t, t), 0) >= lax.broadcasted_iota(jnp.int32, (t, t), 1))
    low_t = lax.broadcasted_iota(jnp.int32, (t, LANES), 1) < SSD_HEADDIM
    low_n = lax.broadcasted_iota(jnp.int32, (n, LANES), 1) < SSD_HEADDIM
    pad = jnp.zeros((sel_ref.shape[0] - 3 * hpg, t), F32)

    for ci in range(tb // t):
        rows = slice(ci * t, (ci + 1) * t)
        xs, bm, cm, dtr = xs_all[rows], bm_all[rows], cm_all[rows], dtr_all[:, rows]
        xsb = xs.astype(BF16)
        dta = dtr * anr_ref[...]
        parts = jnp.concatenate(list(_split3(dta)) + [jnp.zeros_like(dta)], axis=0).astype(BF16)
        c3 = jnp.dot(parts, tri_ref[...], preferred_element_type=F32)
        cum = c3[0:hpg] + c3[hpg:2 * hpg] + c3[2 * hpg:3 * hpg]
        cols = jnp.concatenate(list(_split3(cum)) + [pad], axis=0).astype(BF16)
        bc_all = lax.dot_general(cols, sel_ref[...], (((0,), (0,)), ((), ())), preferred_element_type=F32)
        row_term = cum - jnp.log(dtr)
        last = jnp.broadcast_to(cum[:, t - 1:t], cum.shape)
        row_scale = jnp.exp(last - cum) * dtr
        e_last = jnp.exp(bc_all[t - 1:t, :])
        cb = lax.dot_general(cm.astype(BF16), bm.astype(BF16), (((1,), (1,)), ((), ())),
                             preferred_element_type=F32)
        bt = bm.T

        for pair in range(hpg // 2):
            pc = slice(pair * LANES, (pair + 1) * LANES)
            rhs = jnp.concatenate([xsb[:, pc], st_ref[:, pc].astype(BF16)], axis=0)
            lhs, zl = [], []
            for h in (2 * pair, 2 * pair + 1):
                bc = bc_all[:, h * LANES:(h + 1) * LANES]
                dec = jnp.exp(jnp.where(causal, bc - row_term[h:h + 1, :], NEG_INF))
                lhs.append(jnp.concatenate([cb * dec, cm * jnp.exp(bc)], axis=1).astype(BF16))
                zl.append((bt * row_scale[h:h + 1, :]).astype(BF16))
            y2 = jnp.dot(jnp.concatenate(lhs, axis=0), rhs, preferred_element_type=F32)
            y_ref[rows, pc] = jnp.where(low_t, y2[:t], y2[t:]) + dsk_ref[:, pc] * xs[:, pc]
            z2 = jnp.dot(jnp.concatenate(zl, axis=0), xsb[:, pc], preferred_element_type=F32)
            e_pair = jnp.where(low_n[0:1], e_last[:, 2 * pair * LANES:(2 * pair + 1) * LANES],
                               e_last[:, (2 * pair + 1) * LANES:(2 * pair + 2) * LANES])
            st_ref[:, pc] = st_ref[:, pc] * e_pair + jnp.where(low_n, z2[:n], z2[n:])

    y = y_ref[...] * _silu(z_ref[0])
    return (y * _rms_scale(y) * nw_ref[...]).astype(BF16)


def _ssd_kernel(z_ref, x_ref, b_ref, c_ref, dtr_ref,
                cwx_ref, cbx_ref, cwb_ref, cbb_ref, cwc_ref, cbc_ref,
                dbr_ref, anr_ref, dsk_ref, nw_ref, tri_ref, sel_ref, wo_ref, res_ref,
                o_ref, st_ref, y_ref, yn_ref, bufx_ref, bufb_ref, bufc_ref, tailx_ref, tailb_ref, tailc_ref,
                *, row_blocks, total_blocks):
    kk = pl.program_id(0)
    g = pl.program_id(1)
    slot = lax.rem(kk, 2)
    groups = yn_ref.shape[1]
    bufs = (bufx_ref, bufb_ref, bufc_ref)
    tails = (tailx_ref.at[g], tailb_ref.at[g], tailc_ref.at[g])

    def enter():
        seq_start = lax.rem(kk, row_blocks) == 0

        @pl.when(seq_start)
        def _():
            st_ref[g] = jnp.zeros(st_ref.shape[1:], F32)
            for buf_ref in bufs:
                buf_ref[0:SUBLANES, :] = jnp.zeros((SUBLANES, buf_ref.shape[1]), F32)

        @pl.when(jnp.logical_not(seq_start))
        def _():
            for buf_ref, tail_ref in zip(bufs, tails):
                buf_ref[0:SUBLANES, :] = tail_ref[...]

    def scan():
        yn_ref[slot, g] = _ssd_scan_block(
            z_ref, x_ref, b_ref, c_ref, dtr_ref,
            tuple(r.at[g] for r in (cwx_ref, cbx_ref, cwb_ref, cbb_ref, cwc_ref, cbc_ref)),
            dbr_ref.at[g], anr_ref.at[g], dsk_ref.at[g], nw_ref.at[g], tri_ref, sel_ref, st_ref.at[g], y_ref,
            bufs, tails)

    def project():
        lhs = jnp.concatenate([yn_ref[1 - slot, gg] for gg in range(groups)], axis=1)
        o_ref[0] = res_ref[0] + jnp.dot(lhs, wo_ref[...], preferred_element_type=F32)

    @pl.when(kk == 0)
    def _():
        enter()
        scan()

    @pl.when((kk > 0) & (kk < total_blocks))
    def _():
        enter()
        project()
        scan()

    @pl.when(kk == total_blocks)
    def _():
        project()


def _ssd(proj, dt_raw, conv_w, conv_b, dt_bias, a_neg, d_skip, norm_w, w_out, layer, res, inner, groups):
    b, s, _ = proj.shape
    d = res.shape[2]
    dg = d // groups
    t = min(SSD_CHUNK, s)
    tb = min(SSD_TB, s)
    gw = inner // groups
    n = SSD_STATE
    hpg = SSD_HPG
    assert gw == hpg * SSD_HEADDIM and 2 * SSD_HEADDIM == LANES and tb % t == 0
    dt_row = dt_raw.reshape(b, s, groups, hpg).transpose(0, 2, 3, 1)
    db_r = dt_bias.reshape(groups, hpg, 1)
    an_r = a_neg.reshape(groups, hpg, 1)
    dsk = jnp.repeat(d_skip, SSD_HEADDIM).reshape(groups, 1, gw)
    nw = norm_w.reshape(groups, 1, gw)
    by_group = lambda a, lo, w: (a[:, lo:lo + groups * w].reshape(a.shape[0], groups, w).transpose(1, 0, 2))
    conv_b2 = conv_b.reshape(1, -1)
    conv_parts = []
    for lo, w in ((0, gw), (inner, n), (inner + groups * n, n)):
        conv_parts += [by_group(conv_w, lo, w), by_group(conv_b2, lo, w)]
    tri = (jnp.arange(t)[:, None] <= jnp.arange(t)[None, :]).astype(BF16)
    krow = jnp.arange(LANES)[:, None]
    sel = ((krow < 3 * hpg) & ((krow % hpg) == (jnp.arange(hpg * LANES)[None, :] // LANES))).astype(BF16)
    xoff, boff, coff = inner // gw, (2 * inner) // n, (2 * inner + groups * n) // n
    nk = s // tb
    total = b * nk

    def cur(kk):
        c = jnp.minimum(kk, total - 1)
        return c // nk, c % nk

    def prev(kk):
        p = jnp.maximum(kk - 1, 0)
        return p // nk, p % nk

    out_col = lambda kk, g: jnp.where(kk == 0, 0, g)
    const = lambda a: pl.BlockSpec(a.shape, lambda kk, g: (0,) * a.ndim)
    small = conv_parts + [db_r, an_r, dsk, nw, tri, sel]
    return pl.pallas_call(
        functools.partial(_ssd_kernel, row_blocks=nk, total_blocks=total),
        grid=(total + 1, groups),
        in_specs=[
            pl.BlockSpec((1, tb, gw), lambda kk, g: (*cur(kk), g)),
            pl.BlockSpec((1, tb, gw), lambda kk, g: (*cur(kk), xoff + g)),
            pl.BlockSpec((1, tb, n), lambda kk, g: (*cur(kk), boff + g)),
            pl.BlockSpec((1, tb, n), lambda kk, g: (*cur(kk), coff + g)),
            pl.BlockSpec((1, 1, hpg, tb), lambda kk, g: (cur(kk)[0], g, 0, cur(kk)[1])),
            *[const(a) for a in small],
            pl.BlockSpec((None, inner, dg), lambda kk, g: (layer, 0, g)),
            pl.BlockSpec((1, tb, dg), lambda kk, g: (*prev(kk), out_col(kk, g))),
        ],
        out_specs=pl.BlockSpec((1, tb, dg), lambda kk, g: (*prev(kk), out_col(kk, g))),
        out_shape=jax.ShapeDtypeStruct((b, s, d), F32),
        scratch_shapes=[
            pltpu.VMEM((groups, n, gw), F32),
            pltpu.VMEM((tb, gw), F32),
            pltpu.VMEM((2, groups, tb, gw), BF16),
            pltpu.VMEM((tb + SUBLANES, gw), F32),
            pltpu.VMEM((tb + SUBLANES, n), F32),
            pltpu.VMEM((tb + SUBLANES, n), F32),
            pltpu.VMEM((groups, SUBLANES, gw), F32),
            pltpu.VMEM((groups, SUBLANES, n), F32),
            pltpu.VMEM((groups, SUBLANES, n), F32),
        ],
        compiler_params=_cparams("arbitrary", "arbitrary"),
        name="ssd",
    )(proj, proj, proj, proj, dt_row, *small, w_out, res)


TM = 1024
TN = 1024
SQ_TM = 512
SQ_TN = 2048
FFN_TH = 512
XA_TM = 512
LRU_TB = 256
HGRN_TB = 512
SSD_TB = 512


def kernel(x, mem, norm_mix, norm_xattn, norm_ffn, norm_mem, norm_final, ab_w_in, ab_w_out, lru_conv_w, lru_conv_b, lru_w_r, lru_b_r, lru_w_i, lru_b_i, lru_lambda, hgrn_lower_bounds, hgrn_norm, ssd_w_in, ssd_w_out, ssd_conv_w, ssd_conv_b, ssd_dt_bias, ssd_a_log, ssd_d, ssd_norm, xa_w_q, xa_w_kv, xa_w_o, ffn_w_gate, ffn_w_up, ffn_w_down):
    bsz, seq, d = x.shape
    depth = norm_mix.shape[0]
    m = bsz * seq
    lru_w = lru_conv_w.shape[2]
    hgrn_w = hgrn_norm.shape[1]
    ssd_heads = ssd_a_log.shape[1]
    ssd_inner = ssd_norm.shape[1]
    ssd_groups = ssd_heads // SSD_HPG
    ssd_main = ssd_w_in.shape[2] - ssd_heads

    sm = jax.nn.softmax(hgrn_lower_bounds.astype(F32), axis=0)
    lower_bounds = jnp.cumsum(sm, axis=0) - sm[0]
    lru_sp = jax.nn.softplus(-lru_lambda.astype(F32))
    ssd_a_neg = -jnp.exp(ssd_a_log.astype(F32))
    lru_w_ri = jnp.concatenate([lru_w_r, lru_w_i], axis=-1).astype(BF16)
    ab_in_b = _cast_bf16(ab_w_in)
    ab_out_b = _cast_bf16(ab_w_out)
    ssd_in_b = ssd_w_in[:, :, :ssd_main].astype(BF16)
    ssd_dt_b = ssd_w_in[:, :, ssd_main:].astype(BF16)
    ssd_out_b = _cast_bf16(ssd_w_out)
    wq_b = _cast_bf16(xa_w_q)
    wo_b = _cast_bf16(xa_w_o)
    wg_b = _cast_bf16(ffn_w_gate)
    wu_b = _cast_bf16(ffn_w_up)
    wd_b = _cast_bf16(ffn_w_down)

    x2 = x.reshape(m, d)
    mem_len = mem.shape[1]
    kv = _kv_proj(mem.reshape(bsz * mem_len, d), norm_mem, xa_w_kv, TN).reshape(depth, bsz, mem_len, 2 * d)

    for layer in range(depth):
        if layer % 2 == 0:
            e = layer // 2
            proj = _norm_matmul(x2, norm_mix[layer], ab_in_b, e, F32, TM, TN, "ab_in_proj")
            proj = proj.reshape(bsz, seq, -1)
            ya = _lru(proj, lru_conv_w[e], lru_conv_b[e], lru_w_ri[e], lru_b_r[e], lru_b_i[e], lru_sp[e], LRU_TB)
            yb = _hgrn(proj, lower_bounds[e], hgrn_norm[e], 2 * lru_w, hgrn_w // HGRN_DK, HGRN_TB)
            x2 = _out_proj2(ya.reshape(m, lru_w), yb.reshape(m, hgrn_w), ab_out_b, e, x2, SQ_TM, SQ_TN)
        else:
            o = layer // 2
            proj, dt_raw = _norm_matmul_side(x2, norm_mix[layer], ssd_in_b, ssd_dt_b, o, TM, TN, "ssd_in_proj")
            x2 = _ssd(proj.reshape(bsz, seq, ssd_main), dt_raw.reshape(bsz, seq, ssd_heads),
                      ssd_conv_w[o], ssd_conv_b[o], ssd_dt_bias[o], ssd_a_neg[o], ssd_d[o], ssd_norm[o],
                      ssd_out_b, o, x2.reshape(bsz, seq, d), ssd_inner, ssd_groups).reshape(m, d)
        q = _norm_matmul(x2, norm_xattn[layer], wq_b, layer, BF16, SQ_TM, SQ_TN, "xa_q_proj")
        x2 = _xattn(q.reshape(bsz, seq, d), kv, layer, wo_b, x2.reshape(bsz, seq, d), XA_TM)
        x2 = x2.reshape(m, d)
        g_out = norm_final if layer == depth - 1 else None
        x2 = _ffn(x2, norm_ffn[layer], wg_b, wu_b, wd_b, layer, TM, FFN_TH, g_out)
    return x2.reshape(bsz, seq, d)
```

```python
import functools
import math

import jax
import jax.numpy as jnp
from jax import lax
from jax.experimental import pallas as pl
from jax.experimental.pallas import tpu as pltpu

F32 = jnp.float32
BF16 = jnp.bfloat16
EPS = 1e-6
NEG_INF = float("-inf")

VMEM_LIMIT_BYTES = 56 * 1024 * 1024
CAST_BLOCK_BYTES = 4 * 1024 * 1024
SUBLANES = 8
LANES = 128

CONV_K = 4
LRU_BLOCK = 128
LRU_C = 8.0
HGRN_DK = 128
HGRN_CHUNK = 64
HGRN_SUB = SUBLANES
SSD_HEADDIM = 64
SSD_HPG = 8
SSD_STATE = 128
SSD_CHUNK = 128
XA_HEADS = 4


def _cparams(*sem):
    return pltpu.CompilerParams(dimension_semantics=sem, vmem_limit_bytes=VMEM_LIMIT_BYTES)


def _rms_scale(x):
    return lax.rsqrt(jnp.mean(x * x, axis=-1, keepdims=True) + EPS)


def _sigmoid(x):
    return 0.5 * (1.0 + jnp.tanh(0.5 * x))


def _silu(x):
    return x * _sigmoid(x)


def _gelu_tanh(x):
    c = math.sqrt(2.0 / math.pi)
    return x * (0.5 * (1.0 + jnp.tanh(c * (x + 0.044715 * (x * x * x)))))


def _softplus(x):
    return jnp.maximum(x, 0.0) + jnp.log1p(jnp.exp(-jnp.abs(x)))


def _linear_scan_rows(a, b, h0):
    t, c = a.shape
    groups = t // SUBLANES
    a3 = a.reshape(groups, SUBLANES, c)
    b3 = b.reshape(groups, SUBLANES, c)
    row = lax.broadcasted_iota(jnp.int32, a3.shape, 1)
    s = 1
    while s < SUBLANES:
        keep = row >= s
        a_sh = pltpu.roll(a3, s, 1)
        b_sh = pltpu.roll(b3, s, 1)
        b3 = jnp.where(keep, a3 * b_sh + b3, b3)
        a3 = jnp.where(keep, a3 * a_sh, a3)
        s *= 2
    out = []
    carry = h0
    for g in range(groups):
        h = a3[g] * carry + b3[g]
        out.append(h)
        carry = h[SUBLANES - 1:SUBLANES, :]
    return jnp.concatenate(out, axis=0)


def _sqrt_nonneg(y):
    return jnp.where(y > 0.0, y * lax.rsqrt(y), 0.0)


def _cumsum(x, axis):
    n = x.shape[axis]
    idx = lax.broadcasted_iota(jnp.int32, x.shape, axis)
    s = 1
    while s < n:
        x = jnp.where(idx >= s, x + pltpu.roll(x, s, axis), x)
        s *= 2
    return x


def _causal_conv(u, buf_ref, w_ref, b_ref):
    t = u.shape[0]
    buf_ref[SUBLANES:SUBLANES + t, :] = u
    out = b_ref[...] + w_ref[3:4, :] * u
    for j in range(1, CONV_K):
        out = out + w_ref[CONV_K - 1 - j:CONV_K - j, :] * buf_ref[SUBLANES - j:SUBLANES - j + t, :]
    buf_ref[0:SUBLANES, :] = u[t - SUBLANES:t, :]
    return out


def _norm_matmul_kernel(x_ref, g_ref, w_ref, o_ref, xn_ref):
    @pl.when(pl.program_id(1) == 0)
    def _():
        x = x_ref[...]
        xn_ref[...] = (x * _rms_scale(x) * g_ref[...]).astype(BF16)

    o_ref[...] = jnp.dot(xn_ref[...], w_ref[...], preferred_element_type=F32).astype(o_ref.dtype)


def _norm_matmul(x, g, w, layer, out_dtype, tm, tn, name):
    m, k = x.shape
    n = w.shape[2]
    tm = min(tm, m)
    tn = min(tn, n)
    return pl.pallas_call(
        _norm_matmul_kernel,
        grid=(m // tm, n // tn),
        in_specs=[
            pl.BlockSpec((tm, k), lambda i, j: (i, 0)),
            pl.BlockSpec((1, k), lambda i, j: (0, 0)),
            pl.BlockSpec((None, k, tn), lambda i, j: (layer, 0, j)),
        ],
        out_specs=pl.BlockSpec((tm, tn), lambda i, j: (i, j)),
        out_shape=jax.ShapeDtypeStruct((m, n), out_dtype),
        scratch_shapes=[pltpu.VMEM((tm, k), BF16)],
        compiler_params=_cparams("parallel", "arbitrary"),
        name=name,
    )(x, g.reshape(1, k), w)


def _norm_matmul_side_kernel(x_ref, g_ref, w_ref, ws_ref, o_ref, os_ref, xn_ref):
    @pl.when(pl.program_id(1) == 0)
    def _():
        x = x_ref[...]
        xn = (x * _rms_scale(x) * g_ref[...]).astype(BF16)
        xn_ref[...] = xn
        os_ref[...] = jnp.dot(xn, ws_ref[...], preferred_element_type=F32)

    o_ref[...] = jnp.dot(xn_ref[...], w_ref[...], preferred_element_type=F32).astype(o_ref.dtype)


def _norm_matmul_side(x, g, w, w_side, layer, tm, tn, name):
    m, k = x.shape
    n = w.shape[2]
    ns = w_side.shape[2]
    tm = min(tm, m)
    tn = min(tn, n)
    return pl.pallas_call(
        _norm_matmul_side_kernel,
        grid=(m // tm, n // tn),
        in_specs=[
            pl.BlockSpec((tm, k), lambda i, j: (i, 0)),
            pl.BlockSpec((1, k), lambda i, j: (0, 0)),
            pl.BlockSpec((None, k, tn), lambda i, j: (layer, 0, j)),
            pl.BlockSpec((None, k, ns), lambda i, j: (layer, 0, 0)),
        ],
        out_specs=[pl.BlockSpec((tm, tn), lambda i, j: (i, j)), pl.BlockSpec((tm, ns), lambda i, j: (i, 0))],
        out_shape=[jax.ShapeDtypeStruct((m, n), F32), jax.ShapeDtypeStruct((m, ns), F32)],
        scratch_shapes=[pltpu.VMEM((tm, k), BF16)],
        compiler_params=_cparams("parallel", "arbitrary"),
        name=name,
    )(x, g.reshape(1, k), w, w_side)


def _kv_proj_kernel(x_ref, g_ref, w_ref, o_ref, xn_ref):
    @pl.when((pl.program_id(0) == 0) & (pl.program_id(1) == 0))
    def _():
        x = x_ref[...]
        xn_ref[...] = (x * _rms_scale(x) * g_ref[...]).astype(BF16)

    o_ref[...] = jnp.dot(xn_ref[...], w_ref[...].astype(BF16), preferred_element_type=F32).astype(o_ref.dtype)


def _kv_proj(x, g, w, tn):
    m, k = x.shape
    nl, _, n = w.shape
    return pl.pallas_call(
        _kv_proj_kernel,
        grid=(nl, n // tn),
        in_specs=[
            pl.BlockSpec((m, k), lambda l, j: (0, 0)),
            pl.BlockSpec((1, k), lambda l, j: (0, 0)),
            pl.BlockSpec((None, k, tn), lambda l, j: (l, 0, j)),
        ],
        out_specs=pl.BlockSpec((None, m, tn), lambda l, j: (l, 0, j)),
        out_shape=jax.ShapeDtypeStruct((nl, m, n), BF16),
        scratch_shapes=[pltpu.VMEM((m, k), BF16)],
        compiler_params=_cparams("arbitrary", "arbitrary"),
        name="kv_proj",
    )(x, g.reshape(1, k), w)


def _cast_kernel(x_ref, o_ref):
    o_ref[...] = x_ref[...].astype(BF16)


def _cast_bf16(w):
    nl, k, n = w.shape
    tk = SUBLANES
    while tk * 2 <= k and k % (tk * 2) == 0 and tk * 2 * n * 4 <= CAST_BLOCK_BYTES:
        tk *= 2
    return pl.pallas_call(
        _cast_kernel,
        grid=(nl, k // tk),
        in_specs=[pl.BlockSpec((None, tk, n), lambda l, i: (l, i, 0))],
        out_specs=pl.BlockSpec((None, tk, n), lambda l, i: (l, i, 0)),
        out_shape=jax.ShapeDtypeStruct((nl, k, n), BF16),
        compiler_params=_cparams("parallel", "parallel"),
        name="cast_bf16",
    )(w)


def _ffn_kernel(x_ref, g_ref, go_ref, wg_ref, wu_ref, wd_ref, o_ref, xn_ref, *, out_norm):
    @pl.when(pl.program_id(1) == 0)
    def _():
        x = x_ref[...]
        xn_ref[...] = (x * _rms_scale(x) * g_ref[...]).astype(BF16)
        o_ref[...] = x

    xn = xn_ref[...]
    gate = jnp.dot(xn, wg_ref[...], preferred_element_type=F32)
    up = jnp.dot(xn, wu_ref[...], preferred_element_type=F32)
    hid = (_silu(gate) * up).astype(BF16)
    o_ref[...] += jnp.dot(hid, wd_ref[...], preferred_element_type=F32)

    if out_norm:
        @pl.when(pl.program_id(1) == pl.num_programs(1) - 1)
        def _():
            y = o_ref[...]
            o_ref[...] = y * _rms_scale(y) * go_ref[...]


def _ffn(x, g, wg, wu, wd, layer, tm, th, g_out=None):
    m, d = x.shape
    hdim = wg.shape[2]
    tm = min(tm, m)
    out_norm = g_out is not None
    g_out = g if g_out is None else g_out
    return pl.pallas_call(
        functools.partial(_ffn_kernel, out_norm=out_norm),
        grid=(m // tm, hdim // th),
        in_specs=[
            pl.BlockSpec((tm, d), lambda i, j: (i, 0)),
            pl.BlockSpec((1, d), lambda i, j: (0, 0)),
            pl.BlockSpec((1, d), lambda i, j: (0, 0)),
            pl.BlockSpec((None, d, th), lambda i, j: (layer, 0, j)),
            pl.BlockSpec((None, d, th), lambda i, j: (layer, 0, j)),
            pl.BlockSpec((None, th, d), lambda i, j: (layer, j, 0)),
        ],
        out_specs=pl.BlockSpec((tm, d), lambda i, j: (i, 0)),
        out_shape=jax.ShapeDtypeStruct((m, d), F32),
        scratch_shapes=[pltpu.VMEM((tm, d), BF16)],
        compiler_params=_cparams("parallel", "arbitrary"),
        name="ffn",
    )(x, g.reshape(1, d), g_out.reshape(1, d), wg, wu, wd)


def _xattn_kernel(x_ref, g_ref, wq_ref, k_ref, v_ref, wo_ref, o_ref, q_ref, ob_ref, *, heads, scale):
    x = x_ref[0]
    xn = (x * _rms_scale(x) * g_ref[...]).astype(BF16)
    q_ref[...] = jnp.dot(xn, wq_ref[...], preferred_element_type=F32).astype(BF16)
    hd = q_ref.shape[1] // heads
    for h in range(heads):
        sl = slice(h * hd, (h + 1) * hd)
        s = lax.dot_general(q_ref[:, sl], k_ref[0, :, sl], (((1,), (1,)), ((), ())),
                            preferred_element_type=F32) * scale
        p = jnp.exp(s - jnp.max(s, axis=-1, keepdims=True))
        p = p / jnp.sum(p, axis=-1, keepdims=True)
        ob_ref[:, sl] = jnp.dot(p.astype(BF16), v_ref[0, :, sl], preferred_element_type=F32).astype(BF16)
    o_ref[0] = x + jnp.dot(ob_ref[...], wo_ref[...], preferred_element_type=F32)


def _xattn(x, g, wq, kv, wo, layer, tm):
    b, s, d = x.shape
    mem = kv.shape[2]
    tm = min(tm, s)
    kern = functools.partial(_xattn_kernel, heads=XA_HEADS, scale=(d // XA_HEADS) ** -0.5)
    resident = lambda: pl.BlockSpec((None, d, d), lambda i, j: (layer, 0, 0), pipeline_mode=pl.Buffered(1))
    return pl.pallas_call(
        kern,
        grid=(b, s // tm),
        in_specs=[
            pl.BlockSpec((1, tm, d), lambda i, j: (i, j, 0)),
            pl.BlockSpec((1, d), lambda i, j: (0, 0)),
            resident(),
            pl.BlockSpec((None, 1, mem, d), lambda i, j: (layer, i, 0, 0)),
            pl.BlockSpec((None, 1, mem, d), lambda i, j: (layer, i, 0, 1)),
            resident(),
        ],
        out_specs=pl.BlockSpec((1, tm, d), lambda i, j: (i, j, 0)),
        out_shape=jax.ShapeDtypeStruct((b, s, d), F32),
        scratch_shapes=[pltpu.VMEM((tm, d), BF16), pltpu.VMEM((tm, d), BF16)],
        compiler_params=_cparams("parallel", "arbitrary"),
        name="xattn",
    )(x, g.reshape(1, d), wq, kv, kv, wo)


def _lru_kernel(xa_ref, ga_ref, cw_ref, cb_ref, wri_ref, br_ref, bi_ref, sp_ref, o_ref, buf_ref, h_ref):
    @pl.when(pl.program_id(1) == 0)
    def _():
        buf_ref[0:SUBLANES, :] = jnp.zeros((SUBLANES, buf_ref.shape[1]), F32)
        h_ref[...] = jnp.zeros(h_ref.shape, F32)

    xc = _causal_conv(xa_ref[0], buf_ref, cw_ref, cb_ref)
    t = xc.shape[0]
    for blk in range(xc.shape[1] // LRU_BLOCK):
        sl = slice(blk * LRU_BLOCK, (blk + 1) * LRU_BLOCK)
        xb = xc[:, sl]
        pre = jnp.dot(xb.astype(BF16), wri_ref[blk], preferred_element_type=F32)
        r_gate = _sigmoid(pre[:, :LRU_BLOCK] + br_ref[:, sl])
        i_gate = _sigmoid(pre[:, LRU_BLOCK:] + bi_ref[:, sl])
        log_a = (-LRU_C) * r_gate * sp_ref[:, sl]
        a = jnp.exp(log_a)
        mult = _sqrt_nonneg(-jnp.tanh(log_a) * (a * a + 1.0))
        h = _linear_scan_rows(a, mult * i_gate * xb, h_ref[0:1, sl])
        h_ref[0:1, sl] = h[t - 1:t, :]
        o_ref[0, :, sl] = (_gelu_tanh(ga_ref[0, :, sl]) * h).astype(BF16)


def _lru(proj, conv_w, conv_b, w_ri, b_r, b_i, sp, tb):
    b, s, _ = proj.shape
    w = conv_w.shape[1]
    tb = min(tb, s)
    vec = lambda: pl.BlockSpec((1, w), lambda i, j: (0, 0))
    return pl.pallas_call(
        _lru_kernel,
        grid=(b, s // tb),
        in_specs=[
            pl.BlockSpec((1, tb, w), lambda i, j: (i, j, 0)),
            pl.BlockSpec((1, tb, w), lambda i, j: (i, j, 1)),
            pl.BlockSpec((CONV_K, w), lambda i, j: (0, 0)),
            vec(),
            pl.BlockSpec(w_ri.shape, lambda i, j: (0, 0, 0)),
            vec(), vec(), vec(),
        ],
        out_specs=pl.BlockSpec((1, tb, w), lambda i, j: (i, j, 0)),
        out_shape=jax.ShapeDtypeStruct((b, s, w), BF16),
        scratch_shapes=[pltpu.VMEM((tb + SUBLANES, w), F32), pltpu.VMEM((SUBLANES, w), F32)],
        compiler_params=_cparams("parallel", "arbitrary"),
        name="lru",
    )(proj, proj, conv_w, conv_b.reshape(1, w), w_ri, b_r.reshape(1, w), b_i.reshape(1, w), sp.reshape(1, w))


def _hgrn_chunk(q, f, v, lb, st):
    c, dk = q.shape
    nsub = c // HGRN_SUB
    qh = _silu(q)
    fg = lb + (1.0 - lb) * _sigmoid(f)
    kh = 1.0 - fg
    g = jnp.log(fg)
    cum = _cumsum(g, 0)
    ex = cum - g
    cum3 = cum.reshape(nsub, HGRN_SUB, dk)
    ex3 = ex.reshape(nsub, HGRN_SUB, dk)
    base3 = jnp.broadcast_to(ex3[:, 0:1, :], cum3.shape)
    q3 = qh.reshape(nsub, HGRN_SUB, dk)
    k3 = kh.reshape(nsub, HGRN_SUB, dk)
    v3 = v.reshape(nsub, HGRN_SUB, dk)
    vb = v.astype(BF16)

    row3 = lax.broadcasted_iota(jnp.int32, cum3.shape, 1)
    terms = []
    for s in range(HGRN_SUB):
        diff = cum3 - cum3[:, s:s + 1, :]
        dec = jnp.exp(jnp.where(row3 >= s, diff, NEG_INF))
        terms.append((dec * q3 * k3[:, s:s + 1, :]).reshape(c, dk))
    stacked = jnp.concatenate(terms, axis=0).astype(BF16)
    ones = jnp.ones((dk, dk), BF16)
    summed = jnp.dot(stacked, ones, preferred_element_type=F32)
    o3 = jnp.zeros(cum3.shape, F32)
    for s in range(HGRN_SUB):
        o3 = o3 + summed[s * c:(s + 1) * c, :].reshape(nsub, HGRN_SUB, dk) * v3[:, s:s + 1, :]
    o = o3.reshape(c, dk)

    q_loc = (qh * jnp.exp(cum - base3.reshape(c, dk))).astype(BF16)
    pad = jnp.zeros((LANES, dk), F32)
    k_parts = []
    for i in range(1, nsub):
        n = i * HGRN_SUB
        base_i = ex[n:n + 1, :]
        k_parts.append(kh[0:n, :] * jnp.exp(base_i - cum[0:n, :]))
        k_parts.append(pad[0:LANES - n, :])
    k_hat = jnp.concatenate(k_parts, axis=0).astype(BF16)
    a_all = lax.dot_general(q_loc, k_hat, (((1,), (1,)), ((), ())), preferred_element_type=F32)
    a_rows = [jnp.zeros((HGRN_SUB, c), F32)]
    for i in range(1, nsub):
        a_rows.append(a_all[i * HGRN_SUB:(i + 1) * HGRN_SUB, (i - 1) * LANES:(i - 1) * LANES + c])
    a_off = jnp.concatenate(a_rows, axis=0).astype(BF16)
    o = o + jnp.dot(a_off, vb, preferred_element_type=F32)

    q_in = (qh * jnp.exp(cum)).astype(BF16)
    o = o + lax.dot_general(q_in, st.astype(BF16), (((1,), (1,)), ((), ())), preferred_element_type=F32)
    last = cum[c - 1:c, :]
    k_out = (kh * jnp.exp(last - cum)).astype(BF16)
    st_new = st * jnp.exp(last) + lax.dot_general(vb, k_out, (((0,), (0,)), ((), ())),
                                                  preferred_element_type=F32)
    return o, st_new


def _hgrn_kernel(q_ref, f_ref, v_ref, gb_ref, lb_ref, gn_ref, o_ref, st_ref):
    @pl.when(pl.program_id(2) == 0)
    def _():
        st_ref[...] = jnp.zeros(st_ref.shape, F32)

    lb = lb_ref[...]
    st = st_ref[...]
    for ci in range(q_ref.shape[1] // HGRN_CHUNK):
        rows = slice(ci * HGRN_CHUNK, (ci + 1) * HGRN_CHUNK)
        o, st = _hgrn_chunk(q_ref[0, rows, :], f_ref[0, rows, :], v_ref[0, rows, :], lb, st)
        o = o * _rms_scale(o) * gn_ref[...]
        o_ref[0, rows, :] = (o * _silu(gb_ref[0, rows, :])).astype(BF16)
    st_ref[...] = st


def _hgrn(proj, lower_bound, head_norm, col0, heads, tb):
    b, s, _ = proj.shape
    tb = min(tb, s)
    c0 = col0 // HGRN_DK
    part = lambda p: pl.BlockSpec((1, tb, HGRN_DK), lambda i, h, j: (i, j, c0 + p * heads + h))
    vec = lambda: pl.BlockSpec((1, HGRN_DK), lambda i, h, j: (0, h))
    w = heads * HGRN_DK
    return pl.pallas_call(
        _hgrn_kernel,
        grid=(b, heads, s // tb),
        in_specs=[part(0), part(1), part(2), part(3), vec(), vec()],
        out_specs=pl.BlockSpec((1, tb, HGRN_DK), lambda i, h, j: (i, j, h)),
        out_shape=jax.ShapeDtypeStruct((b, s, w), BF16),
        scratch_shapes=[pltpu.VMEM((HGRN_DK, HGRN_DK), F32)],
        compiler_params=_cparams("parallel", "parallel", "arbitrary"),
        name="hgrn",
    )(proj, proj, proj, proj, lower_bound.reshape(1, w), head_norm.reshape(1, w))


def _out_proj2_kernel(a1_ref, a2_ref, w1_ref, w2_ref, r_ref, o_ref):
    acc = jnp.dot(a1_ref[...], w1_ref[...], preferred_element_type=F32)
    acc = acc + jnp.dot(a2_ref[...], w2_ref[...], preferred_element_type=F32)
    o_ref[...] = r_ref[...] + acc


def _out_proj2(a1, a2, w, layer, res, tm, tn):
    m, k1 = a1.shape
    assert a2.shape[1] == k1
    n = w.shape[2]
    tm = min(tm, m)
    tn = min(tn, n)
    return pl.pallas_call(
        _out_proj2_kernel,
        grid=(m // tm, n // tn),
        in_specs=[
            pl.BlockSpec((tm, k1), lambda i, j: (i, 0)),
            pl.BlockSpec((tm, k1), lambda i, j: (i, 0)),
            pl.BlockSpec((None, k1, tn), lambda i, j: (layer, 0, j)),
            pl.BlockSpec((None, k1, tn), lambda i, j: (layer, 1, j)),
            pl.BlockSpec((tm, tn), lambda i, j: (i, j)),
        ],
        out_specs=pl.BlockSpec((tm, tn), lambda i, j: (i, j)),
        out_shape=jax.ShapeDtypeStruct((m, n), F32),
        compiler_params=_cparams("parallel", "arbitrary"),
        name="ab_out_proj",
    )(a1, a2, w, w, res)


def _split3(x):
    hi = x.astype(BF16).astype(F32)
    rem = x - hi
    mid = rem.astype(BF16).astype(F32)
    return hi, mid, rem - mid


def _ssd_scan_block(z_ref, x_ref, b_ref, c_ref, dtr_ref, conv_refs, dbr_ref, anr_ref, dsk_ref, nw_ref,
                    tri_ref, sel_ref, st_ref, y_ref, buf_refs, tail_refs):
    tb = x_ref.shape[1]
    t = tri_ref.shape[0]
    n = st_ref.shape[0]
    hpg = dbr_ref.shape[0]
    (cwx_ref, cbx_ref, cwb_ref, cbb_ref, cwc_ref, cbc_ref) = conv_refs
    xs_all = _silu(_causal_conv(x_ref[0], buf_refs[0], cwx_ref, cbx_ref))
    bm_all = _silu(_causal_conv(b_ref[0], buf_refs[1], cwb_ref, cbb_ref))
    cm_all = _silu(_causal_conv(c_ref[0], buf_refs[2], cwc_ref, cbc_ref))
    for buf_ref, tail_ref in zip(buf_refs, tail_refs):
        tail_ref[...] = buf_ref[0:SUBLANES, :]
    dtr_all = _softplus(dtr_ref[0, 0] + dbr_ref[...])
    causal = (lax.broadcasted_iota(jnp.int32, (t, t), 0) >= lax.broadcasted_iota(jnp.int32, (t, t), 1))
    low_t = lax.broadcasted_iota(jnp.int32, (t, LANES), 1) < SSD_HEADDIM
    low_n = lax.broadcasted_iota(jnp.int32, (n, LANES), 1) < SSD_HEADDIM
    pad = jnp.zeros((sel_ref.shape[0] - 3 * hpg, t), F32)

    for ci in range(tb // t):
        rows = slice(ci * t, (ci + 1) * t)
        xs, bm, cm, dtr = xs_all[rows], bm_all[rows], cm_all[rows], dtr_all[:, rows]
        xsb = xs.astype(BF16)
        dta = dtr * anr_ref[...]
        parts = jnp.concatenate(list(_split3(dta)) + [jnp.zeros_like(dta)], axis=0).astype(BF16)
        c3 = jnp.dot(parts, tri_ref[...], preferred_element_type=F32)
        cum = c3[0:hpg] + c3[hpg:2 * hpg] + c3[2 * hpg:3 * hpg]
        cols = jnp.concatenate(list(_split3(cum)) + [pad], axis=0).astype(BF16)
        bc_all = lax.dot_general(cols, sel_ref[...], (((0,), (0,)), ((), ())), preferred_element_type=F32)
        row_term = cum - jnp.log(dtr)
        last = jnp.broadcast_to(cum[:, t - 1:t], cum.shape)
        row_scale = jnp.exp(last - cum) * dtr
        e_last = jnp.exp(bc_all[t - 1:t, :])
        cb = lax.dot_general(cm.astype(BF16), bm.astype(BF16), (((1,), (1,)), ((), ())),
                             preferred_element_type=F32)
        bt = bm.T

        for pair in range(hpg // 2):
            pc = slice(pair * LANES, (pair + 1) * LANES)
            rhs = jnp.concatenate([xsb[:, pc], st_ref[:, pc].astype(BF16)], axis=0)
            lhs, zl = [], []
            for h in (2 * pair, 2 * pair + 1):
                bc = bc_all[:, h * LANES:(h + 1) * LANES]
                dec = jnp.exp(jnp.where(causal, bc - row_term[h:h + 1, :], NEG_INF))
                lhs.append(jnp.concatenate([cb * dec, cm * jnp.exp(bc)], axis=1).astype(BF16))
                zl.append((bt * row_scale[h:h + 1, :]).astype(BF16))
            y2 = jnp.dot(jnp.concatenate(lhs, axis=0), rhs, preferred_element_type=F32)
            y_ref[rows, pc] = jnp.where(low_t, y2[:t], y2[t:]) + dsk_ref[:, pc] * xs[:, pc]
            z2 = jnp.dot(jnp.concatenate(zl, axis=0), xsb[:, pc], preferred_element_type=F32)
            e_pair = jnp.where(low_n[0:1], e_last[:, 2 * pair * LANES:(2 * pair + 1) * LANES],
                               e_last[:, (2 * pair + 1) * LANES:(2 * pair + 2) * LANES])
            st_ref[:, pc] = st_ref[:, pc] * e_pair + jnp.where(low_n, z2[:n], z2[n:])

    y = y_ref[...] * _silu(z_ref[0])
    return (y * _rms_scale(y) * nw_ref[...]).astype(BF16)


def _ssd_kernel(z_ref, x_ref, b_ref, c_ref, dtr_ref,
                cwx_ref, cbx_ref, cwb_ref, cbb_ref, cwc_ref, cbc_ref,
                dbr_ref, anr_ref, dsk_ref, nw_ref, tri_ref, sel_ref, wo_ref, res_ref,
                o_ref, st_ref, y_ref, yn_ref, bufx_ref, bufb_ref, bufc_ref, tailx_ref, tailb_ref, tailc_ref,
                *, row_blocks, total_blocks):
    kk = pl.program_id(0)
    g = pl.program_id(1)
    slot = lax.rem(kk, 2)
    groups = yn_ref.shape[1]
    bufs = (bufx_ref, bufb_ref, bufc_ref)
    tails = (tailx_ref.at[g], tailb_ref.at[g], tailc_ref.at[g])

    def enter():
        seq_start = lax.rem(kk, row_blocks) == 0

        @pl.when(seq_start)
        def _():
            st_ref[g] = jnp.zeros(st_ref.shape[1:], F32)
            for buf_ref in bufs:
                buf_ref[0:SUBLANES, :] = jnp.zeros((SUBLANES, buf_ref.shape[1]), F32)

        @pl.when(jnp.logical_not(seq_start))
        def _():
            for buf_ref, tail_ref in zip(bufs, tails):
                buf_ref[0:SUBLANES, :] = tail_ref[...]

    def scan():
        yn_ref[slot, g] = _ssd_scan_block(
            z_ref, x_ref, b_ref, c_ref, dtr_ref,
            tuple(r.at[g] for r in (cwx_ref, cbx_ref, cwb_ref, cbb_ref, cwc_ref, cbc_ref)),
            dbr_ref.at[g], anr_ref.at[g], dsk_ref.at[g], nw_ref.at[g], tri_ref, sel_ref, st_ref.at[g], y_ref,
            bufs, tails)

    def project():
        lhs = jnp.concatenate([yn_ref[1 - slot, gg] for gg in range(groups)], axis=1)
        o_ref[0] = res_ref[0] + jnp.dot(lhs, wo_ref[...], preferred_element_type=F32)

    @pl.when(kk == 0)
    def _():
        enter()
        scan()

    @pl.when((kk > 0) & (kk < total_blocks))
    def _():
        enter()
        project()
        scan()

    @pl.when(kk == total_blocks)
    def _():
        project()


def _ssd(proj, dt_raw, conv_w, conv_b, dt_bias, a_neg, d_skip, norm_w, w_out, layer, res, inner, groups):
    b, s, _ = proj.shape
    d = res.shape[2]
    dg = d // groups
    t = min(SSD_CHUNK, s)
    tb = min(SSD_TB, s)
    gw = inner // groups
    n = SSD_STATE
    hpg = SSD_HPG
    assert gw == hpg * SSD_HEADDIM and 2 * SSD_HEADDIM == LANES and tb % t == 0
    dt_row = dt_raw.reshape(b, s, groups, hpg).transpose(0, 2, 3, 1)
    db_r = dt_bias.reshape(groups, hpg, 1)
    an_r = a_neg.reshape(groups, hpg, 1)
    dsk = jnp.repeat(d_skip, SSD_HEADDIM).reshape(groups, 1, gw)
    nw = norm_w.reshape(groups, 1, gw)
    by_group = lambda a, lo, w: (a[:, lo:lo + groups * w].reshape(a.shape[0], groups, w).transpose(1, 0, 2))
    conv_b2 = conv_b.reshape(1, -1)
    conv_parts = []
    for lo, w in ((0, gw), (inner, n), (inner + groups * n, n)):
        conv_parts += [by_group(conv_w, lo, w), by_group(conv_b2, lo, w)]
    tri = (jnp.arange(t)[:, None] <= jnp.arange(t)[None, :]).astype(BF16)
    krow = jnp.arange(LANES)[:, None]
    sel = ((krow < 3 * hpg) & ((krow % hpg) == (jnp.arange(hpg * LANES)[None, :] // LANES))).astype(BF16)
    xoff, boff, coff = inner // gw, (2 * inner) // n, (2 * inner + groups * n) // n
    nk = s // tb
    total = b * nk

    def cur(kk):
        c = jnp.minimum(kk, total - 1)
        return c // nk, c % nk

    def prev(kk):
        p = jnp.maximum(kk - 1, 0)
        return p // nk, p % nk

    out_col = lambda kk, g: jnp.where(kk == 0, 0, g)
    const = lambda a: pl.BlockSpec(a.shape, lambda kk, g: (0,) * a.ndim)
    small = conv_parts + [db_r, an_r, dsk, nw, tri, sel]
    return pl.pallas_call(
        functools.partial(_ssd_kernel, row_blocks=nk, total_blocks=total),
        grid=(total + 1, groups),
        in_specs=[
            pl.BlockSpec((1, tb, gw), lambda kk, g: (*cur(kk), g)),
            pl.BlockSpec((1, tb, gw), lambda kk, g: (*cur(kk), xoff + g)),
            pl.BlockSpec((1, tb, n), lambda kk, g: (*cur(kk), boff + g)),
            pl.BlockSpec((1, tb, n), lambda kk, g: (*cur(kk), coff + g)),
            pl.BlockSpec((1, 1, hpg, tb), lambda kk, g: (cur(kk)[0], g, 0, cur(kk)[1])),
            *[const(a) for a in small],
            pl.BlockSpec((None, inner, dg), lambda kk, g: (layer, 0, g)),
            pl.BlockSpec((1, tb, dg), lambda kk, g: (*prev(kk), out_col(kk, g))),
        ],
        out_specs=pl.BlockSpec((1, tb, dg), lambda kk, g: (*prev(kk), out_col(kk, g))),
        out_shape=jax.ShapeDtypeStruct((b, s, d), F32),
        scratch_shapes=[
            pltpu.VMEM((groups, n, gw), F32),
            pltpu.VMEM((tb, gw), F32),
            pltpu.VMEM((2, groups, tb, gw), BF16),
            pltpu.VMEM((tb + SUBLANES, gw), F32),
            pltpu.VMEM((tb + SUBLANES, n), F32),
            pltpu.VMEM((tb + SUBLANES, n), F32),
            pltpu.VMEM((groups, SUBLANES, gw), F32),
            pltpu.VMEM((groups, SUBLANES, n), F32),
            pltpu.VMEM((groups, SUBLANES, n), F32),
        ],
        compiler_params=_cparams("arbitrary", "arbitrary"),
        name="ssd",
    )(proj, proj, proj, proj, dt_row, *small, w_out, res)


TM = 1024
TN = 1024
SQ_TM = 512
SQ_TN = 2048
FFN_TH = 512
XA_TM = 512
LRU_TB = 256
HGRN_TB = 512
SSD_TB = 512


def kernel(x, mem, norm_mix, norm_xattn, norm_ffn, norm_mem, norm_final, ab_w_in, ab_w_out, lru_conv_w, lru_conv_b, lru_w_r, lru_b_r, lru_w_i, lru_b_i, lru_lambda, hgrn_lower_bounds, hgrn_norm, ssd_w_in, ssd_w_out, ssd_conv_w, ssd_conv_b, ssd_dt_bias, ssd_a_log, ssd_d, ssd_norm, xa_w_q, xa_w_kv, xa_w_o, ffn_w_gate, ffn_w_up, ffn_w_down):
    bsz, seq, d = x.shape
    depth = norm_mix.shape[0]
    m = bsz * seq
    lru_w = lru_conv_w.shape[2]
    hgrn_w = hgrn_norm.shape[1]
    ssd_heads = ssd_a_log.shape[1]
    ssd_inner = ssd_norm.shape[1]
    ssd_groups = ssd_heads // SSD_HPG
    ssd_main = ssd_w_in.shape[2] - ssd_heads

    sm = jax.nn.softmax(hgrn_lower_bounds.astype(F32), axis=0)
    lower_bounds = jnp.cumsum(sm, axis=0) - sm[0]
    lru_sp = jax.nn.softplus(-lru_lambda.astype(F32))
    ssd_a_neg = -jnp.exp(ssd_a_log.astype(F32))
    lru_w_ri = jnp.concatenate([lru_w_r, lru_w_i], axis=-1).astype(BF16)
    ab_in_b = _cast_bf16(ab_w_in)
    ab_out_b = _cast_bf16(ab_w_out)
    ssd_in_b = ssd_w_in[:, :, :ssd_main].astype(BF16)
    ssd_dt_b = ssd_w_in[:, :, ssd_main:].astype(BF16)
    ssd_out_b = _cast_bf16(ssd_w_out)
    wq_b = _cast_bf16(xa_w_q)
    wo_b = _cast_bf16(xa_w_o)
    wg_b = _cast_bf16(ffn_w_gate)
    wu_b = _cast_bf16(ffn_w_up)
    wd_b = _cast_bf16(ffn_w_down)

    x2 = x.reshape(m, d)
    mem_len = mem.shape[1]
    kv = _kv_proj(mem.reshape(bsz * mem_len, d), norm_mem, xa_w_kv, TN).reshape(depth, bsz, mem_len, 2 * d)

    for layer in range(depth):
        if layer % 2 == 0:
            e = layer // 2
            proj = _norm_matmul(x2, norm_mix[layer], ab_in_b, e, F32, TM, TN, "ab_in_proj")
            proj = proj.reshape(bsz, seq, -1)
            ya = _lru(proj, lru_conv_w[e], lru_conv_b[e], lru_w_ri[e], lru_b_r[e], lru_b_i[e], lru_sp[e], LRU_TB)
            yb = _hgrn(proj, lower_bounds[e], hgrn_norm[e], 2 * lru_w, hgrn_w // HGRN_DK, HGRN_TB)
            x2 = _out_proj2(ya.reshape(m, lru_w), yb.reshape(m, hgrn_w), ab_out_b, e, x2, SQ_TM, SQ_TN)
        else:
            o = layer // 2
            proj, dt_raw = _norm_matmul_side(x2, norm_mix[layer], ssd_in_b, ssd_dt_b, o, TM, TN, "ssd_in_proj")
            x2 = _ssd(proj.reshape(bsz, seq, ssd_main), dt_raw.reshape(bsz, seq, ssd_heads),
                      ssd_conv_w[o], ssd_conv_b[o], ssd_dt_bias[o], ssd_a_neg[o], ssd_d[o], ssd_norm[o],
                      ssd_out_b, o, x2.reshape(bsz, seq, d), ssd_inner, ssd_groups).reshape(m, d)
        x2 = _xattn(x2.reshape(bsz, seq, d), norm_xattn[layer], wq_b, kv, wo_b, layer, XA_TM).reshape(m, d)
        g_out = norm_final if layer == depth - 1 else None
        x2 = _ffn(x2, norm_ffn[layer], wg_b, wu_b, wd_b, layer, TM, FFN_TH, g_out)
    return x2.reshape(bsz, seq, d)
```

```python
import functools
import math

import jax
import jax.numpy as jnp
from jax import lax
from jax.experimental import pallas as pl
from jax.experimental.pallas import tpu as pltpu

F32 = jnp.float32
BF16 = jnp.bfloat16
EPS = 1e-6
NEG_INF = float("-inf")

VMEM_LIMIT_BYTES = 56 * 1024 * 1024
CAST_BLOCK_BYTES = 4 * 1024 * 1024
SUBLANES = 8
LANES = 128

CONV_K = 4
LRU_BLOCK = 128
LRU_C = 8.0
HGRN_DK = 128
HGRN_CHUNK = 64
HGRN_SUB = SUBLANES
SSD_HEADDIM = 64
SSD_HPG = 8
SSD_STATE = 128
SSD_CHUNK = 128
XA_HEADS = 4


def _cparams(*sem):
    return pltpu.CompilerParams(dimension_semantics=sem, vmem_limit_bytes=VMEM_LIMIT_BYTES)


def _rms_scale(x):
    return lax.rsqrt(jnp.mean(x * x, axis=-1, keepdims=True) + EPS)


def _sigmoid(x):
    return 0.5 * (1.0 + jnp.tanh(0.5 * x))


def _silu(x):
    return x * _sigmoid(x)


def _gelu_tanh(x):
    c = math.sqrt(2.0 / math.pi)
    return x * (0.5 * (1.0 + jnp.tanh(c * (x + 0.044715 * (x * x * x)))))


def _softplus(x):
    return jnp.maximum(x, 0.0) + jnp.log1p(jnp.exp(-jnp.abs(x)))


def _linear_scan_rows(a, b, h0):
    t, c = a.shape
    groups = t // SUBLANES
    a3 = a.reshape(groups, SUBLANES, c)
    b3 = b.reshape(groups, SUBLANES, c)
    row = lax.broadcasted_iota(jnp.int32, a3.shape, 1)
    s = 1
    while s < SUBLANES:
        keep = row >= s
        a_sh = pltpu.roll(a3, s, 1)
        b_sh = pltpu.roll(b3, s, 1)
        b3 = jnp.where(keep, a3 * b_sh + b3, b3)
        a3 = jnp.where(keep, a3 * a_sh, a3)
        s *= 2
    out = []
    carry = h0
    for g in range(groups):
        h = a3[g] * carry + b3[g]
        out.append(h)
        carry = h[SUBLANES - 1:SUBLANES, :]
    return jnp.concatenate(out, axis=0)


def _sqrt_nonneg(y):
    return jnp.where(y > 0.0, y * lax.rsqrt(y), 0.0)


def _cumsum(x, axis):
    n = x.shape[axis]
    idx = lax.broadcasted_iota(jnp.int32, x.shape, axis)
    s = 1
    while s < n:
        x = jnp.where(idx >= s, x + pltpu.roll(x, s, axis), x)
        s *= 2
    return x


def _causal_conv(u, buf_ref, w_ref, b_ref):
    t = u.shape[0]
    buf_ref[SUBLANES:SUBLANES + t, :] = u
    out = b_ref[...] + w_ref[3:4, :] * u
    for j in range(1, CONV_K):
        out = out + w_ref[CONV_K - 1 - j:CONV_K - j, :] * buf_ref[SUBLANES - j:SUBLANES - j + t, :]
    buf_ref[0:SUBLANES, :] = u[t - SUBLANES:t, :]
    return out


def _norm_matmul_kernel(x_ref, g_ref, w_ref, o_ref, xn_ref):
    @pl.when(pl.program_id(1) == 0)
    def _():
        x = x_ref[...]
        xn_ref[...] = (x * _rms_scale(x) * g_ref[...]).astype(BF16)

    o_ref[...] = jnp.dot(xn_ref[...], w_ref[...], preferred_element_type=F32).astype(o_ref.dtype)


def _norm_matmul(x, g, w, layer, out_dtype, tm, tn, name):
    m, k = x.shape
    n = w.shape[2]
    tm = min(tm, m)
    tn = min(tn, n)
    return pl.pallas_call(
        _norm_matmul_kernel,
        grid=(m // tm, n // tn),
        in_specs=[
            pl.BlockSpec((tm, k), lambda i, j: (i, 0)),
            pl.BlockSpec((1, k), lambda i, j: (0, 0)),
            pl.BlockSpec((None, k, tn), lambda i, j: (layer, 0, j)),
        ],
        out_specs=pl.BlockSpec((tm, tn), lambda i, j: (i, j)),
        out_shape=jax.ShapeDtypeStruct((m, n), out_dtype),
        scratch_shapes=[pltpu.VMEM((tm, k), BF16)],
        compiler_params=_cparams("parallel", "arbitrary"),
        name=name,
    )(x, g.reshape(1, k), w)


def _norm_matmul_side_kernel(x_ref, g_ref, w_ref, ws_ref, o_ref, os_ref, xn_ref):
    @pl.when(pl.program_id(1) == 0)
    def _():
        x = x_ref[...]
        xn = (x * _rms_scale(x) * g_ref[...]).astype(BF16)
        xn_ref[...] = xn
        os_ref[...] = jnp.dot(xn, ws_ref[...], preferred_element_type=F32)

    o_ref[...] = jnp.dot(xn_ref[...], w_ref[...], preferred_element_type=F32).astype(o_ref.dtype)


def _norm_matmul_side(x, g, w, w_side, layer, tm, tn, name):
    m, k = x.shape
    n = w.shape[2]
    ns = w_side.shape[2]
    tm = min(tm, m)
    tn = min(tn, n)
    return pl.pallas_call(
        _norm_matmul_side_kernel,
        grid=(m // tm, n // tn),
        in_specs=[
            pl.BlockSpec((tm, k), lambda i, j: (i, 0)),
            pl.BlockSpec((1, k), lambda i, j: (0, 0)),
            pl.BlockSpec((None, k, tn), lambda i, j: (layer, 0, j)),
            pl.BlockSpec((None, k, ns), lambda i, j: (layer, 0, 0)),
        ],
        out_specs=[pl.BlockSpec((tm, tn), lambda i, j: (i, j)), pl.BlockSpec((tm, ns), lambda i, j: (i, 0))],
        out_shape=[jax.ShapeDtypeStruct((m, n), F32), jax.ShapeDtypeStruct((m, ns), F32)],
        scratch_shapes=[pltpu.VMEM((tm, k), BF16)],
        compiler_params=_cparams("parallel", "arbitrary"),
        name=name,
    )(x, g.reshape(1, k), w, w_side)


def _kv_proj_kernel(x_ref, g_ref, w_ref, o_ref, xn_ref):
    @pl.when((pl.program_id(0) == 0) & (pl.program_id(1) == 0))
    def _():
        x = x_ref[...]
        xn_ref[...] = (x * _rms_scale(x) * g_ref[...]).astype(BF16)

    o_ref[...] = jnp.dot(xn_ref[...], w_ref[...].astype(BF16), preferred_element_type=F32).astype(o_ref.dtype)


def _kv_proj(x, g, w, tn):
    m, k = x.shape
    nl, _, n = w.shape
    return pl.pallas_call(
        _kv_proj_kernel,
        grid=(nl, n // tn),
        in_specs=[
            pl.BlockSpec((m, k), lambda l, j: (0, 0)),
            pl.BlockSpec((1, k), lambda l, j: (0, 0)),
            pl.BlockSpec((None, k, tn), lambda l, j: (l, 0, j)),
        ],
        out_specs=pl.BlockSpec((None, m, tn), lambda l, j: (l, 0, j)),
        out_shape=jax.ShapeDtypeStruct((nl, m, n), BF16),
        scratch_shapes=[pltpu.VMEM((m, k), BF16)],
        compiler_params=_cparams("arbitrary", "arbitrary"),
        name="kv_proj",
    )(x, g.reshape(1, k), w)


def _cast_kernel(x_ref, o_ref):
    o_ref[...] = x_ref[...].astype(BF16)


def _cast_bf16(w):
    nl, k, n = w.shape
    tk = SUBLANES
    while tk * 2 <= k and k % (tk * 2) == 0 and tk * 2 * n * 4 <= CAST_BLOCK_BYTES:
        tk *= 2
    return pl.pallas_call(
        _cast_kernel,
        grid=(nl, k // tk),
        in_specs=[pl.BlockSpec((None, tk, n), lambda l, i: (l, i, 0))],
        out_specs=pl.BlockSpec((None, tk, n), lambda l, i: (l, i, 0)),
        out_shape=jax.ShapeDtypeStruct((nl, k, n), BF16),
        compiler_params=_cparams("parallel", "parallel"),
        name="cast_bf16",
    )(w)


def _ffn_kernel(x_ref, g_ref, go_ref, wg_ref, wu_ref, wd_ref, o_ref, xn_ref, *, out_norm):
    @pl.when(pl.program_id(1) == 0)
    def _():
        x = x_ref[...]
        xn_ref[...] = (x * _rms_scale(x) * g_ref[...]).astype(BF16)
        o_ref[...] = x

    xn = xn_ref[...]
    gate = jnp.dot(xn, wg_ref[...], preferred_element_type=F32)
    up = jnp.dot(xn, wu_ref[...], preferred_element_type=F32)
    hid = (_silu(gate) * up).astype(BF16)
    o_ref[...] += jnp.dot(hid, wd_ref[...], preferred_element_type=F32)

    if out_norm:
        @pl.when(pl.program_id(1) == pl.num_programs(1) - 1)
        def _():
            y = o_ref[...]
            o_ref[...] = y * _rms_scale(y) * go_ref[...]


def _ffn(x, g, wg, wu, wd, layer, tm, th, g_out=None):
    m, d = x.shape
    hdim = wg.shape[2]
    tm = min(tm, m)
    out_norm = g_out is not None
    g_out = g if g_out is None else g_out
    return pl.pallas_call(
        functools.partial(_ffn_kernel, out_norm=out_norm),
        grid=(m // tm, hdim // th),
        in_specs=[
            pl.BlockSpec((tm, d), lambda i, j: (i, 0)),
            pl.BlockSpec((1, d), lambda i, j: (0, 0)),
            pl.BlockSpec((1, d), lambda i, j: (0, 0)),
            pl.BlockSpec((None, d, th), lambda i, j: (layer, 0, j)),
            pl.BlockSpec((None, d, th), lambda i, j: (layer, 0, j)),
            pl.BlockSpec((None, th, d), lambda i, j: (layer, j, 0)),
        ],
        out_specs=pl.BlockSpec((tm, d), lambda i, j: (i, 0)),
        out_shape=jax.ShapeDtypeStruct((m, d), F32),
        scratch_shapes=[pltpu.VMEM((tm, d), BF16)],
        compiler_params=_cparams("parallel", "arbitrary"),
        name="ffn",
    )(x, g.reshape(1, d), g_out.reshape(1, d), wg, wu, wd)


def _xattn_kernel(x_ref, g_ref, wq_ref, k_ref, v_ref, wo_ref, o_ref, q_ref, ob_ref, *, heads, scale):
    x = x_ref[0]
    xn = (x * _rms_scale(x) * g_ref[...]).astype(BF16)
    q_ref[...] = jnp.dot(xn, wq_ref[...], preferred_element_type=F32).astype(BF16)
    hd = q_ref.shape[1] // heads
    for h in range(heads):
        sl = slice(h * hd, (h + 1) * hd)
        s = lax.dot_general(q_ref[:, sl], k_ref[0, :, sl], (((1,), (1,)), ((), ())),
                            preferred_element_type=F32) * scale
        p = jnp.exp(s - jnp.max(s, axis=-1, keepdims=True))
        p = p / jnp.sum(p, axis=-1, keepdims=True)
        ob_ref[:, sl] = jnp.dot(p.astype(BF16), v_ref[0, :, sl], preferred_element_type=F32).astype(BF16)
    o_ref[0] = x + jnp.dot(ob_ref[...], wo_ref[...], preferred_element_type=F32)


def _xattn(x, g, wq, kv, wo, layer, tm):
    b, s, d = x.shape
    mem = kv.shape[2]
    tm = min(tm, s)
    kern = functools.partial(_xattn_kernel, heads=XA_HEADS, scale=(d // XA_HEADS) ** -0.5)
    resident = lambda: pl.BlockSpec((None, d, d), lambda i, j: (layer, 0, 0), pipeline_mode=pl.Buffered(1))
    return pl.pallas_call(
        kern,
        grid=(b, s // tm),
        in_specs=[
            pl.BlockSpec((1, tm, d), lambda i, j: (i, j, 0)),
            pl.BlockSpec((1, d), lambda i, j: (0, 0)),
            resident(),
            pl.BlockSpec((None, 1, mem, d), lambda i, j: (layer, i, 0, 0)),
            pl.BlockSpec((None, 1, mem, d), lambda i, j: (layer, i, 0, 1)),
            resident(),
        ],
        out_specs=pl.BlockSpec((1, tm, d), lambda i, j: (i, j, 0)),
        out_shape=jax.ShapeDtypeStruct((b, s, d), F32),
        scratch_shapes=[pltpu.VMEM((tm, d), BF16), pltpu.VMEM((tm, d), BF16)],
        compiler_params=_cparams("parallel", "arbitrary"),
        name="xattn",
    )(x, g.reshape(1, d), wq, kv, kv, wo)


def _lru_kernel(xa_ref, ga_ref, cw_ref, cb_ref, wri_ref, br_ref, bi_ref, sp_ref, o_ref, buf_ref, h_ref):
    @pl.when(pl.program_id(1) == 0)
    def _():
        buf_ref[0:SUBLANES, :] = jnp.zeros((SUBLANES, buf_ref.shape[1]), F32)
        h_ref[...] = jnp.zeros(h_ref.shape, F32)

    xc = _causal_conv(xa_ref[0], buf_ref, cw_ref, cb_ref)
    t = xc.shape[0]
    for blk in range(xc.shape[1] // LRU_BLOCK):
        sl = slice(blk * LRU_BLOCK, (blk + 1) * LRU_BLOCK)
        xb = xc[:, sl]
        pre = jnp.dot(xb.astype(BF16), wri_ref[blk], preferred_element_type=F32)
        r_gate = _sigmoid(pre[:, :LRU_BLOCK] + br_ref[:, sl])
        i_gate = _sigmoid(pre[:, LRU_BLOCK:] + bi_ref[:, sl])
        log_a = (-LRU_C) * r_gate * sp_ref[:, sl]
        a = jnp.exp(log_a)
        mult = _sqrt_nonneg(-jnp.tanh(log_a) * (a * a + 1.0))
        h = _linear_scan_rows(a, mult * i_gate * xb, h_ref[0:1, sl])
        h_ref[0:1, sl] = h[t - 1:t, :]
        o_ref[0, :, sl] = (_gelu_tanh(ga_ref[0, :, sl]) * h).astype(BF16)


def _lru(proj, conv_w, conv_b, w_ri, b_r, b_i, sp, tb):
    b, s, _ = proj.shape
    w = conv_w.shape[1]
    tb = min(tb, s)
    vec = lambda: pl.BlockSpec((1, w), lambda i, j: (0, 0))
    return pl.pallas_call(
        _lru_kernel,
        grid=(b, s // tb),
        in_specs=[
            pl.BlockSpec((1, tb, w), lambda i, j: (i, j, 0)),
            pl.BlockSpec((1, tb, w), lambda i, j: (i, j, 1)),
            pl.BlockSpec((CONV_K, w), lambda i, j: (0, 0)),
            vec(),
            pl.BlockSpec(w_ri.shape, lambda i, j: (0, 0, 0)),
            vec(), vec(), vec(),
        ],
        out_specs=pl.BlockSpec((1, tb, w), lambda i, j: (i, j, 0)),
        out_shape=jax.ShapeDtypeStruct((b, s, w), BF16),
        scratch_shapes=[pltpu.VMEM((tb + SUBLANES, w), F32), pltpu.VMEM((SUBLANES, w), F32)],
        compiler_params=_cparams("parallel", "arbitrary"),
        name="lru",
    )(proj, proj, conv_w, conv_b.reshape(1, w), w_ri, b_r.reshape(1, w), b_i.reshape(1, w), sp.reshape(1, w))


def _hgrn_chunk(q, f, v, lb, st):
    c, dk = q.shape
    nsub = c // HGRN_SUB
    qh = _silu(q)
    fg = lb + (1.0 - lb) * _sigmoid(f)
    kh = 1.0 - fg
    g = jnp.log(fg)
    cum = _cumsum(g, 0)
    ex = cum - g
    cum3 = cum.reshape(nsub, HGRN_SUB, dk)
    ex3 = ex.reshape(nsub, HGRN_SUB, dk)
    base3 = jnp.broadcast_to(ex3[:, 0:1, :], cum3.shape)
    q3 = qh.reshape(nsub, HGRN_SUB, dk)
    k3 = kh.reshape(nsub, HGRN_SUB, dk)
    v3 = v.reshape(nsub, HGRN_SUB, dk)
    vb = v.astype(BF16)

    row3 = lax.broadcasted_iota(jnp.int32, cum3.shape, 1)
    terms = []
    for s in range(HGRN_SUB):
        diff = cum3 - cum3[:, s:s + 1, :]
        dec = jnp.exp(jnp.where(row3 >= s, diff, NEG_INF))
        terms.append((dec * q3 * k3[:, s:s + 1, :]).reshape(c, dk))
    stacked = jnp.concatenate(terms, axis=0).astype(BF16)
    ones = jnp.ones((dk, dk), BF16)
    summed = jnp.dot(stacked, ones, preferred_element_type=F32)
    o3 = jnp.zeros(cum3.shape, F32)
    for s in range(HGRN_SUB):
        o3 = o3 + summed[s * c:(s + 1) * c, :].reshape(nsub, HGRN_SUB, dk) * v3[:, s:s + 1, :]
    o = o3.reshape(c, dk)

    q_loc = (qh * jnp.exp(cum - base3.reshape(c, dk))).astype(BF16)
    pad = jnp.zeros((LANES, dk), F32)
    k_parts = []
    for i in range(1, nsub):
        n = i * HGRN_SUB
        base_i = ex[n:n + 1, :]
        k_parts.append(kh[0:n, :] * jnp.exp(base_i - cum[0:n, :]))
        k_parts.append(pad[0:LANES - n, :])
    k_hat = jnp.concatenate(k_parts, axis=0).astype(BF16)
    a_all = lax.dot_general(q_loc, k_hat, (((1,), (1,)), ((), ())), preferred_element_type=F32)
    a_rows = [jnp.zeros((HGRN_SUB, c), F32)]
    for i in range(1, nsub):
        a_rows.append(a_all[i * HGRN_SUB:(i + 1) * HGRN_SUB, (i - 1) * LANES:(i - 1) * LANES + c])
    a_off = jnp.concatenate(a_rows, axis=0).astype(BF16)
    o = o + jnp.dot(a_off, vb, preferred_element_type=F32)

    q_in = (qh * jnp.exp(cum)).astype(BF16)
    o = o + lax.dot_general(q_in, st.astype(BF16), (((1,), (1,)), ((), ())), preferred_element_type=F32)
    last = cum[c - 1:c, :]
    k_out = (kh * jnp.exp(last - cum)).astype(BF16)
    st_new = st * jnp.exp(last) + lax.dot_general(vb, k_out, (((0,), (0,)), ((), ())),
                                                  preferred_element_type=F32)
    return o, st_new


def _hgrn_kernel(q_ref, f_ref, v_ref, gb_ref, lb_ref, gn_ref, o_ref, st_ref):
    @pl.when(pl.program_id(2) == 0)
    def _():
        st_ref[...] = jnp.zeros(st_ref.shape, F32)

    lb = lb_ref[...]
    st = st_ref[...]
    for ci in range(q_ref.shape[1] // HGRN_CHUNK):
        rows = slice(ci * HGRN_CHUNK, (ci + 1) * HGRN_CHUNK)
        o, st = _hgrn_chunk(q_ref[0, rows, :], f_ref[0, rows, :], v_ref[0, rows, :], lb, st)
        o = o * _rms_scale(o) * gn_ref[...]
        o_ref[0, rows, :] = (o * _silu(gb_ref[0, rows, :])).astype(BF16)
    st_ref[...] = st


def _hgrn(proj, lower_bound, head_norm, col0, heads, tb):
    b, s, _ = proj.shape
    tb = min(tb, s)
    c0 = col0 // HGRN_DK
    part = lambda p: pl.BlockSpec((1, tb, HGRN_DK), lambda i, h, j: (i, j, c0 + p * heads + h))
    vec = lambda: pl.BlockSpec((1, HGRN_DK), lambda i, h, j: (0, h))
    w = heads * HGRN_DK
    return pl.pallas_call(
        _hgrn_kernel,
        grid=(b, heads, s // tb),
        in_specs=[part(0), part(1), part(2), part(3), vec(), vec()],
        out_specs=pl.BlockSpec((1, tb, HGRN_DK), lambda i, h, j: (i, j, h)),
        out_shape=jax.ShapeDtypeStruct((b, s, w), BF16),
        scratch_shapes=[pltpu.VMEM((HGRN_DK, HGRN_DK), F32)],
        compiler_params=_cparams("parallel", "parallel", "arbitrary"),
        name="hgrn",
    )(proj, proj, proj, proj, lower_bound.reshape(1, w), head_norm.reshape(1, w))


def _out_proj2_kernel(a1_ref, a2_ref, w1_ref, w2_ref, r_ref, o_ref):
    acc = jnp.dot(a1_ref[...], w1_ref[...], preferred_element_type=F32)
    acc = acc + jnp.dot(a2_ref[...], w2_ref[...], preferred_element_type=F32)
    o_ref[...] = r_ref[...] + acc


def _out_proj2(a1, a2, w, layer, res, tm, tn):
    m, k1 = a1.shape
    assert a2.shape[1] == k1
    n = w.shape[2]
    tm = min(tm, m)
    tn = min(tn, n)
    return pl.pallas_call(
        _out_proj2_kernel,
        grid=(m // tm, n // tn),
        in_specs=[
            pl.BlockSpec((tm, k1), lambda i, j: (i, 0)),
            pl.BlockSpec((tm, k1), lambda i, j: (i, 0)),
            pl.BlockSpec((None, k1, tn), lambda i, j: (layer, 0, j)),
            pl.BlockSpec((None, k1, tn), lambda i, j: (layer, 1, j)),
            pl.BlockSpec((tm, tn), lambda i, j: (i, j)),
        ],
        out_specs=pl.BlockSpec((tm, tn), lambda i, j: (i, j)),
        out_shape=jax.ShapeDtypeStruct((m, n), F32),
        compiler_params=_cparams("parallel", "arbitrary"),
        name="ab_out_proj",
    )(a1, a2, w, w, res)


def _split3(x):
    hi = x.astype(BF16).astype(F32)
    rem = x - hi
    mid = rem.astype(BF16).astype(F32)
    return hi, mid, rem - mid


def _ssd_scan_block(z_ref, x_ref, b_ref, c_ref, dtr_ref, conv_refs, dbr_ref, anr_ref, dsk_ref, nw_ref,
                    tri_ref, sel_ref, st_ref, y_ref, buf_refs, tail_refs):
    tb = x_ref.shape[1]
    t = tri_ref.shape[0]
    n = st_ref.shape[0]
    hpg = dbr_ref.shape[0]
    (cwx_ref, cbx_ref, cwb_ref, cbb_ref, cwc_ref, cbc_ref) = conv_refs
    xs_all = _silu(_causal_conv(x_ref[0], buf_refs[0], cwx_ref, cbx_ref))
    bm_all = _silu(_causal_conv(b_ref[0], buf_refs[1], cwb_ref, cbb_ref))
    cm_all = _silu(_causal_conv(c_ref[0], buf_refs[2], cwc_ref, cbc_ref))
    for buf_ref, tail_ref in zip(buf_refs, tail_refs):
        tail_ref[...] = buf_ref[0:SUBLANES, :]
    dtr_all = _softplus(dtr_ref[0, 0] + dbr_ref[...])
    causal = (lax.broadcasted_iota(jnp.int32, (t, t), 0) >= lax.broadcasted_iota(jnp.int32, (t, t), 1))
    low_t = lax.broadcasted_iota(jnp.int32, (t, LANES), 1) < SSD_HEADDIM
    low_n = lax.broadcasted_iota(jnp.int32, (n, LANES), 1) < SSD_HEADDIM
    pad = jnp.zeros((sel_ref.shape[0] - 3 * hpg, t), F32)

    for ci in range(tb // t):
        rows = slice(ci * t, (ci + 1) * t)
        xs, bm, cm, dtr = xs_all[rows], bm_all[rows], cm_all[rows], dtr_all[:, rows]
        xsb = xs.astype(BF16)
        dta = dtr * anr_ref[...]
        parts = jnp.concatenate(list(_split3(dta)) + [jnp.zeros_like(dta)], axis=0).astype(BF16)
        c3 = jnp.dot(parts, tri_ref[...], preferred_element_type=F32)
        cum = c3[0:hpg] + c3[hpg:2 * hpg] + c3[2 * hpg:3 * hpg]
        cols = jnp.concatenate(list(_split3(cum)) + [pad], axis=0).astype(BF16)
        bc_all = lax.dot_general(cols, sel_ref[...], (((0,), (0,)), ((), ())), preferred_element_type=F32)
        row_term = cum - jnp.log(dtr)
        last = jnp.broadcast_to(cum[:, t - 1:t], cum.shape)
        row_scale = jnp.exp(last - cum) * dtr
        e_last = jnp.exp(bc_all[t - 1:t, :])
        cb = lax.dot_general(cm.astype(BF16), bm.astype(BF16), (((1,), (1,)), ((), ())),
                             preferred_element_type=F32)
        bt = bm.T

        for pair in range(hpg // 2):
            pc = slice(pair * LANES, (pair + 1) * LANES)
            rhs = jnp.concatenate([xsb[:, pc], st_ref[:, pc].astype(BF16)], axis=0)
            lhs, zl = [], []
            for h in (2 * pair, 2 * pair + 1):
                bc = bc_all[:, h * LANES:(h + 1) * LANES]
                dec = jnp.exp(jnp.where(causal, bc - row_term[h:h + 1, :], NEG_INF))
                lhs.append(jnp.concatenate([cb * dec, cm * jnp.exp(bc)], axis=1).astype(BF16))
                zl.append((bt * row_scale[h:h + 1, :]).astype(BF16))
            y2 = jnp.dot(jnp.concatenate(lhs, axis=0), rhs, preferred_element_type=F32)
            y_ref[rows, pc] = jnp.where(low_t, y2[:t], y2[t:]) + dsk_ref[:, pc] * xs[:, pc]
            z2 = jnp.dot(jnp.concatenate(zl, axis=0), xsb[:, pc], preferred_element_type=F32)
            e_pair = jnp.where(low_n[0:1], e_last[:, 2 * pair * LANES:(2 * pair + 1) * LANES],
                               e_last[:, (2 * pair + 1) * LANES:(2 * pair + 2) * LANES])
            st_ref[:, pc] = st_ref[:, pc] * e_pair + jnp.where(low_n, z2[:n], z2[n:])

    y = y_ref[...] * _silu(z_ref[0])
    return (y * _rms_scale(y) * nw_ref[...]).astype(BF16)


def _ssd_kernel(z_ref, x_ref, b_ref, c_ref, dtr_ref,
                cwx_ref, cbx_ref, cwb_ref, cbb_ref, cwc_ref, cbc_ref,
                dbr_ref, anr_ref, dsk_ref, nw_ref, tri_ref, sel_ref, wo_ref, res_ref,
                o_ref, st_ref, y_ref, yn_ref, bufx_ref, bufb_ref, bufc_ref, tailx_ref, tailb_ref, tailc_ref,
                *, row_blocks, total_blocks):
    kk = pl.program_id(0)
    g = pl.program_id(1)
    slot = lax.rem(kk, 2)
    groups = yn_ref.shape[1]
    bufs = (bufx_ref, bufb_ref, bufc_ref)
    tails = (tailx_ref.at[g], tailb_ref.at[g], tailc_ref.at[g])

    def enter():
        seq_start = lax.rem(kk, row_blocks) == 0

        @pl.when(seq_start)
        def _():
            st_ref[g] = jnp.zeros(st_ref.shape[1:], F32)
            for buf_ref in bufs:
                buf_ref[0:SUBLANES, :] = jnp.zeros((SUBLANES, buf_ref.shape[1]), F32)

        @pl.when(jnp.logical_not(seq_start))
        def _():
            for buf_ref, tail_ref in zip(bufs, tails):
                buf_ref[0:SUBLANES, :] = tail_ref[...]

    def scan():
        yn_ref[slot, g] = _ssd_scan_block(
            z_ref, x_ref, b_ref, c_ref, dtr_ref,
            tuple(r.at[g] for r in (cwx_ref, cbx_ref, cwb_ref, cbb_ref, cwc_ref, cbc_ref)),
            dbr_ref.at[g], anr_ref.at[g], dsk_ref.at[g], nw_ref.at[g], tri_ref, sel_ref, st_ref.at[g], y_ref,
            bufs, tails)

    def project():
        lhs = jnp.concatenate([yn_ref[1 - slot, gg] for gg in range(groups)], axis=1)
        o_ref[0] = res_ref[0] + jnp.dot(lhs, wo_ref[...], preferred_element_type=F32)

    @pl.when(kk == 0)
    def _():
        enter()
        scan()

    @pl.when((kk > 0) & (kk < total_blocks))
    def _():
        enter()
        project()
        scan()

    @pl.when(kk == total_blocks)
    def _():
        project()


def _ssd(proj, dt_raw, conv_w, conv_b, dt_bias, a_neg, d_skip, norm_w, w_out, layer, res, inner, groups):
    b, s, _ = proj.shape
    d = res.shape[2]
    dg = d // groups
    t = min(SSD_CHUNK, s)
    tb = min(SSD_TB, s)
    gw = inner // groups
    n = SSD_STATE
    hpg = SSD_HPG
    assert gw == hpg * SSD_HEADDIM and 2 * SSD_HEADDIM == LANES and tb % t == 0
    dt_row = dt_raw.reshape(b, s, groups, hpg).transpose(0, 2, 3, 1)
    db_r = dt_bias.reshape(groups, hpg, 1)
    an_r = a_neg.reshape(groups, hpg, 1)
    dsk = jnp.repeat(d_skip, SSD_HEADDIM).reshape(groups, 1, gw)
    nw = norm_w.reshape(groups, 1, gw)
    by_group = lambda a, lo, w: (a[:, lo:lo + groups * w].reshape(a.shape[0], groups, w).transpose(1, 0, 2))
    conv_b2 = conv_b.reshape(1, -1)
    conv_parts = []
    for lo, w in ((0, gw), (inner, n), (inner + groups * n, n)):
        conv_parts += [by_group(conv_w, lo, w), by_group(conv_b2, lo, w)]
    tri = (jnp.arange(t)[:, None] <= jnp.arange(t)[None, :]).astype(BF16)
    krow = jnp.arange(LANES)[:, None]
    sel = ((krow < 3 * hpg) & ((krow % hpg) == (jnp.arange(hpg * LANES)[None, :] // LANES))).astype(BF16)
    xoff, boff, coff = inner // gw, (2 * inner) // n, (2 * inner + groups * n) // n
    nk = s // tb
    total = b * nk

    def cur(kk):
        c = jnp.minimum(kk, total - 1)
        return c // nk, c % nk

    def prev(kk):
        p = jnp.maximum(kk - 1, 0)
        return p // nk, p % nk

    out_col = lambda kk, g: jnp.where(kk == 0, 0, g)
    const = lambda a: pl.BlockSpec(a.shape, lambda kk, g: (0,) * a.ndim)
    small = conv_parts + [db_r, an_r, dsk, nw, tri, sel]
    return pl.pallas_call(
        functools.partial(_ssd_kernel, row_blocks=nk, total_blocks=total),
        grid=(total + 1, groups),
        in_specs=[
            pl.BlockSpec((1, tb, gw), lambda kk, g: (*cur(kk), g)),
            pl.BlockSpec((1, tb, gw), lambda kk, g: (*cur(kk), xoff + g)),
            pl.BlockSpec((1, tb, n), lambda kk, g: (*cur(kk), boff + g)),
            pl.BlockSpec((1, tb, n), lambda kk, g: (*cur(kk), coff + g)),
            pl.BlockSpec((1, 1, hpg, tb), lambda kk, g: (cur(kk)[0], g, 0, cur(kk)[1])),
            *[const(a) for a in small],
            pl.BlockSpec((None, inner, dg), lambda kk, g: (layer, 0, g)),
            pl.BlockSpec((1, tb, dg), lambda kk, g: (*prev(kk), out_col(kk, g))),
        ],
        out_specs=pl.BlockSpec((1, tb, dg), lambda kk, g: (*prev(kk), out_col(kk, g))),
        out_shape=jax.ShapeDtypeStruct((b, s, d), F32),
        scratch_shapes=[
            pltpu.VMEM((groups, n, gw), F32),
            pltpu.VMEM((tb, gw), F32),
            pltpu.VMEM((2, groups, tb, gw), BF16),
            pltpu.VMEM((tb + SUBLANES, gw), F32),
            pltpu.VMEM((tb + SUBLANES, n), F32),
            pltpu.VMEM((tb + SUBLANES, n), F32),
            pltpu.VMEM((groups, SUBLANES, gw), F32),
            pltpu.VMEM((groups, SUBLANES, n), F32),
            pltpu.VMEM((groups, SUBLANES, n), F32),
        ],
        compiler_params=_cparams("arbitrary", "arbitrary"),
        name="ssd",
    )(proj, proj, proj, proj, dt_row, *small, w_out, res)


TM = 1024
TN = 1024
SQ_TM = 512
SQ_TN = 2048
FFN_TH = 512
XA_TM = 512
LRU_TB = 512
HGRN_TB = 1024
SSD_TB = 512


def kernel(x, mem, norm_mix, norm_xattn, norm_ffn, norm_mem, norm_final, ab_w_in, ab_w_out, lru_conv_w, lru_conv_b, lru_w_r, lru_b_r, lru_w_i, lru_b_i, lru_lambda, hgrn_lower_bounds, hgrn_norm, ssd_w_in, ssd_w_out, ssd_conv_w, ssd_conv_b, ssd_dt_bias, ssd_a_log, ssd_d, ssd_norm, xa_w_q, xa_w_kv, xa_w_o, ffn_w_gate, ffn_w_up, ffn_w_down):
    bsz, seq, d = x.shape
    depth = norm_mix.shape[0]
    m = bsz * seq
    lru_w = lru_conv_w.shape[2]
    hgrn_w = hgrn_norm.shape[1]
    ssd_heads = ssd_a_log.shape[1]
    ssd_inner = ssd_norm.shape[1]
    ssd_groups = ssd_heads // SSD_HPG
    ssd_main = ssd_w_in.shape[2] - ssd_heads

    sm = jax.nn.softmax(hgrn_lower_bounds.astype(F32), axis=0)
    lower_bounds = jnp.cumsum(sm, axis=0) - sm[0]
    lru_sp = jax.nn.softplus(-lru_lambda.astype(F32))
    ssd_a_neg = -jnp.exp(ssd_a_log.astype(F32))
    lru_w_ri = jnp.concatenate([lru_w_r, lru_w_i], axis=-1).astype(BF16)
    ab_in_b = _cast_bf16(ab_w_in)
    ab_out_b = _cast_bf16(ab_w_out)
    ssd_in_b = ssd_w_in[:, :, :ssd_main].astype(BF16)
    ssd_dt_b = ssd_w_in[:, :, ssd_main:].astype(BF16)
    ssd_out_b = _cast_bf16(ssd_w_out)
    wq_b = _cast_bf16(xa_w_q)
    wo_b = _cast_bf16(xa_w_o)
    wg_b = _cast_bf16(ffn_w_gate)
    wu_b = _cast_bf16(ffn_w_up)
    wd_b = _cast_bf16(ffn_w_down)

    x2 = x.reshape(m, d)
    mem_len = mem.shape[1]
    kv = _kv_proj(mem.reshape(bsz * mem_len, d), norm_mem, xa_w_kv, TN).reshape(depth, bsz, mem_len, 2 * d)

    for layer in range(depth):
        if layer % 2 == 0:
            e = layer // 2
            proj = _norm_matmul(x2, norm_mix[layer], ab_in_b, e, F32, TM, TN, "ab_in_proj")
            proj = proj.reshape(bsz, seq, -1)
            ya = _lru(proj, lru_conv_w[e], lru_conv_b[e], lru_w_ri[e], lru_b_r[e], lru_b_i[e], lru_sp[e], LRU_TB)
            yb = _hgrn(proj, lower_bounds[e], hgrn_norm[e], 2 * lru_w, hgrn_w // HGRN_DK, HGRN_TB)
            x2 = _out_proj2(ya.reshape(m, lru_w), yb.reshape(m, hgrn_w), ab_out_b, e, x2, SQ_TM, SQ_TN)
        else:
            o = layer // 2
            proj, dt_raw = _norm_matmul_side(x2, norm_mix[layer], ssd_in_b, ssd_dt_b, o, TM, TN, "ssd_in_proj")
            x2 = _ssd(proj.reshape(bsz, seq, ssd_main), dt_raw.reshape(bsz, seq, ssd_heads),
                      ssd_conv_w[o], ssd_conv_b[o], ssd_dt_bias[o], ssd_a_neg[o], ssd_d[o], ssd_norm[o],
                      ssd_out_b, o, x2.reshape(bsz, seq, d), ssd_inner, ssd_groups).reshape(m, d)
        x2 = _xattn(x2.reshape(bsz, seq, d), norm_xattn[layer], wq_b, kv, wo_b, layer, XA_TM).reshape(m, d)
        g_out = norm_final if layer == depth - 1 else None
        x2 = _ffn(x2, norm_ffn[layer], wg_b, wu_b, wd_b, layer, TM, FFN_TH, g_out)
    return x2.reshape(bsz, seq, d)
```

```python
import functools
import math

import jax
import jax.numpy as jnp
from jax import lax
from jax.experimental import pallas as pl
from jax.experimental.pallas import tpu as pltpu

F32 = jnp.float32
BF16 = jnp.bfloat16
EPS = 1e-6
NEG_INF = float("-inf")

VMEM_LIMIT_BYTES = 56 * 1024 * 1024
CAST_BLOCK_BYTES = 4 * 1024 * 1024
SUBLANES = 8
LANES = 128

CONV_K = 4
LRU_BLOCK = 128
LRU_C = 8.0
HGRN_DK = 128
HGRN_CHUNK = 64
HGRN_SUB = SUBLANES
SSD_HEADDIM = 64
SSD_HPG = 8
SSD_STATE = 128
SSD_CHUNK = 128
XA_HEADS = 4


def _cparams(*sem):
    return pltpu.CompilerParams(dimension_semantics=sem, vmem_limit_bytes=VMEM_LIMIT_BYTES)


def _rms_scale(x):
    return lax.rsqrt(jnp.mean(x * x, axis=-1, keepdims=True) + EPS)


def _sigmoid(x):
    return 0.5 * (1.0 + jnp.tanh(0.5 * x))


def _silu(x):
    return x * _sigmoid(x)


def _gelu_tanh(x):
    c = math.sqrt(2.0 / math.pi)
    return x * (0.5 * (1.0 + jnp.tanh(c * (x + 0.044715 * (x * x * x)))))


def _softplus(x):
    return jnp.maximum(x, 0.0) + jnp.log1p(jnp.exp(-jnp.abs(x)))


def _linear_scan_rows(a, b, h0):
    t, c = a.shape
    groups = t // SUBLANES
    a3 = a.reshape(groups, SUBLANES, c)
    b3 = b.reshape(groups, SUBLANES, c)
    row = lax.broadcasted_iota(jnp.int32, a3.shape, 1)
    s = 1
    while s < SUBLANES:
        keep = row >= s
        a_sh = pltpu.roll(a3, s, 1)
        b_sh = pltpu.roll(b3, s, 1)
        b3 = jnp.where(keep, a3 * b_sh + b3, b3)
        a3 = jnp.where(keep, a3 * a_sh, a3)
        s *= 2
    out = []
    carry = h0
    for g in range(groups):
        h = a3[g] * carry + b3[g]
        out.append(h)
        carry = h[SUBLANES - 1:SUBLANES, :]
    return jnp.concatenate(out, axis=0)


def _sqrt_nonneg(y):
    return jnp.where(y > 0.0, y * lax.rsqrt(y), 0.0)


def _cumsum(x, axis):
    n = x.shape[axis]
    idx = lax.broadcasted_iota(jnp.int32, x.shape, axis)
    s = 1
    while s < n:
        x = jnp.where(idx >= s, x + pltpu.roll(x, s, axis), x)
        s *= 2
    return x


def _causal_conv(u, buf_ref, w_ref, b_ref):
    t = u.shape[0]
    buf_ref[SUBLANES:SUBLANES + t, :] = u
    out = b_ref[...] + w_ref[3:4, :] * u
    for j in range(1, CONV_K):
        out = out + w_ref[CONV_K - 1 - j:CONV_K - j, :] * buf_ref[SUBLANES - j:SUBLANES - j + t, :]
    buf_ref[0:SUBLANES, :] = u[t - SUBLANES:t, :]
    return out


def _norm_matmul_kernel(x_ref, g_ref, w_ref, o_ref, xn_ref):
    @pl.when(pl.program_id(1) == 0)
    def _():
        x = x_ref[...]
        xn_ref[...] = (x * _rms_scale(x) * g_ref[...]).astype(BF16)

    o_ref[...] = jnp.dot(xn_ref[...], w_ref[...], preferred_element_type=F32).astype(o_ref.dtype)


def _norm_matmul(x, g, w, layer, out_dtype, tm, tn, name):
    m, k = x.shape
    n = w.shape[2]
    tm = min(tm, m)
    tn = min(tn, n)
    return pl.pallas_call(
        _norm_matmul_kernel,
        grid=(m // tm, n // tn),
        in_specs=[
            pl.BlockSpec((tm, k), lambda i, j: (i, 0)),
            pl.BlockSpec((1, k), lambda i, j: (0, 0)),
            pl.BlockSpec((None, k, tn), lambda i, j: (layer, 0, j)),
        ],
        out_specs=pl.BlockSpec((tm, tn), lambda i, j: (i, j)),
        out_shape=jax.ShapeDtypeStruct((m, n), out_dtype),
        scratch_shapes=[pltpu.VMEM((tm, k), BF16)],
        compiler_params=_cparams("parallel", "arbitrary"),
        name=name,
    )(x, g.reshape(1, k), w)


def _norm_matmul_side_kernel(x_ref, g_ref, w_ref, ws_ref, o_ref, os_ref, xn_ref):
    @pl.when(pl.program_id(1) == 0)
    def _():
        x = x_ref[...]
        xn = (x * _rms_scale(x) * g_ref[...]).astype(BF16)
        xn_ref[...] = xn
        os_ref[...] = jnp.dot(xn, ws_ref[...], preferred_element_type=F32)

    o_ref[...] = jnp.dot(xn_ref[...], w_ref[...], preferred_element_type=F32).astype(o_ref.dtype)


def _norm_matmul_side(x, g, w, w_side, layer, tm, tn, name):
    m, k = x.shape
    n = w.shape[2]
    ns = w_side.shape[2]
    tm = min(tm, m)
    tn = min(tn, n)
    return pl.pallas_call(
        _norm_matmul_side_kernel,
        grid=(m // tm, n // tn),
        in_specs=[
            pl.BlockSpec((tm, k), lambda i, j: (i, 0)),
            pl.BlockSpec((1, k), lambda i, j: (0, 0)),
            pl.BlockSpec((None, k, tn), lambda i, j: (layer, 0, j)),
            pl.BlockSpec((None, k, ns), lambda i, j: (layer, 0, 0)),
        ],
        out_specs=[pl.BlockSpec((tm, tn), lambda i, j: (i, j)), pl.BlockSpec((tm, ns), lambda i, j: (i, 0))],
        out_shape=[jax.ShapeDtypeStruct((m, n), F32), jax.ShapeDtypeStruct((m, ns), F32)],
        scratch_shapes=[pltpu.VMEM((tm, k), BF16)],
        compiler_params=_cparams("parallel", "arbitrary"),
        name=name,
    )(x, g.reshape(1, k), w, w_side)


def _kv_proj_kernel(x_ref, g_ref, w_ref, o_ref, xn_ref):
    @pl.when((pl.program_id(0) == 0) & (pl.program_id(1) == 0))
    def _():
        x = x_ref[...]
        xn_ref[...] = (x * _rms_scale(x) * g_ref[...]).astype(BF16)

    o_ref[...] = jnp.dot(xn_ref[...], w_ref[...].astype(BF16), preferred_element_type=F32).astype(o_ref.dtype)


def _kv_proj(x, g, w, tn):
    m, k = x.shape
    nl, _, n = w.shape
    return pl.pallas_call(
        _kv_proj_kernel,
        grid=(nl, n // tn),
        in_specs=[
            pl.BlockSpec((m, k), lambda l, j: (0, 0)),
            pl.BlockSpec((1, k), lambda l, j: (0, 0)),
            pl.BlockSpec((None, k, tn), lambda l, j: (l, 0, j)),
        ],
        out_specs=pl.BlockSpec((None, m, tn), lambda l, j: (l, 0, j)),
        out_shape=jax.ShapeDtypeStruct((nl, m, n), BF16),
        scratch_shapes=[pltpu.VMEM((m, k), BF16)],
        compiler_params=_cparams("arbitrary", "arbitrary"),
        name="kv_proj",
    )(x, g.reshape(1, k), w)


def _cast_kernel(x_ref, o_ref):
    o_ref[...] = x_ref[...].astype(BF16)


def _cast_bf16(w):
    nl, k, n = w.shape
    tk = SUBLANES
    while tk * 2 <= k and k % (tk * 2) == 0 and tk * 2 * n * 4 <= CAST_BLOCK_BYTES:
        tk *= 2
    return pl.pallas_call(
        _cast_kernel,
        grid=(nl, k // tk),
        in_specs=[pl.BlockSpec((None, tk, n), lambda l, i: (l, i, 0))],
        out_specs=pl.BlockSpec((None, tk, n), lambda l, i: (l, i, 0)),
        out_shape=jax.ShapeDtypeStruct((nl, k, n), BF16),
        compiler_params=_cparams("parallel", "parallel"),
        name="cast_bf16",
    )(w)


def _ffn_kernel(x_ref, g_ref, go_ref, wg_ref, wu_ref, wd_ref, o_ref, xn_ref, *, out_norm):
    @pl.when(pl.program_id(1) == 0)
    def _():
        x = x_ref[...]
        xn_ref[...] = (x * _rms_scale(x) * g_ref[...]).astype(BF16)
        o_ref[...] = x

    xn = xn_ref[...]
    gate = jnp.dot(xn, wg_ref[...], preferred_element_type=F32)
    up = jnp.dot(xn, wu_ref[...], preferred_element_type=F32)
    hid = (_silu(gate) * up).astype(BF16)
    o_ref[...] += jnp.dot(hid, wd_ref[...], preferred_element_type=F32)

    if out_norm:
        @pl.when(pl.program_id(1) == pl.num_programs(1) - 1)
        def _():
            y = o_ref[...]
            o_ref[...] = y * _rms_scale(y) * go_ref[...]


def _ffn(x, g, wg, wu, wd, layer, tm, th, g_out=None):
    m, d = x.shape
    hdim = wg.shape[2]
    tm = min(tm, m)
    out_norm = g_out is not None
    g_out = g if g_out is None else g_out
    return pl.pallas_call(
        functools.partial(_ffn_kernel, out_norm=out_norm),
        grid=(m // tm, hdim // th),
        in_specs=[
            pl.BlockSpec((tm, d), lambda i, j: (i, 0)),
            pl.BlockSpec((1, d), lambda i, j: (0, 0)),
            pl.BlockSpec((1, d), lambda i, j: (0, 0)),
            pl.BlockSpec((None, d, th), lambda i, j: (layer, 0, j)),
            pl.BlockSpec((None, d, th), lambda i, j: (layer, 0, j)),
            pl.BlockSpec((None, th, d), lambda i, j: (layer, j, 0)),
        ],
        out_specs=pl.BlockSpec((tm, d), lambda i, j: (i, 0)),
        out_shape=jax.ShapeDtypeStruct((m, d), F32),
        scratch_shapes=[pltpu.VMEM((tm, d), BF16)],
        compiler_params=_cparams("parallel", "arbitrary"),
        name="ffn",
    )(x, g.reshape(1, d), g_out.reshape(1, d), wg, wu, wd)


def _xattn_kernel(x_ref, g_ref, wq_ref, k_ref, v_ref, wo_ref, o_ref, q_ref, ob_ref, *, heads, scale):
    x = x_ref[0]
    xn = (x * _rms_scale(x) * g_ref[...]).astype(BF16)
    q_ref[...] = jnp.dot(xn, wq_ref[...], preferred_element_type=F32).astype(BF16)
    hd = q_ref.shape[1] // heads
    for h in range(heads):
        sl = slice(h * hd, (h + 1) * hd)
        s = lax.dot_general(q_ref[:, sl], k_ref[0, :, sl], (((1,), (1,)), ((), ())),
                            preferred_element_type=F32) * scale
        p = jnp.exp(s - jnp.max(s, axis=-1, keepdims=True))
        p = p / jnp.sum(p, axis=-1, keepdims=True)
        ob_ref[:, sl] = jnp.dot(p.astype(BF16), v_ref[0, :, sl], preferred_element_type=F32).astype(BF16)
    o_ref[0] = x + jnp.dot(ob_ref[...], wo_ref[...], preferred_element_type=F32)


def _xattn(x, g, wq, kv, wo, layer, tm):
    b, s, d = x.shape
    mem = kv.shape[2]
    tm = min(tm, s)
    kern = functools.partial(_xattn_kernel, heads=XA_HEADS, scale=(d // XA_HEADS) ** -0.5)
    resident = lambda: pl.BlockSpec((None, d, d), lambda i, j: (layer, 0, 0), pipeline_mode=pl.Buffered(1))
    return pl.pallas_call(
        kern,
        grid=(b, s // tm),
        in_specs=[
            pl.BlockSpec((1, tm, d), lambda i, j: (i, j, 0)),
            pl.BlockSpec((1, d), lambda i, j: (0, 0)),
            resident(),
            pl.BlockSpec((None, 1, mem, d), lambda i, j: (layer, i, 0, 0)),
            pl.BlockSpec((None, 1, mem, d), lambda i, j: (layer, i, 0, 1)),
            resident(),
        ],
        out_specs=pl.BlockSpec((1, tm, d), lambda i, j: (i, j, 0)),
        out_shape=jax.ShapeDtypeStruct((b, s, d), F32),
        scratch_shapes=[pltpu.VMEM((tm, d), BF16), pltpu.VMEM((tm, d), BF16)],
        compiler_params=_cparams("parallel", "arbitrary"),
        name="xattn",
    )(x, g.reshape(1, d), wq, kv, kv, wo)


def _lru_kernel(xa_ref, ga_ref, cw_ref, cb_ref, wri_ref, br_ref, bi_ref, sp_ref, o_ref, buf_ref, h_ref):
    @pl.when(pl.program_id(1) == 0)
    def _():
        buf_ref[0:SUBLANES, :] = jnp.zeros((SUBLANES, buf_ref.shape[1]), F32)
        h_ref[...] = jnp.zeros(h_ref.shape, F32)

    xc = _causal_conv(xa_ref[0], buf_ref, cw_ref, cb_ref)
    t = xc.shape[0]
    for blk in range(xc.shape[1] // LRU_BLOCK):
        sl = slice(blk * LRU_BLOCK, (blk + 1) * LRU_BLOCK)
        xb = xc[:, sl]
        pre = jnp.dot(xb.astype(BF16), wri_ref[blk], preferred_element_type=F32)
        r_gate = _sigmoid(pre[:, :LRU_BLOCK] + br_ref[:, sl])
        i_gate = _sigmoid(pre[:, LRU_BLOCK:] + bi_ref[:, sl])
        log_a = (-LRU_C) * r_gate * sp_ref[:, sl]
        a = jnp.exp(log_a)
        mult = _sqrt_nonneg(-jnp.tanh(log_a) * (a * a + 1.0))
        h = _linear_scan_rows(a, mult * i_gate * xb, h_ref[0:1, sl])
        h_ref[0:1, sl] = h[t - 1:t, :]
        o_ref[0, :, sl] = (_gelu_tanh(ga_ref[0, :, sl]) * h).astype(BF16)


def _lru(proj, conv_w, conv_b, w_ri, b_r, b_i, sp, tb):
    b, s, _ = proj.shape
    w = conv_w.shape[1]
    tb = min(tb, s)
    vec = lambda: pl.BlockSpec((1, w), lambda i, j: (0, 0))
    return pl.pallas_call(
        _lru_kernel,
        grid=(b, s // tb),
        in_specs=[
            pl.BlockSpec((1, tb, w), lambda i, j: (i, j, 0)),
            pl.BlockSpec((1, tb, w), lambda i, j: (i, j, 1)),
            pl.BlockSpec((CONV_K, w), lambda i, j: (0, 0)),
            vec(),
            pl.BlockSpec(w_ri.shape, lambda i, j: (0, 0, 0)),
            vec(), vec(), vec(),
        ],
        out_specs=pl.BlockSpec((1, tb, w), lambda i, j: (i, j, 0)),
        out_shape=jax.ShapeDtypeStruct((b, s, w), BF16),
        scratch_shapes=[pltpu.VMEM((tb + SUBLANES, w), F32), pltpu.VMEM((SUBLANES, w), F32)],
        compiler_params=_cparams("parallel", "arbitrary"),
        name="lru",
    )(proj, proj, conv_w, conv_b.reshape(1, w), w_ri, b_r.reshape(1, w), b_i.reshape(1, w), sp.reshape(1, w))


def _hgrn_chunk(q, f, v, lb, st):
    c, dk = q.shape
    nsub = c // HGRN_SUB
    qh = _silu(q)
    fg = lb + (1.0 - lb) * _sigmoid(f)
    kh = 1.0 - fg
    g = jnp.log(fg)
    cum = _cumsum(g, 0)
    ex = cum - g
    cum3 = cum.reshape(nsub, HGRN_SUB, dk)
    ex3 = ex.reshape(nsub, HGRN_SUB, dk)
    base3 = jnp.broadcast_to(ex3[:, 0:1, :], cum3.shape)
    q3 = qh.reshape(nsub, HGRN_SUB, dk)
    k3 = kh.reshape(nsub, HGRN_SUB, dk)
    v3 = v.reshape(nsub, HGRN_SUB, dk)
    vb = v.astype(BF16)

    row3 = lax.broadcasted_iota(jnp.int32, cum3.shape, 1)
    terms = []
    for s in range(HGRN_SUB):
        diff = cum3 - cum3[:, s:s + 1, :]
        dec = jnp.exp(jnp.where(row3 >= s, diff, NEG_INF))
        terms.append((dec * q3 * k3[:, s:s + 1, :]).reshape(c, dk))
    stacked = jnp.concatenate(terms, axis=0).astype(BF16)
    ones = jnp.ones((dk, dk), BF16)
    summed = jnp.dot(stacked, ones, preferred_element_type=F32)
    o3 = jnp.zeros(cum3.shape, F32)
    for s in range(HGRN_SUB):
        o3 = o3 + summed[s * c:(s + 1) * c, :].reshape(nsub, HGRN_SUB, dk) * v3[:, s:s + 1, :]
    o = o3.reshape(c, dk)

    q_loc = (qh * jnp.exp(cum - base3.reshape(c, dk))).astype(BF16)
    pad = jnp.zeros((LANES, dk), F32)
    k_parts = []
    for i in range(1, nsub):
        n = i * HGRN_SUB
        base_i = ex[n:n + 1, :]
        k_parts.append(kh[0:n, :] * jnp.exp(base_i - cum[0:n, :]))
        k_parts.append(pad[0:LANES - n, :])
    k_hat = jnp.concatenate(k_parts, axis=0).astype(BF16)
    a_all = lax.dot_general(q_loc, k_hat, (((1,), (1,)), ((), ())), preferred_element_type=F32)
    a_rows = [jnp.zeros((HGRN_SUB, c), F32)]
    for i in range(1, nsub):
        a_rows.append(a_all[i * HGRN_SUB:(i + 1) * HGRN_SUB, (i - 1) * LANES:(i - 1) * LANES + c])
    a_off = jnp.concatenate(a_rows, axis=0).astype(BF16)
    o = o + jnp.dot(a_off, vb, preferred_element_type=F32)

    q_in = (qh * jnp.exp(cum)).astype(BF16)
    o = o + lax.dot_general(q_in, st.astype(BF16), (((1,), (1,)), ((), ())), preferred_element_type=F32)
    last = cum[c - 1:c, :]
    k_out = (kh * jnp.exp(last - cum)).astype(BF16)
    st_new = st * jnp.exp(last) + lax.dot_general(vb, k_out, (((0,), (0,)), ((), ())),
                                                  preferred_element_type=F32)
    return o, st_new


def _hgrn_kernel(q_ref, f_ref, v_ref, gb_ref, lb_ref, gn_ref, o_ref, st_ref):
    @pl.when(pl.program_id(2) == 0)
    def _():
        st_ref[...] = jnp.zeros(st_ref.shape, F32)

    lb = lb_ref[...]
    st = st_ref[...]
    for ci in range(q_ref.shape[1] // HGRN_CHUNK):
        rows = slice(ci * HGRN_CHUNK, (ci + 1) * HGRN_CHUNK)
        o, st = _hgrn_chunk(q_ref[0, rows, :], f_ref[0, rows, :], v_ref[0, rows, :], lb, st)
        o = o * _rms_scale(o) * gn_ref[...]
        o_ref[0, rows, :] = (o * _silu(gb_ref[0, rows, :])).astype(BF16)
    st_ref[...] = st


def _hgrn(proj, lower_bound, head_norm, col0, heads, tb):
    b, s, _ = proj.shape
    tb = min(tb, s)
    c0 = col0 // HGRN_DK
    part = lambda p: pl.BlockSpec((1, tb, HGRN_DK), lambda i, h, j: (i, j, c0 + p * heads + h))
    vec = lambda: pl.BlockSpec((1, HGRN_DK), lambda i, h, j: (0, h))
    w = heads * HGRN_DK
    return pl.pallas_call(
        _hgrn_kernel,
        grid=(b, heads, s // tb),
        in_specs=[part(0), part(1), part(2), part(3), vec(), vec()],
        out_specs=pl.BlockSpec((1, tb, HGRN_DK), lambda i, h, j: (i, j, h)),
        out_shape=jax.ShapeDtypeStruct((b, s, w), BF16),
        scratch_shapes=[pltpu.VMEM((HGRN_DK, HGRN_DK), F32)],
        compiler_params=_cparams("parallel", "parallel", "arbitrary"),
        name="hgrn",
    )(proj, proj, proj, proj, lower_bound.reshape(1, w), head_norm.reshape(1, w))


def _out_proj2_kernel(a1_ref, a2_ref, w1_ref, w2_ref, r_ref, o_ref):
    acc = jnp.dot(a1_ref[...], w1_ref[...], preferred_element_type=F32)
    acc = acc + jnp.dot(a2_ref[...], w2_ref[...], preferred_element_type=F32)
    o_ref[...] = r_ref[...] + acc


def _out_proj2(a1, a2, w, layer, res, tm, tn):
    m, k1 = a1.shape
    assert a2.shape[1] == k1
    n = w.shape[2]
    tm = min(tm, m)
    tn = min(tn, n)
    return pl.pallas_call(
        _out_proj2_kernel,
        grid=(m // tm, n // tn),
        in_specs=[
            pl.BlockSpec((tm, k1), lambda i, j: (i, 0)),
            pl.BlockSpec((tm, k1), lambda i, j: (i, 0)),
            pl.BlockSpec((None, k1, tn), lambda i, j: (layer, 0, j)),
            pl.BlockSpec((None, k1, tn), lambda i, j: (layer, 1, j)),
            pl.BlockSpec((tm, tn), lambda i, j: (i, j)),
        ],
        out_specs=pl.BlockSpec((tm, tn), lambda i, j: (i, j)),
        out_shape=jax.ShapeDtypeStruct((m, n), F32),
        compiler_params=_cparams("parallel", "arbitrary"),
        name="ab_out_proj",
    )(a1, a2, w, w, res)


def _split3(x):
    hi = x.astype(BF16).astype(F32)
    rem = x - hi
    mid = rem.astype(BF16).astype(F32)
    return hi, mid, rem - mid


def _ssd_scan_block(z_ref, x_ref, b_ref, c_ref, dtr_ref, conv_refs, dbr_ref, anr_ref, dsk_ref, nw_ref,
                    tri_ref, sel_ref, st_ref, y_ref, buf_refs, tail_refs):
    tb = x_ref.shape[1]
    t = tri_ref.shape[0]
    n = st_ref.shape[0]
    hpg = dbr_ref.shape[0]
    (cwx_ref, cbx_ref, cwb_ref, cbb_ref, cwc_ref, cbc_ref) = conv_refs
    xs_all = _silu(_causal_conv(x_ref[0], buf_refs[0], cwx_ref, cbx_ref))
    bm_all = _silu(_causal_conv(b_ref[0], buf_refs[1], cwb_ref, cbb_ref))
    cm_all = _silu(_causal_conv(c_ref[0], buf_refs[2], cwc_ref, cbc_ref))
    for buf_ref, tail_ref in zip(buf_refs, tail_refs):
        tail_ref[...] = buf_ref[0:SUBLANES, :]
    dtr_all = _softplus(dtr_ref[0, 0] + dbr_ref[...])
    causal = (lax.broadcasted_iota(jnp.int32, (t, t), 0) >= lax.broadcasted_iota(jnp.int32, (t, t), 1))
    low_t = lax.broadcasted_iota(jnp.int32, (t, LANES), 1) < SSD_HEADDIM
    low_n = lax.broadcasted_iota(jnp.int32, (n, LANES), 1) < SSD_HEADDIM
    pad = jnp.zeros((sel_ref.shape[0] - 3 * hpg, t), F32)

    for ci in range(tb // t):
        rows = slice(ci * t, (ci + 1) * t)
        xs, bm, cm, dtr = xs_all[rows], bm_all[rows], cm_all[rows], dtr_all[:, rows]
        xsb = xs.astype(BF16)
        dta = dtr * anr_ref[...]
        parts = jnp.concatenate(list(_split3(dta)) + [jnp.zeros_like(dta)], axis=0).astype(BF16)
        c3 = jnp.dot(parts, tri_ref[...], preferred_element_type=F32)
        cum = c3[0:hpg] + c3[hpg:2 * hpg] + c3[2 * hpg:3 * hpg]
        cols = jnp.concatenate(list(_split3(cum)) + [pad], axis=0).astype(BF16)
        bc_all = lax.dot_general(cols, sel_ref[...], (((0,), (0,)), ((), ())), preferred_element_type=F32)
        row_term = cum - jnp.log(dtr)
        last = jnp.broadcast_to(cum[:, t - 1:t], cum.shape)
        row_scale = jnp.exp(last - cum) * dtr
        e_last = jnp.exp(bc_all[t - 1:t, :])
        cb = lax.dot_general(cm.astype(BF16), bm.astype(BF16), (((1,), (1,)), ((), ())),
                             preferred_element_type=F32)
        bt = bm.T

        for pair in range(hpg // 2):
            pc = slice(pair * LANES, (pair + 1) * LANES)
            rhs = jnp.concatenate([xsb[:, pc], st_ref[:, pc].astype(BF16)], axis=0)
            lhs, zl = [], []
            for h in (2 * pair, 2 * pair + 1):
                bc = bc_all[:, h * LANES:(h + 1) * LANES]
                dec = jnp.exp(jnp.where(causal, bc - row_term[h:h + 1, :], NEG_INF))
                lhs.append(jnp.concatenate([cb * dec, cm * jnp.exp(bc)], axis=1).astype(BF16))
                zl.append((bt * row_scale[h:h + 1, :]).astype(BF16))
            y2 = jnp.dot(jnp.concatenate(lhs, axis=0), rhs, preferred_element_type=F32)
            y_ref[rows, pc] = jnp.where(low_t, y2[:t], y2[t:]) + dsk_ref[:, pc] * xs[:, pc]
            z2 = jnp.dot(jnp.concatenate(zl, axis=0), xsb[:, pc], preferred_element_type=F32)
            e_pair = jnp.where(low_n[0:1], e_last[:, 2 * pair * LANES:(2 * pair + 1) * LANES],
                               e_last[:, (2 * pair + 1) * LANES:(2 * pair + 2) * LANES])
            st_ref[:, pc] = st_ref[:, pc] * e_pair + jnp.where(low_n, z2[:n], z2[n:])

    y = y_ref[...] * _silu(z_ref[0])
    return (y * _rms_scale(y) * nw_ref[...]).astype(BF16)


def _ssd_kernel(z_ref, x_ref, b_ref, c_ref, dtr_ref,
                cwx_ref, cbx_ref, cwb_ref, cbb_ref, cwc_ref, cbc_ref,
                dbr_ref, anr_ref, dsk_ref, nw_ref, tri_ref, sel_ref, wo_ref, res_ref,
                o_ref, st_ref, y_ref, yn_ref, bufx_ref, bufb_ref, bufc_ref, tailx_ref, tailb_ref, tailc_ref,
                *, row_blocks, total_blocks):
    kk = pl.program_id(0)
    g = pl.program_id(1)
    slot = lax.rem(kk, 2)
    groups = yn_ref.shape[1]
    bufs = (bufx_ref, bufb_ref, bufc_ref)
    tails = (tailx_ref.at[g], tailb_ref.at[g], tailc_ref.at[g])

    def enter():
        seq_start = lax.rem(kk, row_blocks) == 0

        @pl.when(seq_start)
        def _():
            st_ref[g] = jnp.zeros(st_ref.shape[1:], F32)
            for buf_ref in bufs:
                buf_ref[0:SUBLANES, :] = jnp.zeros((SUBLANES, buf_ref.shape[1]), F32)

        @pl.when(jnp.logical_not(seq_start))
        def _():
            for buf_ref, tail_ref in zip(bufs, tails):
                buf_ref[0:SUBLANES, :] = tail_ref[...]

    def scan():
        yn_ref[slot, g] = _ssd_scan_block(
            z_ref, x_ref, b_ref, c_ref, dtr_ref,
            tuple(r.at[g] for r in (cwx_ref, cbx_ref, cwb_ref, cbb_ref, cwc_ref, cbc_ref)),
            dbr_ref.at[g], anr_ref.at[g], dsk_ref.at[g], nw_ref.at[g], tri_ref, sel_ref, st_ref.at[g], y_ref,
            bufs, tails)

    def project():
        lhs = jnp.concatenate([yn_ref[1 - slot, gg] for gg in range(groups)], axis=1)
        o_ref[0] = res_ref[0] + jnp.dot(lhs, wo_ref[...], preferred_element_type=F32)

    @pl.when(kk == 0)
    def _():
        enter()
        scan()

    @pl.when((kk > 0) & (kk < total_blocks))
    def _():
        enter()
        project()
        scan()

    @pl.when(kk == total_blocks)
    def _():
        project()


def _ssd(proj, dt_raw, conv_w, conv_b, dt_bias, a_neg, d_skip, norm_w, w_out, layer, res, inner, groups):
    b, s, _ = proj.shape
    d = res.shape[2]
    dg = d // groups
    t = min(SSD_CHUNK, s)
    tb = min(SSD_TB, s)
    gw = inner // groups
    n = SSD_STATE
    hpg = SSD_HPG
    assert gw == hpg * SSD_HEADDIM and 2 * SSD_HEADDIM == LANES and tb % t == 0
    dt_row = dt_raw.reshape(b, s, groups, hpg).transpose(0, 2, 3, 1)
    db_r = dt_bias.reshape(groups, hpg, 1)
    an_r = a_neg.reshape(groups, hpg, 1)
    dsk = jnp.repeat(d_skip, SSD_HEADDIM).reshape(groups, 1, gw)
    nw = norm_w.reshape(groups, 1, gw)
    by_group = lambda a, lo, w: (a[:, lo:lo + groups * w].reshape(a.shape[0], groups, w).transpose(1, 0, 2))
    conv_b2 = conv_b.reshape(1, -1)
    conv_parts = []
    for lo, w in ((0, gw), (inner, n), (inner + groups * n, n)):
        conv_parts += [by_group(conv_w, lo, w), by_group(conv_b2, lo, w)]
    tri = (jnp.arange(t)[:, None] <= jnp.arange(t)[None, :]).astype(BF16)
    krow = jnp.arange(LANES)[:, None]
    sel = ((krow < 3 * hpg) & ((krow % hpg) == (jnp.arange(hpg * LANES)[None, :] // LANES))).astype(BF16)
    xoff, boff, coff = inner // gw, (2 * inner) // n, (2 * inner + groups * n) // n
    nk = s // tb
    total = b * nk

    def cur(kk):
        c = jnp.minimum(kk, total - 1)
        return c // nk, c % nk

    def prev(kk):
        p = jnp.maximum(kk - 1, 0)
        return p // nk, p % nk

    out_col = lambda kk, g: jnp.where(kk == 0, 0, g)
    const = lambda a: pl.BlockSpec(a.shape, lambda kk, g: (0,) * a.ndim)
    small = conv_parts + [db_r, an_r, dsk, nw, tri, sel]
    return pl.pallas_call(
        functools.partial(_ssd_kernel, row_blocks=nk, total_blocks=total),
        grid=(total + 1, groups),
        in_specs=[
            pl.BlockSpec((1, tb, gw), lambda kk, g: (*cur(kk), g)),
            pl.BlockSpec((1, tb, gw), lambda kk, g: (*cur(kk), xoff + g)),
            pl.BlockSpec((1, tb, n), lambda kk, g: (*cur(kk), boff + g)),
            pl.BlockSpec((1, tb, n), lambda kk, g: (*cur(kk), coff + g)),
            pl.BlockSpec((1, 1, hpg, tb), lambda kk, g: (cur(kk)[0], g, 0, cur(kk)[1])),
            *[const(a) for a in small],
            pl.BlockSpec((None, inner, dg), lambda kk, g: (layer, 0, g)),
            pl.BlockSpec((1, tb, dg), lambda kk, g: (*prev(kk), out_col(kk, g))),
        ],
        out_specs=pl.BlockSpec((1, tb, dg), lambda kk, g: (*prev(kk), out_col(kk, g))),
        out_shape=jax.ShapeDtypeStruct((b, s, d), F32),
        scratch_shapes=[
            pltpu.VMEM((groups, n, gw), F32),
            pltpu.VMEM((tb, gw), F32),
            pltpu.VMEM((2, groups, tb, gw), BF16),
            pltpu.VMEM((tb + SUBLANES, gw), F32),
            pltpu.VMEM((tb + SUBLANES, n), F32),
            pltpu.VMEM((tb + SUBLANES, n), F32),
            pltpu.VMEM((groups, SUBLANES, gw), F32),
            pltpu.VMEM((groups, SUBLANES, n), F32),
            pltpu.VMEM((groups, SUBLANES, n), F32),
        ],
        compiler_params=_cparams("arbitrary", "arbitrary"),
        name="ssd",
    )(proj, proj, proj, proj, dt_row, *small, w_out, res)


TM = 1024
TN = 1024
SQ_TM = 512
SQ_TN = 2048
FFN_TH = 512
XA_TM = 512
LRU_TB = 1024
HGRN_TB = 2048
SSD_TB = 512


def kernel(x, mem, norm_mix, norm_xattn, norm_ffn, norm_mem, norm_final, ab_w_in, ab_w_out, lru_conv_w, lru_conv_b, lru_w_r, lru_b_r, lru_w_i, lru_b_i, lru_lambda, hgrn_lower_bounds, hgrn_norm, ssd_w_in, ssd_w_out, ssd_conv_w, ssd_conv_b, ssd_dt_bias, ssd_a_log, ssd_d, ssd_norm, xa_w_q, xa_w_kv, xa_w_o, ffn_w_gate, ffn_w_up, ffn_w_down):
    bsz, seq, d = x.shape
    depth = norm_mix.shape[0]
    m = bsz * seq
    lru_w = lru_conv_w.shape[2]
    hgrn_w = hgrn_norm.shape[1]
    ssd_heads = ssd_a_log.shape[1]
    ssd_inner = ssd_norm.shape[1]
    ssd_groups = ssd_heads // SSD_HPG
    ssd_main = ssd_w_in.shape[2] - ssd_heads

    sm = jax.nn.softmax(hgrn_lower_bounds.astype(F32), axis=0)
    lower_bounds = jnp.cumsum(sm, axis=0) - sm[0]
    lru_sp = jax.nn.softplus(-lru_lambda.astype(F32))
    ssd_a_neg = -jnp.exp(ssd_a_log.astype(F32))
    lru_w_ri = jnp.concatenate([lru_w_r, lru_w_i], axis=-1).astype(BF16)
    ab_in_b = _cast_bf16(ab_w_in)
    ab_out_b = _cast_bf16(ab_w_out)
    ssd_in_b = ssd_w_in[:, :, :ssd_main].astype(BF16)
    ssd_dt_b = ssd_w_in[:, :, ssd_main:].astype(BF16)
    ssd_out_b = _cast_bf16(ssd_w_out)
    wq_b = _cast_bf16(xa_w_q)
    wo_b = _cast_bf16(xa_w_o)
    wg_b = _cast_bf16(ffn_w_gate)
    wu_b = _cast_bf16(ffn_w_up)
    wd_b = _cast_bf16(ffn_w_down)

    x2 = x.reshape(m, d)
    mem_len = mem.shape[1]
    kv = _kv_proj(mem.reshape(bsz * mem_len, d), norm_mem, xa_w_kv, TN).reshape(depth, bsz, mem_len, 2 * d)

    for layer in range(depth):
        if layer % 2 == 0:
            e = layer // 2
            proj = _norm_matmul(x2, norm_mix[layer], ab_in_b, e, F32, TM, TN, "ab_in_proj")
            proj = proj.reshape(bsz, seq, -1)
            ya = _lru(proj, lru_conv_w[e], lru_conv_b[e], lru_w_ri[e], lru_b_r[e], lru_b_i[e], lru_sp[e], LRU_TB)
            yb = _hgrn(proj, lower_bounds[e], hgrn_norm[e], 2 * lru_w, hgrn_w // HGRN_DK, HGRN_TB)
            x2 = _out_proj2(ya.reshape(m, lru_w), yb.reshape(m, hgrn_w), ab_out_b, e, x2, SQ_TM, SQ_TN)
        else:
            o = layer // 2
            proj, dt_raw = _norm_matmul_side(x2, norm_mix[layer], ssd_in_b, ssd_dt_b, o, TM, TN, "ssd_in_proj")
            x2 = _ssd(proj.reshape(bsz, seq, ssd_main), dt_raw.reshape(bsz, seq, ssd_heads),
                      ssd_conv_w[o], ssd_conv_b[o], ssd_dt_bias[o], ssd_a_neg[o], ssd_d[o], ssd_norm[o],
                      ssd_out_b, o, x2.reshape(bsz, seq, d), ssd_inner, ssd_groups).reshape(m, d)
        x2 = _xattn(x2.reshape(bsz, seq, d), norm_xattn[layer], wq_b, kv, wo_b, layer, XA_TM).reshape(m, d)
        g_out = norm_final if layer == depth - 1 else None
        x2 = _ffn(x2, norm_ffn[layer], wg_b, wu_b, wd_b, layer, TM, FFN_TH, g_out)
    return x2.reshape(bsz, seq, d)
```
